```python
import math
import jax, jax.numpy as jnp
from jax import lax
import numpy as np

D_MODEL = 4096
BATCH = 4
SEQ = 2048
DEPTH = 2
DEC_BATCH = 8
DEC_SEQ = 8
PAST_LEN = 16384
PAGE_SIZE = 128

N_EVEN = (DEPTH + 1) // 2
N_ODD = DEPTH // 2
HEAD_DIM = 128
EPS = 1e-6
H_A = (3 * D_MODEL // 4) // HEAD_DIM
H_KV_A = H_A // 3
REP_A = H_A // H_KV_A
H_IDX = 32
D_IDX = 128
R_IDX = 512
TOPK_MAX = 256
DSA_Q_BLOCK = 64
B_WIDTH = D_MODEL // 4
B_GROUP = 16
B_GROUPS = B_WIDTH // B_GROUP
B_STATE = 64
C_PAIRS = ((128, 1), (512, 4), (2048, 16))
N_C_GROUPS = 3
H_C_G = H_A // N_C_GROUPS
H_C = H_C_G * N_C_GROUPS
C_WIDTH = H_C * HEAD_DIM
C_OUT = H_C_G * HEAD_DIM
Q_BLOCK = 128
NUM_BUCKETS = 32
MAX_DISTANCE = 2048
D_FF = 11008
D_PLE = 256
Q_A_W = H_A * HEAD_DIM
KV_A_W = 2 * H_KV_A * HEAD_DIM
OFF_KV = Q_A_W
OFF_CQ = OFF_KV + KV_A_W
OFF_W = OFF_CQ + R_IDX
OFF_KI = OFF_W + H_IDX
OFF_U = OFF_KI + D_IDX
IN_EVEN = OFF_U + B_WIDTH
OUT_EVEN = Q_A_W + B_WIDTH

kernel_name = "hybrid_dsa_s5_dilated_decoder_step"


def rms_norm(x, g):
    xf = x.astype(jnp.float32)
    y = xf * lax.rsqrt(jnp.mean(xf * xf, axis=-1, keepdims=True) + EPS)
    return (y * g.astype(jnp.float32)).astype(x.dtype)


def macaron_half(x, g, w1, w3, w2):
    h = rms_norm(x, g)
    return x + 0.5 * ((jax.nn.silu(h @ w1) * (h @ w3)) @ w2)


def ple_add(x, p, g, w_gate, w_proj):
    gate = jax.nn.sigmoid(rms_norm(x, g) @ w_gate)
    return x + gate * (p.astype(x.dtype) @ w_proj)


def t5_bucket(dist):
    max_exact = NUM_BUCKETS // 2
    d = jnp.maximum(dist, 0)
    ratio = jnp.log(jnp.maximum(d, 1).astype(jnp.float32) / max_exact) / math.log(MAX_DISTANCE / max_exact)
    large = jnp.minimum(max_exact + (ratio * (NUM_BUCKETS - max_exact)).astype(jnp.int32), NUM_BUCKETS - 1)
    return jnp.where(d < max_exact, d, large)


def even_project(h, w_in, w_idx_qb):
    B, T, _ = h.shape
    q, kv, cq, wi, ki, u = jnp.split(h @ w_in, [OFF_KV, OFF_CQ, OFF_W, OFF_KI, OFF_U], axis=-1)
    q = q.reshape(B, T, H_A, HEAD_DIM)
    kv = kv.reshape(B, T, 2, H_KV_A, HEAD_DIM)
    qi = (cq @ w_idx_qb).reshape(B, T, H_IDX, D_IDX)
    wi = wi * (H_IDX ** -0.5)
    return q, kv, qi, wi, ki, u


def dsa_indexer_topk(qi, wi, ki, q_pos, k_sel):
    s = jnp.einsum('bthd,bsd->bths', qi.astype(jnp.float32), ki.astype(jnp.float32)) * (D_IDX ** -0.5)
    score = jnp.einsum('bths,bth->bts', jax.nn.relu(s), wi.astype(jnp.float32))
    key_pos = jnp.arange(ki.shape[1])
    score = jnp.where(key_pos[None, None, :] <= q_pos[None, :, None], score, -jnp.inf)
    _, idx = lax.top_k(score, k_sel)
    return idx


def sparse_gqa_attend(q, k_sel, v_sel, idx, q_pos, bias_table):
    B, T = q.shape[:2]
    qg = q.reshape(B, T, H_KV_A, REP_A, HEAD_DIM)
    logits = jnp.einsum('btgrd,btkgd->btgrk', qg, k_sel).astype(jnp.float32) * (HEAD_DIM ** -0.5)
    dist = q_pos[None, :, None] - idx
    bias = bias_table[t5_bucket(dist)].astype(jnp.float32)
    bias = bias.reshape(B, T, -1, H_KV_A, REP_A).transpose(0, 1, 3, 4, 2)
    logits = jnp.where((dist >= 0)[:, :, None, None, :], logits + bias, -jnp.inf)
    p = jax.nn.softmax(logits, axis=-1).astype(v_sel.dtype)
    out = jnp.einsum('btgrk,btkgd->btgrd', p, v_sel)
    return out.reshape(B, T, Q_A_W)


def dsa_prompt(q, kv, qi, wi, ki, bias_table):
    B, T = q.shape[:2]
    k_sel = min(TOPK_MAX, T // 4)
    nb = T // DSA_Q_BLOCK
    bidx = jnp.arange(B)[:, None, None]

    def block(args):
        qb, qib, wib, pos = args
        idx = dsa_indexer_topk(qib, wib, ki, pos, k_sel)
        kv_sel = kv[bidx, idx]
        return sparse_gqa_attend(qb, kv_sel[:, :, :, 0], kv_sel[:, :, :, 1], idx, pos, bias_table)

    def to_blocks(a):
        return a.reshape(B, nb, DSA_Q_BLOCK, *a.shape[2:]).swapaxes(0, 1)

    out = lax.map(block, (to_blocks(q), to_blocks(qi), to_blocks(wi), jnp.arange(T).reshape(nb, DSA_Q_BLOCK)))
    return out.swapaxes(0, 1).reshape(B, T, Q_A_W)


def dsa_sample(q, kv_new, qi, wi, ki_new, pool_kv, pool_kidx, page_table, bias_table):
    Bd, T = q.shape[:2]
    k_sel = min(TOPK_MAX, (PAST_LEN + T) // 4)
    ki_past = pool_kidx[page_table].reshape(Bd, PAST_LEN, D_IDX).astype(ki_new.dtype)
    ki_all = jnp.concatenate([ki_past, ki_new], axis=1)
    q_pos = PAST_LEN + jnp.arange(T)
    idx = dsa_indexer_topk(qi, wi, ki_all, q_pos, k_sel)
    in_past = idx < PAST_LEN
    pidx = jnp.minimum(idx, PAST_LEN - 1)
    phys = jnp.take_along_axis(page_table, (pidx // PAGE_SIZE).reshape(Bd, -1), axis=1).reshape(idx.shape)
    kv_past = pool_kv[phys, pidx % PAGE_SIZE].astype(kv_new.dtype)
    kv_cur = kv_new[jnp.arange(Bd)[:, None, None], jnp.clip(idx - PAST_LEN, 0, T - 1)]
    kv_sel = jnp.where(in_past[..., None, None, None], kv_past, kv_cur)
    return sparse_gqa_attend(q, kv_sel[:, :, :, 0], kv_sel[:, :, :, 1], idx, q_pos, bias_table)


def s5_block(u, h0_re, h0_im, lam_re, lam_im, log_step, b_re, b_im, c_re, c_im, d_skip, w_glu, b_glu):
    f32 = jnp.float32
    Bn, T, _ = u.shape
    uf = u.astype(f32)
    ug = uf.reshape(Bn, T, B_GROUPS, B_GROUP)
    lr, li = lam_re.astype(f32), lam_im.astype(f32)
    step = jnp.exp(log_step.astype(f32))[:, None]
    mag = jnp.exp(lr * step)
    ab_re, ab_im = mag * jnp.cos(li * step), mag * jnp.sin(li * step)
    den = lr * lr + li * li
    nr, ni = ab_re - 1.0, ab_im
    f_re, f_im = (nr * lr + ni * li) / den, (ni * lr - nr * li) / den
    br, bi = b_re.astype(f32), b_im.astype(f32)
    bb_re = f_re[..., None] * br - f_im[..., None] * bi
    bb_im = f_re[..., None] * bi + f_im[..., None] * br
    x_re = jnp.einsum('btgc,gnc->btgn', ug, bb_re)
    x_im = jnp.einsum('btgc,gnc->btgn', ug, bb_im)
    hr0, hi0 = h0_re.astype(f32), h0_im.astype(f32)
    x_re = x_re.at[:, 0].add(ab_re * hr0 - ab_im * hi0)
    x_im = x_im.at[:, 0].add(ab_re * hi0 + ab_im * hr0)
    a_re = jnp.broadcast_to(ab_re, x_re.shape)
    a_im = jnp.broadcast_to(ab_im, x_re.shape)

    def combine(e1, e2):
        a1r, a1i, b1r, b1i = e1
        a2r, a2i, b2r, b2i = e2
        return (a2r * a1r - a2i * a1i, a2r * a1i + a2i * a1r,
                a2r * b1r - a2i * b1i + b2r, a2r * b1i + a2i * b1r + b2i)

    _, _, h_re, h_im = lax.associative_scan(combine, (a_re, a_im, x_re, x_im), axis=1)
    y = (jnp.einsum('btgn,gcn->btgc', h_re, c_re.astype(f32))
         - jnp.einsum('btgn,gcn->btgc', h_im, c_im.astype(f32))).reshape(Bn, T, B_WIDTH)
    y = y + d_skip.astype(f32) * uf
    g = jax.nn.gelu(y)
    out = g * jax.nn.sigmoid(g @ w_glu.astype(f32) + b_glu.astype(f32))
    return out.astype(u.dtype), h_re[:, -1].astype(u.dtype), h_im[:, -1].astype(u.dtype)


def odd_project(h, w_in):
    B, T, _ = h.shape
    z = (h @ w_in).reshape(B, T, 3, H_C, HEAD_DIM)
    return z[:, :, 0], z[:, :, 1:]


def group_slice(a, g):
    return a[..., g * H_C_G:(g + 1) * H_C_G, :]


def dilated_group_attend(q, kv_ext, q_ext_idx, window, dilation, bias_cols):
    n_keys = window // dilation + 1
    offs = dilation * jnp.arange(n_keys)
    e = q_ext_idx[:, None] - offs[None, :]
    valid = e >= 0
    kv_sel = kv_ext[:, jnp.maximum(e, 0)]
    logits = jnp.einsum('bthd,btjhd->bthj', q, kv_sel[:, :, :, 0]).astype(jnp.float32) * (HEAD_DIM ** -0.5)
    bias = bias_cols[t5_bucket(offs)].T.astype(jnp.float32)
    logits = jnp.where(valid[None, :, None, :], logits + bias[None, None], -jnp.inf)
    lse = jax.nn.logsumexp(logits, axis=-1)
    p = jnp.exp(logits - lse[..., None]).astype(kv_sel.dtype)
    out = jnp.einsum('bthj,btjhd->bthd', p, kv_sel[:, :, :, 1])
    return out, lse


def dilated_combine(q, kv_exts, q_ext_idxs, bias_table):
    outs, lses = [], []
    for g, (window, dilation) in enumerate(C_PAIRS):
        o, l = dilated_group_attend(group_slice(q, g), kv_exts[g], q_ext_idxs[g], window, dilation,
                                    bias_table[:, g * H_C_G:(g + 1) * H_C_G])
        outs.append(o)
        lses.append(l)
    alpha = jax.nn.softmax(jnp.stack(lses), axis=0)
    out = jnp.sum(alpha[..., None] * jnp.stack(outs).astype(jnp.float32), axis=0)
    B, T = q.shape[:2]
    return out.reshape(B, T, C_OUT).astype(q.dtype)


def dilated_prompt(q, kv, bias_table):
    B, T = q.shape[:2]
    nb = T // Q_BLOCK
    kv_groups = [group_slice(kv, g) for g in range(N_C_GROUPS)]

    def block(args):
        qb, pos = args
        return dilated_combine(qb, kv_groups, [pos] * N_C_GROUPS, bias_table)

    out = lax.map(block, (q.reshape(B, nb, Q_BLOCK, H_C, HEAD_DIM).swapaxes(0, 1),
                          jnp.arange(T).reshape(nb, Q_BLOCK)))
    rows = [kv_groups[g][:, T - min(w, T):] for g, (w, _) in enumerate(C_PAIRS)]
    return out.swapaxes(0, 1).reshape(B, T, C_OUT), rows


def dilated_sample(q, kv_new, bufs, bias_table):
    T = q.shape[1]
    kv_exts, idxs, rows = [], [], []
    for g in range(N_C_GROUPS):
        new_g = group_slice(kv_new, g)
        wb = bufs[g].shape[1]
        kv_exts.append(jnp.concatenate([bufs[g].astype(kv_new.dtype), new_g], axis=1))
        idxs.append(wb + jnp.arange(T))
        rows.append(new_g)
    return dilated_combine(q, kv_exts, idxs, bias_table), rows


def setup_inputs(seed: int = 0) -> dict:
    key = jax.random.key(seed)
    ks = list(jax.random.split(key, 48))
    f32 = jnp.float32

    def nrm(shape, scale):
        return jax.random.normal(ks.pop(), shape, f32) * scale

    n_pages = PAST_LEN // PAGE_SIZE
    used = DEC_BATCH * n_pages
    n_pool = used + max(1, used // 4)
    page_table = jax.random.permutation(ks.pop(), n_pool)[:used].reshape(DEC_BATCH, n_pages).astype(jnp.int32)
    return {
        'x_prompt': nrm((BATCH, SEQ, D_MODEL), 1.0),
        'x_sample': nrm((DEC_BATCH, DEC_SEQ, D_MODEL), 1.0),
        'cache_a_kv': nrm((N_EVEN, n_pool, PAGE_SIZE, 2, H_KV_A, HEAD_DIM), 1.0),
        'cache_a_kidx': nrm((N_EVEN, n_pool, PAGE_SIZE, D_IDX), 1.0),
        'state_b_re': nrm((N_EVEN, DEC_BATCH, B_GROUPS, B_STATE), 0.5),
        'state_b_im': nrm((N_EVEN, DEC_BATCH, B_GROUPS, B_STATE), 0.5),
        'cache_c0_kv': nrm((N_ODD, DEC_BATCH, min(C_PAIRS[0][0], PAST_LEN), 2, H_C_G, HEAD_DIM), 1.0),
        'cache_c1_kv': nrm((N_ODD, DEC_BATCH, min(C_PAIRS[1][0], PAST_LEN), 2, H_C_G, HEAD_DIM), 1.0),
        'cache_c2_kv': nrm((N_ODD, DEC_BATCH, min(C_PAIRS[2][0], PAST_LEN), 2, H_C_G, HEAD_DIM), 1.0),
        'page_table': page_table,
        'p_prompt': nrm((DEPTH, BATCH, SEQ, D_PLE), 1.0),
        'p_sample': nrm((DEPTH, DEC_BATCH, DEC_SEQ, D_PLE), 1.0),
        'bias_table': nrm((NUM_BUCKETS, H_A), 0.5),
        'norm_g': 1.0 + nrm((DEPTH, 4, D_MODEL), 0.01),
        'final_g': 1.0 + nrm((D_MODEL,), 0.01),
        'ffn1_w1': nrm((DEPTH, D_MODEL, D_FF), D_MODEL ** -0.5),
        'ffn1_w3': nrm((DEPTH, D_MODEL, D_FF), D_MODEL ** -0.5),
        'ffn1_w2': nrm((DEPTH, D_FF, D_MODEL), D_FF ** -0.5),
        'ffn2_w1': nrm((DEPTH, D_MODEL, D_FF), D_MODEL ** -0.5),
        'ffn2_w3': nrm((DEPTH, D_MODEL, D_FF), D_MODEL ** -0.5),
        'ffn2_w2': nrm((DEPTH, D_FF, D_MODEL), D_FF ** -0.5),
        'ple_gate': nrm((DEPTH, D_MODEL, D_MODEL), D_MODEL ** -0.5),
        'ple_proj': nrm((DEPTH, D_PLE, D_MODEL), D_PLE ** -0.5),
        'w_in_even': nrm((N_EVEN, D_MODEL, IN_EVEN), D_MODEL ** -0.5),
        'w_idx_qb': nrm((N_EVEN, R_IDX, H_IDX * D_IDX), R_IDX ** -0.5),
        'w_out_even': nrm((N_EVEN, OUT_EVEN, D_MODEL), OUT_EVEN ** -0.5),
        's5_lam_re': -0.5 + nrm((N_EVEN, B_GROUPS, B_STATE), 0.01),
        's5_lam_im': jnp.pi * jnp.arange(B_STATE, dtype=f32) + nrm((N_EVEN, B_GROUPS, B_STATE), 0.01),
        's5_log_step': jax.random.uniform(ks.pop(), (N_EVEN, B_GROUPS), f32, math.log(1e-3), math.log(1e-1)),
        's5_b_re': nrm((N_EVEN, B_GROUPS, B_STATE, B_GROUP), 0.5),
        's5_b_im': nrm((N_EVEN, B_GROUPS, B_STATE, B_GROUP), 0.5),
        's5_c_re': nrm((N_EVEN, B_GROUPS, B_GROUP, B_STATE), (2 * B_STATE) ** -0.5),
        's5_c_im': nrm((N_EVEN, B_GROUPS, B_GROUP, B_STATE), (2 * B_STATE) ** -0.5),
        's5_d': nrm((N_EVEN, B_WIDTH), 0.5),
        's5_w_glu': nrm((N_EVEN, B_WIDTH, B_WIDTH), B_WIDTH ** -0.5),
        's5_b_glu': nrm((N_EVEN, B_WIDTH), 0.01),
        'w_in_odd': nrm((N_ODD, D_MODEL, 3 * C_WIDTH), D_MODEL ** -0.5),
        'w_out_odd': nrm((N_ODD, C_OUT, D_MODEL), C_OUT ** -0.5),
    }


def reference(x_prompt, x_sample, cache_a_kv, cache_a_kidx, state_b_re, state_b_im,
              cache_c0_kv, cache_c1_kv, cache_c2_kv, page_table, p_prompt, p_sample,
              bias_table, norm_g, final_g, ffn1_w1, ffn1_w3, ffn1_w2, ffn2_w1, ffn2_w3, ffn2_w2,
              ple_gate, ple_proj, w_in_even, w_idx_qb, w_out_even, s5_lam_re, s5_lam_im, s5_log_step,
              s5_b_re, s5_b_im, s5_c_re, s5_c_im, s5_d, s5_w_glu, s5_b_glu, w_in_odd, w_out_odd):
    xp, xs = x_prompt, x_sample
    a_kv_p, a_ki_p, b_re_p, b_im_p = [], [], [], []
    a_kv_s, a_ki_s, b_re_s, b_im_s = [], [], [], []
    c_p = [[], [], []]
    c_s = [[], [], []]
    for i in range(DEPTH):
        li = i // 2
        xp = macaron_half(xp, norm_g[i, 0], ffn1_w1[i], ffn1_w3[i], ffn1_w2[i])
        xs = macaron_half(xs, norm_g[i, 0], ffn1_w1[i], ffn1_w3[i], ffn1_w2[i])
        hp = rms_norm(xp, norm_g[i, 1])
        hs = rms_norm(xs, norm_g[i, 1])
        if i % 2 == 0:
            s5w = (s5_lam_re[li], s5_lam_im[li], s5_log_step[li], s5_b_re[li], s5_b_im[li],
                   s5_c_re[li], s5_c_im[li], s5_d[li], s5_w_glu[li], s5_b_glu[li])
            q, kv, qi, wi, ki, u = even_project(hp, w_in_even[li], w_idx_qb[li])
            att = dsa_prompt(q, kv, qi, wi, ki, bias_table)
            h0 = jnp.zeros((xp.shape[0], B_GROUPS, B_STATE), jnp.float32)
            ssm, hre, him = s5_block(u, h0, h0, *s5w)
            mp = jnp.concatenate([att, ssm], axis=-1) @ w_out_even[li]
            a_kv_p.append(kv)
            a_ki_p.append(ki)
            b_re_p.append(hre)
            b_im_p.append(him)
            q, kv, qi, wi, ki, u = even_project(hs, w_in_even[li], w_idx_qb[li])
            att = dsa_sample(q, kv, qi, wi, ki, cache_a_kv[li], cache_a_kidx[li], page_table, bias_table)
            ssm, hre, him = s5_block(u, state_b_re[li], state_b_im[li], *s5w)
            ms = jnp.concatenate([att, ssm], axis=-1) @ w_out_even[li]
            a_kv_s.append(kv)
            a_ki_s.append(ki)
            b_re_s.append(hre)
            b_im_s.append(him)
        else:
            q, kv = odd_project(hp, w_in_odd[li])
            att, rows = dilated_prompt(q, kv, bias_table)
            mp = att @ w_out_odd[li]
            for g in range(N_C_GROUPS):
                c_p[g].append(rows[g])
            q, kv = odd_project(hs, w_in_odd[li])
            att, rows = dilated_sample(q, kv, (cache_c0_kv[li], cache_c1_kv[li], cache_c2_kv[li]), bias_table)
            ms = att @ w_out_odd[li]
            for g in range(N_C_GROUPS):
                c_s[g].append(rows[g])
        xp = xp + mp
        xs = xs + ms
        xp = macaron_half(xp, norm_g[i, 2], ffn2_w1[i], ffn2_w3[i], ffn2_w2[i])
        xs = macaron_half(xs, norm_g[i, 2], ffn2_w1[i], ffn2_w3[i], ffn2_w2[i])
        xp = ple_add(xp, p_prompt[i], norm_g[i, 3], ple_gate[i], ple_proj[i])
        xs = ple_add(xs, p_sample[i], norm_g[i, 3], ple_gate[i], ple_proj[i])
    y_prompt = rms_norm(xp, final_g)
    y_sample = rms_norm(xs, final_g)
    return (y_prompt, y_sample,
            jnp.stack(a_kv_p), jnp.stack(a_ki_p), jnp.stack(b_re_p), jnp.stack(b_im_p),
            jnp.stack(c_p[0]), jnp.stack(c_p[1]), jnp.stack(c_p[2]),
            jnp.stack(a_kv_s), jnp.stack(a_ki_s), jnp.stack(b_re_s), jnp.stack(b_im_s),
            jnp.stack(c_s[0]), jnp.stack(c_s[1]), jnp.stack(c_s[2]))
```

```python
import functools
import math

import jax
import jax.numpy as jnp
from jax import lax
from jax.experimental import pallas as pl
from jax.experimental.pallas import tpu as pltpu

D_MODEL = 4096
DEPTH = 2
PAST_LEN = 16384
PAGE_SIZE = 128
HEAD_DIM = 128
EPS = 1e-6
H_A = 24
H_KV_A = 8
H_IDX = 32
D_IDX = 128
R_IDX = 512
TOPK_MAX = 256
B_WIDTH = 1024
B_GROUP = 16
B_GROUPS = 64
B_STATE = 64
C_PAIRS = ((128, 1), (512, 4), (2048, 16))
N_C_GROUPS = 3
H_C_G = 8
H_C = 24
C_WIDTH = H_C * HEAD_DIM
C_OUT = H_C_G * HEAD_DIM
NUM_BUCKETS = 32
MAX_DISTANCE = 2048
D_FF = 11008
D_PLE = 256
Q_A_W = H_A * HEAD_DIM
KV_A_W = 2 * H_KV_A * HEAD_DIM
OFF_KV = Q_A_W
OFF_CQ = OFF_KV + KV_A_W
OFF_W = OFF_CQ + R_IDX
OFF_KI = OFF_W + H_IDX
OFF_U = OFF_KI + D_IDX
IN_EVEN = OFF_U + B_WIDTH

LANES = 128
SUBLANES = 8
VMEM_LIMIT = 56 * 1024 * 1024

S5_SEG = SUBLANES
S5_CHUNK_GROUPS = 8
PAGES_PER_STEP = 8
NEG_INIT = -1e30

F32 = jnp.float32
BF16 = jnp.bfloat16
NT_DIMS = (((1,), (1,)), ((), ()))


def _params(*sem):
    return pltpu.CompilerParams(dimension_semantics=sem, vmem_limit_bytes=VMEM_LIMIT)


def _rms_kernel(x_ref, g_ref, o_ref):
    x = x_ref[...]
    ms = jnp.mean(x * x, axis=-1, keepdims=True)
    o_ref[...] = (x * lax.rsqrt(ms + EPS) * g_ref[...]).astype(o_ref.dtype)


def rmsnorm(x, g, out_dtype):
    m, d = x.shape
    tm = min(m, 256)
    return pl.pallas_call(
        _rms_kernel,
        grid=(m // tm,),
        in_specs=[pl.BlockSpec((tm, d), lambda i: (i, 0)), pl.BlockSpec((1, d), lambda i: (0, 0))],
        out_specs=pl.BlockSpec((tm, d), lambda i: (i, 0)),
        out_shape=jax.ShapeDtypeStruct((m, d), out_dtype),
        compiler_params=_params("parallel"),
        name="rmsnorm",
    )(x, g.reshape(1, d))


def _linear_kernel(*refs, mode, n_out, scale):
    a_ref, w_ref = refs[0], refs[1]
    extras = refs[2:len(refs) - n_out]
    outs = refs[len(refs) - n_out:]
    acc = jnp.dot(a_ref[...].astype(BF16), w_ref[...].astype(BF16), preferred_element_type=F32)
    if mode == "none":
        res = acc
    elif mode == "resid":
        res = extras[0][...] + scale * acc
    elif mode == "glu":
        g = extras[0][...]
        res = g * jax.nn.sigmoid(acc + extras[1][...])
    elif mode == "ple":
        x_ref, p_ref, wp_ref = extras
        proj = jnp.dot(p_ref[...].astype(BF16), wp_ref[...].astype(BF16), preferred_element_type=F32)
        res = x_ref[...] + jax.nn.sigmoid(acc) * proj
    else:
        raise ValueError(mode)
    for o in outs:
        o[...] = res.astype(o.dtype)


def linear(a, w, *, n_cols=None, col_off=0, row_off=0, tn, mode="none", extras=(),
           out_dtypes=(F32,), scale=1.0):
    m, k = a.shape
    n_cols = w.shape[1] if n_cols is None else n_cols
    tm = min(m, 1024)
    assert m % tm == 0 and n_cols % tn == 0 and col_off % tn == 0 and row_off % k == 0
    jo, ro = col_off // tn, row_off // k
    tile = pl.BlockSpec((tm, tn), lambda i, j: (i, j))
    in_specs = [pl.BlockSpec((tm, k), lambda i, j: (i, 0)),
                pl.BlockSpec((k, tn), lambda i, j: (ro, j + jo))]
    if mode == "resid":
        in_specs += [tile]
    elif mode == "glu":
        in_specs += [tile, pl.BlockSpec((1, tn), lambda i, j: (0, j))]
    elif mode == "ple":
        kp = extras[1].shape[1]
        in_specs += [tile, pl.BlockSpec((tm, kp), lambda i, j: (i, 0)),
                     pl.BlockSpec((kp, tn), lambda i, j: (0, j))]
    res = pl.pallas_call(
        functools.partial(_linear_kernel, mode=mode, n_out=len(out_dtypes), scale=scale),
        grid=(m // tm, n_cols // tn),
        in_specs=in_specs,
        out_specs=[tile] * len(out_dtypes),
        out_shape=[jax.ShapeDtypeStruct((m, n_cols), dt) for dt in out_dtypes],
        compiler_params=_params("parallel", "arbitrary"),
        name="linear_" + mode,
    )(a, w, *extras)
    return res if len(out_dtypes) > 1 else res[0]


def _swiglu_kernel(a_ref, w1_ref, w3_ref, o_ref):
    a = a_ref[...]
    g = jnp.dot(a, w1_ref[...].astype(BF16), preferred_element_type=F32)
    u = jnp.dot(a, w3_ref[...].astype(BF16), preferred_element_type=F32)
    o_ref[...] = (jax.nn.silu(g) * u).astype(o_ref.dtype)


def swiglu_up(h, w1, w3, *, tn=256):
    m, k = h.shape
    n = w1.shape[1]
    tm = min(m, 1024)
    return pl.pallas_call(
        _swiglu_kernel,
        grid=(m // tm, n // tn),
        in_specs=[pl.BlockSpec((tm, k), lambda i, j: (i, 0)),
                  pl.BlockSpec((k, tn), lambda i, j: (0, j)),
                  pl.BlockSpec((k, tn), lambda i, j: (0, j))],
        out_specs=pl.BlockSpec((tm, tn), lambda i, j: (i, j)),
        out_shape=jax.ShapeDtypeStruct((m, n), BF16),
        compiler_params=_params("parallel", "arbitrary"),
        name="swiglu_up",
    )(h, w1, w3)


def _down_kernel(a_ref, w_ref, x_ref, o_ref, *, scale):
    @pl.when(pl.program_id(2) == 0)
    def _():
        o_ref[...] = x_ref[...]

    o_ref[...] += scale * jnp.dot(a_ref[...], w_ref[...].astype(BF16), preferred_element_type=F32)


def down_resid(act, w, x, *, scale, tk=256):
    m, k = act.shape
    n = w.shape[1]
    tm = min(m, 2048)
    tn = min(n, 1024)
    return pl.pallas_call(
        functools.partial(_down_kernel, scale=scale),
        grid=(m // tm, n // tn, k // tk),
        in_specs=[pl.BlockSpec((tm, tk), lambda i, j, kk: (i, kk)),
                  pl.BlockSpec((tk, tn), lambda i, j, kk: (kk, j)),
                  pl.BlockSpec((tm, tn), lambda i, j, kk: (i, j))],
        out_specs=pl.BlockSpec((tm, tn), lambda i, j, kk: (i, j)),
        out_shape=jax.ShapeDtypeStruct((m, n), F32),
        compiler_params=_params("parallel", "parallel", "arbitrary"),
        name="down_resid",
    )(act, w, x)


def macaron_half(x, g, w1, w3, w2):
    h = rmsnorm(x, g, BF16)
    act = swiglu_up(h, w1, w3)
    return down_resid(act, w2, x, scale=0.5)


def ple_add(x, p, g, w_gate, w_proj):
    h = rmsnorm(x, g, BF16)
    return linear(h, w_gate, tn=256, mode="ple", extras=(x, p.astype(BF16), w_proj))


def t5_bucket(dist):
    max_exact = NUM_BUCKETS // 2
    d = jnp.maximum(dist, 0)
    ratio = jnp.log(jnp.maximum(d, 1).astype(F32) / max_exact) / math.log(MAX_DISTANCE / max_exact)
    large = jnp.minimum(max_exact + (ratio * (NUM_BUCKETS - max_exact)).astype(jnp.int32), NUM_BUCKETS - 1)
    return jnp.where(d < max_exact, d, large)


def bias_of_dist(bias_cols, dist, valid):
    b = bias_cols[t5_bucket(dist)].astype(F32)
    b = jnp.where(valid[..., None], b, -jnp.inf)
    return jnp.moveaxis(b, -1, 0)


def _flash_init(m_ref, l_ref, acc_ref):
    m_ref[...] = jnp.full(m_ref.shape, NEG_INIT, F32)
    l_ref[...] = jnp.zeros(l_ref.shape, F32)
    acc_ref[...] = jnp.zeros(acc_ref.shape, F32)


def _flash_step(q, k, v, bias_at, sel, m_ref, l_ref, acc_ref, *, hq, hkv):
    rep = hq // hkv
    scale = HEAD_DIM ** -0.5
    for h in range(hq):
        g = h // rep
        hs = slice(h * HEAD_DIM, (h + 1) * HEAD_DIM)
        gs = slice(g * HEAD_DIM, (g + 1) * HEAD_DIM)
        s = lax.dot_general(q[:, hs], k[:, gs], NT_DIMS, preferred_element_type=F32) * scale + bias_at(h)
        if sel is not None:
            s = jnp.where(sel, s, -jnp.inf)
        m_old = m_ref[h]
        m_new = jnp.maximum(m_old, jnp.max(s, axis=1, keepdims=True))
        alpha = jnp.exp(m_old - m_new)
        p = jnp.exp(s - m_new[:, :1])
        l_ref[h] = alpha * l_ref[h] + jnp.sum(p, axis=1, keepdims=True)
        acc_ref[:, hs] = alpha * acc_ref[:, hs] + jnp.dot(p.astype(BF16), v[:, gs], preferred_element_type=F32)
        m_ref[h] = m_new


def _flash_finish(o_ref, lse_ref, m_ref, l_ref, acc_ref, *, hq):
    for h in range(hq):
        hs = slice(h * HEAD_DIM, (h + 1) * HEAD_DIM)
        l = l_ref[h]
        o_ref[:, hs] = (acc_ref[:, hs] / l).astype(o_ref.dtype)
        if lse_ref is not None:
            lse_ref[:, hs] = m_ref[h] + jnp.log(l)


def _flash_scratch(tq, hq):
    return [pltpu.VMEM((hq, tq, LANES), F32), pltpu.VMEM((hq, tq, LANES), F32),
            pltpu.VMEM((tq, hq * HEAD_DIM), F32)]


def _prompt_attn_kernel(*refs, hq, hkv, has_sel, with_lse, first_kb):
    q_ref, k_ref, v_ref, bias_ref = refs[:4]
    pos = 4
    sel_ref = None
    if has_sel:
        sel_ref = refs[pos]
        pos += 1
    o_ref = refs[pos]
    pos += 1
    lse_ref = None
    if with_lse:
        lse_ref = refs[pos]
        pos += 1
    m_ref, l_ref, acc_ref = refs[pos:pos + 3]
    qb = pl.program_id(2)
    kk = pl.program_id(3)
    nk = pl.num_programs(3)

    @pl.when(kk == 0)
    def _():
        _flash_init(m_ref, l_ref, acc_ref)

    @pl.when(first_kb(qb, kk) >= 0)
    def _():
        sel = None if sel_ref is None else sel_ref[...] > 0.0
        _flash_step(q_ref[...].astype(BF16), k_ref[...].astype(BF16), v_ref[...].astype(BF16),
                    lambda h: bias_ref[h], sel, m_ref, l_ref, acc_ref, hq=hq, hkv=hkv)

    @pl.when(kk == nk - 1)
    def _():
        _flash_finish(o_ref, lse_ref, m_ref, l_ref, acc_ref, hq=hq)


def dsa_prompt_attention(q, kvb, bias_lib, sel, *, batch, seq, tq):
    nq = seq // tq
    grid = (batch, 1, nq, nq)

    def kb_of(qb, kk):
        return jnp.minimum(kk, qb)

    in_specs = [
        pl.BlockSpec((tq, Q_A_W), lambda b, r, qb, kk: (b * nq + qb, 0)),
        pl.BlockSpec((tq, H_KV_A * HEAD_DIM), lambda b, r, qb, kk: (b * nq + kb_of(qb, kk), 0)),
        pl.BlockSpec((tq, H_KV_A * HEAD_DIM), lambda b, r, qb, kk: (b * nq + kb_of(qb, kk), 1)),
        pl.BlockSpec((H_A, None, tq, tq), lambda b, r, qb, kk: (0, qb - kb_of(qb, kk), 0, 0)),
        pl.BlockSpec((tq, tq), lambda b, r, qb, kk: (b * nq + qb, kb_of(qb, kk))),
    ]
    return pl.pallas_call(
        functools.partial(_prompt_attn_kernel, hq=H_A, hkv=H_KV_A, has_sel=True, with_lse=False,
                          first_kb=lambda qb, kk: qb - kk),
        grid=grid,
        in_specs=in_specs,
        out_specs=pl.BlockSpec((tq, Q_A_W), lambda b, r, qb, kk: (b * nq + qb, 0)),
        out_shape=jax.ShapeDtypeStruct((batch * seq, Q_A_W), BF16),
        scratch_shapes=_flash_scratch(tq, H_A),
        compiler_params=_params("parallel", "arbitrary", "arbitrary", "arbitrary"),
        name="dsa_prompt_attention",
    )(q, kvb, kvb, bias_lib, sel)


def dilated_prompt_group(q, kvb, bias_lib, *, batch, seq, g, dilation, tq):
    tl = seq // dilation
    nq = tl // tq
    qv = q.reshape(batch, tl, dilation * C_WIDTH)
    kvv = kvb.reshape(batch, tl, dilation * 2 * C_WIDTH)
    qcols, kvcols = C_WIDTH // C_OUT, 2 * C_WIDTH // C_OUT

    def kb_of(qb, kk):
        return jnp.maximum(qb - 1 + kk, 0)

    in_specs = [
        pl.BlockSpec((None, tq, C_OUT), lambda b, r, qb, kk: (b, qb, r * qcols + g)),
        pl.BlockSpec((None, tq, C_OUT), lambda b, r, qb, kk: (b, kb_of(qb, kk), r * kvcols + g)),
        pl.BlockSpec((None, tq, C_OUT), lambda b, r, qb, kk: (b, kb_of(qb, kk), r * kvcols + N_C_GROUPS + g)),
        pl.BlockSpec((H_C_G, None, tq, tq), lambda b, r, qb, kk: (0, 1 - kk, 0, 0)),
    ]
    out_spec = pl.BlockSpec((None, tq, C_OUT), lambda b, r, qb, kk: (b, qb, r))
    o, lse = pl.pallas_call(
        functools.partial(_prompt_attn_kernel, hq=H_C_G, hkv=H_C_G, has_sel=False, with_lse=True,
                          first_kb=lambda qb, kk: qb - 1 + kk),
        grid=(batch, dilation, nq, 2),
        in_specs=in_specs,
        out_specs=[out_spec, out_spec],
        out_shape=[jax.ShapeDtypeStruct((batch, tl, dilation * C_OUT), F32)] * 2,
        scratch_shapes=_flash_scratch(tq, H_C_G),
        compiler_params=_params("parallel", "parallel", "arbitrary", "arbitrary"),
        name="dilated_prompt_group",
    )(qv, kvv, kvv, bias_lib)
    return o.reshape(batch * seq, C_OUT), lse.reshape(batch * seq, C_OUT)


def _sample_attn_kernel(*refs, hq, hkv, n_src, has_sel, with_lse, has_table):
    if has_table:
        refs = refs[1:]
    q_ref = refs[0]
    src_refs = refs[1:1 + n_src]
    new_ref, bias_ref = refs[1 + n_src], refs[2 + n_src]
    pos = 3 + n_src
    sel_ref = None
    if has_sel:
        sel_ref = refs[pos]
        pos += 1
    o_ref = refs[pos]
    pos += 1
    lse_ref = None
    if with_lse:
        lse_ref = refs[pos]
        pos += 1
    m_ref, l_ref, acc_ref = refs[pos:pos + 3]
    kb = pl.program_id(1)
    nk = pl.num_programs(1)
    kw = hkv * HEAD_DIM
    new_rows = new_ref.shape[0]

    @pl.when(kb == 0)
    def _():
        _flash_init(m_ref, l_ref, acc_ref)

    q = q_ref[...].astype(BF16)

    @pl.when(kb < nk - 1)
    def _():
        kv = jnp.concatenate([r[...] for r in src_refs], axis=0) if n_src > 1 else src_refs[0][...]
        sel = None if sel_ref is None else sel_ref[...] > 0.0
        _flash_step(q, kv[:, :kw].astype(BF16), kv[:, kw:].astype(BF16), lambda h: bias_ref[h], sel,
                    m_ref, l_ref, acc_ref, hq=hq, hkv=hkv)

    @pl.when(kb == nk - 1)
    def _():
        kv = new_ref[...]
        sel = None if sel_ref is None else sel_ref[:, :new_rows] > 0.0
        _flash_step(q, kv[:, :kw].astype(BF16), kv[:, kw:].astype(BF16), lambda h: bias_ref[h, :, :new_rows],
                    sel, m_ref, l_ref, acc_ref, hq=hq, hkv=hkv)
        _flash_finish(o_ref, lse_ref, m_ref, l_ref, acc_ref, hq=hq)


def sample_attention(q, src, new_kv, bias, sel, *, hq, hkv, page_table=None, tk, with_lse):
    bsz, t, _ = q.shape
    kvw = 2 * hkv * HEAD_DIM
    new_rows = new_kv.shape[1]
    paged = page_table is not None
    if paged:
        n_src = tk // PAGE_SIZE
        nk = page_table.shape[1] // n_src
        src_specs = [pl.BlockSpec((None, PAGE_SIZE, kvw),
                                  functools.partial(lambda p, b, kb, pt: (pt[b, jnp.minimum(kb, nk - 1) * n_src + p], 0, 0), p))
                     for p in range(n_src)]
        ix = lambda f: (lambda b, kb, pt: f(b, kb))
    else:
        n_src = 1
        nk = src.shape[1] // tk
        src_specs = [pl.BlockSpec((None, tk, kvw), lambda b, kb: (b, jnp.minimum(kb, nk - 1), 0))]
        ix = lambda f: f
    in_specs = ([pl.BlockSpec((None, t, hq * HEAD_DIM), ix(lambda b, kb: (b, 0, 0)))] + src_specs +
                [pl.BlockSpec((None, new_rows, kvw), ix(lambda b, kb: (b, 0, 0))),
                 pl.BlockSpec((hq, t, tk), ix(lambda b, kb: (0, 0, kb)))])
    args = [q] + [src] * n_src + [new_kv, bias]
    if sel is not None:
        in_specs.append(pl.BlockSpec((None, t, tk), ix(lambda b, kb: (b, 0, kb))))
        args.append(sel)
    out_spec = pl.BlockSpec((None, t, hq * HEAD_DIM), ix(lambda b, kb: (b, 0, 0)))
    n_out = 2 if with_lse else 1
    kern = functools.partial(_sample_attn_kernel, hq=hq, hkv=hkv, n_src=n_src, has_sel=sel is not None,
                             with_lse=with_lse, has_table=paged)
    out_shape = [jax.ShapeDtypeStruct((bsz, t, hq * HEAD_DIM), F32)] * n_out
    scratch = _flash_scratch(t, hq)
    cp = _params("parallel", "arbitrary")
    if paged:
        gs = pltpu.PrefetchScalarGridSpec(num_scalar_prefetch=1, grid=(bsz, nk + 1), in_specs=in_specs,
                                          out_specs=[out_spec] * n_out, scratch_shapes=scratch)
        res = pl.pallas_call(kern, grid_spec=gs, out_shape=out_shape, compiler_params=cp,
                             name="sample_attention_paged")(page_table, *args)
    else:
        res = pl.pallas_call(kern, grid=(bsz, nk + 1), in_specs=in_specs, out_specs=[out_spec] * n_out,
                             out_shape=out_shape, scratch_shapes=scratch, compiler_params=cp,
                             name="sample_attention")(*args)
    return res if with_lse else res[0]


IDX_SCALE = (H_IDX ** -0.5) * (D_IDX ** -0.5)


def _idx_prompt_kernel(qi_ref, ki_ref, wi_ref, o_ref, *, tq):
    qb, kb = pl.program_id(1), pl.program_id(2)

    @pl.when(kb > qb)
    def _():
        o_ref[...] = jnp.full(o_ref.shape, -jnp.inf, F32)

    @pl.when(kb <= qb)
    def _():
        kt = ki_ref[...].astype(BF16)
        w = wi_ref[...] * IDX_SCALE
        acc = jnp.zeros((tq, tq), F32)
        for h in range(H_IDX):
            s = lax.dot_general(qi_ref[:, h * D_IDX:(h + 1) * D_IDX], kt, NT_DIMS, preferred_element_type=F32)
            acc = acc + jnp.maximum(s, 0.0) * w[:, h:h + 1]
        row = lax.broadcasted_iota(jnp.int32, (tq, tq), 0) + qb * tq
        col = lax.broadcasted_iota(jnp.int32, (tq, tq), 1) + kb * tq
        o_ref[...] = jnp.where(col <= row, acc, -jnp.inf)


def indexer_prompt(qi, rest, *, batch, seq, tq, ki_blk, wi_blk):
    nq = seq // tq
    return pl.pallas_call(
        functools.partial(_idx_prompt_kernel, tq=tq),
        grid=(batch, nq, nq),
        in_specs=[pl.BlockSpec((tq, H_IDX * D_IDX), lambda b, qb, kb: (b * nq + qb, 0)),
                  pl.BlockSpec((tq, D_IDX), lambda b, qb, kb: (b * nq + jnp.minimum(kb, qb), ki_blk)),
                  pl.BlockSpec((tq, LANES), lambda b, qb, kb: (b * nq + qb, wi_blk))],
        out_specs=pl.BlockSpec((tq, tq), lambda b, qb, kb: (b * nq + qb, kb)),
        out_shape=jax.ShapeDtypeStruct((batch * seq, seq), F32),
        compiler_params=_params("parallel", "parallel", "arbitrary"),
        name="indexer_prompt",
    )(qi, rest, rest)


def _split_dot_nt(e, x):
    hi = x.astype(BF16)
    lo = (x - hi.astype(F32)).astype(BF16)
    return (lax.dot_general(e, hi, NT_DIMS, preferred_element_type=F32) +
            lax.dot_general(e, lo, NT_DIMS, preferred_element_type=F32))


def _idx_sample_kernel(pt_ref, qit_ref, w_ref, e_ref, *refs, n_src, t):
    del pt_ref
    src_refs = refs[:n_src]
    new_ref, o_ref = refs[n_src], refs[n_src + 1]
    kb = pl.program_id(1)
    nk = pl.num_programs(1)
    tk = o_ref.shape[1]
    new_rows = new_ref.shape[0]

    def scores(keys):
        st = jnp.dot(keys.astype(BF16), qit_ref[...], preferred_element_type=F32)
        x = jnp.maximum(st, 0.0) * w_ref[...]
        return _split_dot_nt(e_ref[...], x)[:t]

    @pl.when(kb < nk - 1)
    def _():
        o_ref[...] = scores(jnp.concatenate([r[...] for r in src_refs], axis=0))

    @pl.when(kb == nk - 1)
    def _():
        sc = scores(new_ref[...])
        row = lax.broadcasted_iota(jnp.int32, (t, new_rows), 0)
        col = lax.broadcasted_iota(jnp.int32, (t, new_rows), 1)
        o_ref[...] = jnp.full(o_ref.shape, -jnp.inf, F32)
        o_ref[:, :new_rows] = jnp.where(col <= row, sc, -jnp.inf)


def indexer_sample(qi, wi, ki_new, pool_kidx, page_table, *, tk):
    bsz, t, _ = qi.shape
    n_src = tk // PAGE_SIZE
    nk = page_table.shape[1] // n_src
    qit = qi.reshape(bsz, t * H_IDX, D_IDX).transpose(0, 2, 1)
    w = (wi * IDX_SCALE).reshape(bsz, 1, t * H_IDX)
    e_rows = 2 * SUBLANES
    e = jnp.repeat(jnp.eye(e_rows, t, dtype=BF16), H_IDX, axis=1)
    src_specs = [pl.BlockSpec((None, PAGE_SIZE, D_IDX),
                              functools.partial(lambda p, b, kb, pt: (pt[b, jnp.minimum(kb, nk - 1) * n_src + p], 0, 0), p))
                 for p in range(n_src)]
    gs = pltpu.PrefetchScalarGridSpec(
        num_scalar_prefetch=1, grid=(bsz, nk + 1),
        in_specs=[pl.BlockSpec((None, D_IDX, t * H_IDX), lambda b, kb, pt: (b, 0, 0)),
                  pl.BlockSpec((None, 1, t * H_IDX), lambda b, kb, pt: (b, 0, 0)),
                  pl.BlockSpec((e_rows, t * H_IDX), lambda b, kb, pt: (0, 0))] + src_specs +
                 [pl.BlockSpec((None, ki_new.shape[1], D_IDX), lambda b, kb, pt: (b, 0, 0))],
        out_specs=pl.BlockSpec((None, t, tk), lambda b, kb, pt: (b, 0, kb)))
    return pl.pallas_call(
        functools.partial(_idx_sample_kernel, n_src=n_src, t=t),
        grid_spec=gs,
        out_shape=jax.ShapeDtypeStruct((bsz, t, (nk + 1) * tk), F32),
        compiler_params=_params("parallel", "arbitrary"),
        name="indexer_sample",
    )(page_table, qit, w, e, *([pool_kidx] * n_src), ki_new)


TOPK_CHUNK = 512
INT_MIN = -2 ** 31


def _topk_kernel(s_ref, u_ref, o_ref, *, k):
    x = s_ref[...]
    rows, n = x.shape
    x = jnp.where(x == 0.0, 0.0, x)
    bits = lax.bitcast_convert_type(x, jnp.int32)
    key = bits ^ ((bits >> 31) & jnp.int32(0x7FFFFFFF))
    kf = jnp.float32(k)

    def body(it, tau_u):
        cand_u = tau_u | jnp.left_shift(jnp.int32(1), 31 - it)
        cand = cand_u ^ jnp.int32(INT_MIN)
        cnt = jnp.sum(jnp.where(key >= cand, 1.0, 0.0), axis=1, keepdims=True)
        return jnp.where(cnt >= kf, cand_u, tau_u)

    tau_u = lax.fori_loop(0, 32, body, jnp.zeros((rows, 1), jnp.int32))
    tau = tau_u ^ jnp.int32(INT_MIN)
    gt = key > tau
    eq = key == tau
    need = kf - jnp.sum(jnp.where(gt, 1.0, 0.0), axis=1, keepdims=True)
    run = jnp.zeros((rows, 1), F32)
    for c in range(n // TOPK_CHUNK):
        cs = slice(c * TOPK_CHUNK, (c + 1) * TOPK_CHUNK)
        e = jnp.where(eq[:, cs], 1.0, 0.0)
        before = jnp.dot(e.astype(BF16), u_ref[...], preferred_element_type=F32) + run
        o_ref[:, cs] = jnp.where(gt[:, cs] | (eq[:, cs] & (before < need)), 1.0, 0.0)
        run = run + jnp.sum(e, axis=1, keepdims=True)


def topk_mask(scores, k, *, tr):
    r, s = scores.shape
    assert s % TOPK_CHUNK == 0 and r % tr == 0
    ii = jnp.arange(TOPK_CHUNK)
    upper = (ii[:, None] < ii[None, :]).astype(BF16)
    return pl.pallas_call(
        functools.partial(_topk_kernel, k=k),
        grid=(r // tr,),
        in_specs=[pl.BlockSpec((tr, s), lambda i: (i, 0)),
                  pl.BlockSpec((TOPK_CHUNK, TOPK_CHUNK), lambda i: (0, 0))],
        out_specs=pl.BlockSpec((tr, s), lambda i: (i, 0)),
        out_shape=jax.ShapeDtypeStruct((r, s), F32),
        compiler_params=_params("parallel"),
        name="topk_mask",
    )(scores, upper)


def _s5_param_kernel(lr_ref, li_ref, ls_ref, br_ref, bi_ref, abr_ref, abi_ref, bbr_ref, bbi_ref):
    lr, li = lr_ref[...], li_ref[...]
    step = jnp.exp(ls_ref[...])
    mag = jnp.exp(lr * step)
    ab_re, ab_im = mag * jnp.cos(li * step), mag * jnp.sin(li * step)
    den = lr * lr + li * li
    nr, ni = ab_re - 1.0, ab_im
    f_re, f_im = (nr * lr + ni * li) / den, (ni * lr - nr * li) / den
    br, bi = br_ref[...], bi_ref[...]
    abr_ref[...] = ab_re
    abi_ref[...] = ab_im
    bbr_ref[...] = f_re * br - f_im * bi
    bbi_ref[...] = f_re * bi + f_im * br


def s5_params(lam_re, lam_im, log_step, b_re, b_im):
    w = B_STATE * B_GROUP
    rep = lambda a: jnp.repeat(a, B_GROUP, axis=1)
    ls = jnp.broadcast_to(log_step[:, None], (B_GROUPS, w))
    shp = jax.ShapeDtypeStruct((B_GROUPS, w), F32)
    abr, abi, bbr, bbi = pl.pallas_call(_s5_param_kernel, out_shape=[shp] * 4, name="s5_params")(
        rep(lam_re), rep(lam_im), ls, b_re.reshape(B_GROUPS, w), b_im.reshape(B_GROUPS, w))
    return (abr[:, ::B_GROUP], abi[:, ::B_GROUP],
            bbr.reshape(B_GROUPS, B_STATE, B_GROUP), bbi.reshape(B_GROUPS, B_STATE, B_GROUP))


def _cmul(ar, ai, br, bi):
    return ar * br - ai * bi, ar * bi + ai * br


def _s5_kernel(u_ref, bbr_ref, bbi_ref, ccr_ref, cci_ref, ar_ref, ai_ref, d_ref, h0r_ref, h0i_ref,
               g_ref, hfr_ref, hfi_ref, xr_ref, xi_ref, *, seg_len, precise):
    u = u_ref[...]
    if precise:
        mm = lambda a, b: jnp.dot(a, b, preferred_element_type=F32, precision=lax.Precision.HIGHEST)
    else:
        mm = lambda a, b: jnp.dot(a.astype(BF16), b.astype(BF16), preferred_element_type=F32)
    xr_ref[...] = mm(u, bbr_ref[...])
    xi_ref[...] = mm(u, bbi_ref[...])
    sw = xr_ref.shape[1]
    ar = jnp.broadcast_to(ar_ref[...], (S5_SEG, sw))
    ai = jnp.broadcast_to(ai_ref[...], (S5_SEG, sw))

    def rows(j):
        return pl.ds(pl.multiple_of(j * S5_SEG, S5_SEG), S5_SEG)

    def local_scan(j, carry):
        hr, hi = _cmul(ar, ai, *carry)
        hr = hr + xr_ref[rows(j), :]
        hi = hi + xi_ref[rows(j), :]
        xr_ref[rows(j), :] = hr
        xi_ref[rows(j), :] = hi
        return hr, hi

    zero = jnp.zeros((S5_SEG, sw), F32)
    fr, fi = lax.fori_loop(0, seg_len, local_scan, (zero, zero))

    pr, pi = ar[:1], ai[:1]
    for _ in range(int(math.log2(seg_len))):
        pr, pi = _cmul(pr, pi, pr, pi)

    cr, ci = h0r_ref[...], h0i_ref[...]
    crs, cis = [], []
    for s in range(S5_SEG):
        crs.append(cr)
        cis.append(ci)
        tr, ti = _cmul(pr, pi, cr, ci)
        cr, ci = fr[s:s + 1] + tr, fi[s:s + 1] + ti
    hfr_ref[...] = cr
    hfi_ref[...] = ci
    c_re = jnp.concatenate(crs, axis=0)
    c_im = jnp.concatenate(cis, axis=0)

    def add_carry(j, pw):
        tr, ti = _cmul(pw[0], pw[1], c_re, c_im)
        xr_ref[rows(j), :] += tr
        xi_ref[rows(j), :] += ti
        return _cmul(pw[0], pw[1], ar, ai)

    lax.fori_loop(0, seg_len, add_carry, (ar, ai))

    y = mm(xr_ref[...], ccr_ref[...]) - mm(xi_ref[...], cci_ref[...]) + d_ref[...] * u
    g_ref[...] = jax.nn.gelu(y)


def s5_scan(u_perm, h0_re, h0_im, ab_re, ab_im, bb_re, bb_im, c_re, c_im, d_skip, *, batch, seq, precise):
    seg_len = seq // S5_SEG
    assert seg_len & (seg_len - 1) == 0
    nch = B_GROUPS // S5_CHUNK_GROUPS
    uw, sw = S5_CHUNK_GROUPS * B_GROUP, S5_CHUNK_GROUPS * B_STATE
    eye = jnp.eye(S5_CHUNK_GROUPS, dtype=F32)

    def in_blockdiag(bb):
        bb = bb.reshape(nch, S5_CHUNK_GROUPS, B_STATE, B_GROUP)
        return jnp.einsum("kgnc,gh->kgchn", bb, eye).reshape(nch, uw, sw)

    def out_blockdiag(cc):
        cc = cc.reshape(nch, S5_CHUNK_GROUPS, B_GROUP, B_STATE)
        return jnp.einsum("kgcn,gh->kgnhc", cc, eye).reshape(nch, sw, uw)

    row = lambda a: a.reshape(nch, 1, sw)
    st = lambda a: a.astype(F32).reshape(batch, nch, 1, sw)
    w_in = pl.BlockSpec((None, uw, sw), lambda b, c: (c, 0, 0))
    w_out = pl.BlockSpec((None, sw, uw), lambda b, c: (c, 0, 0))
    a_spec = pl.BlockSpec((None, 1, sw), lambda b, c: (c, 0, 0))
    st_spec = pl.BlockSpec((None, None, 1, sw), lambda b, c: (b, c, 0, 0))
    u_spec = pl.BlockSpec((seq, uw), lambda b, c: (b, c))
    g, hfr, hfi = pl.pallas_call(
        functools.partial(_s5_kernel, seg_len=seg_len, precise=precise),
        grid=(batch, nch),
        in_specs=[u_spec, w_in, w_in, w_out, w_out, a_spec, a_spec,
                  pl.BlockSpec((1, uw), lambda b, c: (0, c)), st_spec, st_spec],
        out_specs=[u_spec, st_spec, st_spec],
        out_shape=[jax.ShapeDtypeStruct((batch * seq, B_WIDTH), F32),
                   jax.ShapeDtypeStruct((batch, nch, 1, sw), F32),
                   jax.ShapeDtypeStruct((batch, nch, 1, sw), F32)],
        scratch_shapes=[pltpu.VMEM((seq, sw), F32), pltpu.VMEM((seq, sw), F32)],
        compiler_params=_params("parallel", "parallel"),
        name="s5_scan",
    )(u_perm, in_blockdiag(bb_re), in_blockdiag(bb_im), out_blockdiag(c_re), out_blockdiag(c_im),
      row(ab_re), row(ab_im), d_skip.reshape(1, B_WIDTH), st(h0_re), st(h0_im))
    return g, hfr.reshape(batch, B_GROUPS, B_STATE), hfi.reshape(batch, B_GROUPS, B_STATE)


def to_segment_order(a, batch, seq):
    seg_len = seq // S5_SEG
    return a.reshape(batch, S5_SEG, seg_len, -1).swapaxes(1, 2).reshape(batch * seq, -1)


def from_segment_order(a, batch, seq):
    seg_len = seq // S5_SEG
    return a.reshape(batch, seg_len, S5_SEG, -1).swapaxes(1, 2).reshape(batch * seq, -1)


def _merge_kernel(o0, o1, o2, l0, l1, l2, out_ref):
    la, lb, lc = l0[...], l1[...], l2[...]
    m = jnp.maximum(jnp.maximum(la, lb), lc)
    wa, wb, wc = jnp.exp(la - m), jnp.exp(lb - m), jnp.exp(lc - m)
    out = (wa * o0[...] + wb * o1[...] + wc * o2[...]) / (wa + wb + wc)
    out_ref[...] = out.astype(out_ref.dtype)


def merge_groups(outs, lses):
    m, w = outs[0].shape
    tm = min(m, 512)
    spec = pl.BlockSpec((tm, w), lambda i: (i, 0))
    return pl.pallas_call(
        _merge_kernel, grid=(m // tm,), in_specs=[spec] * 6, out_specs=spec,
        out_shape=jax.ShapeDtypeStruct((m, w), BF16), compiler_params=_params("parallel"),
        name="merge_groups",
    )(*outs, *lses)


DSA_TQ = 256
C_TQ = 128
SAMPLE_TK = PAGES_PER_STEP * PAGE_SIZE
NEW_ROWS = 128


def _pad_rows(a, rows):
    return jnp.pad(a, ((0, 0), (0, rows - a.shape[1]), (0, 0)))


def even_mixer(x, h, w_in, w_idx_qb, w_out, s5w, bias_table, *, batch, seq, h0_re, h0_im,
               pool_kv=None, pool_kidx=None, page_table=None):
    m = batch * seq
    sample = pool_kv is not None
    lam_re, lam_im, log_step, b_re, b_im, c_re, c_im, d_skip, w_glu, b_glu = s5w

    q = linear(h, w_in, n_cols=Q_A_W, col_off=0, tn=512, out_dtypes=(BF16,))
    kv, kvb = linear(h, w_in, n_cols=KV_A_W, col_off=OFF_KV, tn=512, out_dtypes=(F32, BF16))
    cq = linear(h, w_in, n_cols=R_IDX, col_off=OFF_CQ, tn=512, out_dtypes=(BF16,))
    w_rest = jnp.concatenate([w_in[:, OFF_U:], w_in[:, OFF_KI:OFF_U], w_in[:, OFF_W:OFF_KI],
                              jnp.zeros((D_MODEL, LANES - H_IDX), w_in.dtype)], axis=1)
    rest = linear(h, w_rest, tn=256)
    u, ki, wi = rest[:, :B_WIDTH], rest[:, B_WIDTH:B_WIDTH + D_IDX], rest[:, B_WIDTH + D_IDX:B_WIDTH + D_IDX + H_IDX]
    qi = linear(cq, w_idx_qb, tn=512, out_dtypes=(BF16,))

    if not sample:
        k_sel = min(TOPK_MAX, seq // 4)
        scores = indexer_prompt(qi, rest, batch=batch, seq=seq, tq=DSA_TQ,
                                ki_blk=B_WIDTH // LANES, wi_blk=B_WIDTH // LANES + 1)
        sel = topk_mask(scores, k_sel, tr=64)
        nq = seq // DSA_TQ
        ii = jnp.arange(DSA_TQ)
        dist = (jnp.arange(nq)[:, None, None] * DSA_TQ + ii[None, :, None] - ii[None, None, :])
        lib = bias_of_dist(bias_table, dist, dist >= 0)
        att = dsa_prompt_attention(q, kvb, lib, sel, batch=batch, seq=seq, tq=DSA_TQ)
    else:
        k_sel = min(TOPK_MAX, (PAST_LEN + seq) // 4)
        ki_new = _pad_rows(ki.reshape(batch, seq, D_IDX), NEW_ROWS)
        scores = indexer_sample(qi.reshape(batch, seq, H_IDX * D_IDX), wi.reshape(batch, seq, H_IDX),
                                ki_new, pool_kidx, page_table, tk=SAMPLE_TK)
        s_pad = scores.shape[-1]
        sel = topk_mask(scores.reshape(m, s_pad), k_sel, tr=m).reshape(batch, seq, s_pad)
        key_pos = jnp.arange(s_pad)
        dist = PAST_LEN + jnp.arange(seq)[:, None] - key_pos[None, :]
        bias = bias_of_dist(bias_table, dist, dist >= 0)
        kv_new = _pad_rows(kv.reshape(batch, seq, KV_A_W), NEW_ROWS)
        qf = q.astype(F32).reshape(batch, seq, Q_A_W)
        att = sample_attention(qf, pool_kv.reshape(-1, PAGE_SIZE, KV_A_W), kv_new, bias, sel,
                               hq=H_A, hkv=H_KV_A, page_table=page_table, tk=SAMPLE_TK, with_lse=False)
        att = att.reshape(m, Q_A_W).astype(BF16)

    ab_re, ab_im, bb_re, bb_im = s5_params(lam_re, lam_im, log_step, b_re, b_im)
    g_perm, hf_re, hf_im = s5_scan(to_segment_order(u, batch, seq), h0_re, h0_im, ab_re, ab_im, bb_re, bb_im,
                                   c_re, c_im, d_skip, batch=batch, seq=seq, precise=sample)
    g = from_segment_order(g_perm, batch, seq)
    ssm = linear(g.astype(BF16), w_glu, tn=512, mode="glu", extras=(g, b_glu.reshape(1, B_WIDTH)),
                 out_dtypes=(BF16,))

    x = linear(att, w_out, row_off=0, tn=512, mode="resid", extras=(x,))
    x = linear(ssm, w_out, row_off=Q_A_W, tn=512, mode="resid", extras=(x,))
    return x, kv, ki, hf_re, hf_im


def odd_mixer(x, h, w_in, w_out, bias_table, *, batch, seq, caches=None):
    m = batch * seq
    sample = caches is not None
    q = linear(h, w_in, n_cols=C_WIDTH, col_off=0, tn=512, out_dtypes=(BF16,))
    kv, kvb = linear(h, w_in, n_cols=2 * C_WIDTH, col_off=C_WIDTH, tn=512, out_dtypes=(F32, BF16))
    kv5 = kv.reshape(batch, seq, 2, H_C, HEAD_DIM)
    outs, lses, rows = [], [], []
    for g, (window, dilation) in enumerate(C_PAIRS):
        cols = bias_table[:, g * H_C_G:(g + 1) * H_C_G]
        kv_g = kv5[:, :, :, g * H_C_G:(g + 1) * H_C_G]
        if not sample:
            ii = jnp.arange(C_TQ)
            steps = jnp.arange(2)[:, None, None] * C_TQ + ii[None, :, None] - ii[None, None, :]
            valid = (steps >= 0) & (steps <= window // dilation)
            lib = bias_of_dist(cols, steps * dilation, valid)
            o, l = dilated_prompt_group(q, kvb, lib, batch=batch, seq=seq, g=g, dilation=dilation, tq=C_TQ)
            rows.append(kv_g[:, seq - min(window, seq):])
        else:
            buf = caches[g]
            wb = buf.shape[1]
            tk = min(wb, 512)
            s_pad = wb + tk
            dist = wb + jnp.arange(seq)[:, None] - jnp.arange(s_pad)[None, :]
            valid = (dist >= 0) & (dist <= window) & (dist % dilation == 0)
            bias = bias_of_dist(cols, dist, valid)
            new_kv = _pad_rows(kv_g.reshape(batch, seq, 2 * C_OUT), NEW_ROWS)
            qg = q[:, g * C_OUT:(g + 1) * C_OUT].astype(F32).reshape(batch, seq, C_OUT)
            o, l = sample_attention(qg, buf.reshape(batch, wb, 2 * C_OUT), new_kv, bias, None,
                                    hq=H_C_G, hkv=H_C_G, tk=tk, with_lse=True)
            o, l = o.reshape(m, C_OUT), l.reshape(m, C_OUT)
            rows.append(kv_g)
        outs.append(o)
        lses.append(l)
    att = merge_groups(outs, lses)
    x = linear(att, w_out, tn=512, mode="resid", extras=(x,))
    return x, rows


def kernel(x_prompt, x_sample, cache_a_kv, cache_a_kidx, state_b_re, state_b_im, cache_c0_kv, cache_c1_kv, cache_c2_kv, page_table, p_prompt, p_sample, bias_table, norm_g, final_g, ffn1_w1, ffn1_w3, ffn1_w2, ffn2_w1, ffn2_w3, ffn2_w2, ple_gate, ple_proj, w_in_even, w_idx_qb, w_out_even, s5_lam_re, s5_lam_im, s5_log_step, s5_b_re, s5_b_im, s5_c_re, s5_c_im, s5_d, s5_w_glu, s5_b_glu, w_in_odd, w_out_odd):
    bp, tp, _ = x_prompt.shape
    bs, ts, _ = x_sample.shape
    xp = x_prompt.reshape(bp * tp, D_MODEL)
    xs = x_sample.reshape(bs * ts, D_MODEL)
    pp = p_prompt.reshape(DEPTH, bp * tp, D_PLE)
    ps = p_sample.reshape(DEPTH, bs * ts, D_PLE)
    outs_p = {k: [] for k in ("kv", "ki", "re", "im", "c0", "c1", "c2")}
    outs_s = {k: [] for k in ("kv", "ki", "re", "im", "c0", "c1", "c2")}
    for i in range(DEPTH):
        li = i // 2
        xp = macaron_half(xp, norm_g[i, 0], ffn1_w1[i], ffn1_w3[i], ffn1_w2[i])
        xs = macaron_half(xs, norm_g[i, 0], ffn1_w1[i], ffn1_w3[i], ffn1_w2[i])
        hp = rmsnorm(xp, norm_g[i, 1], BF16)
        hs = rmsnorm(xs, norm_g[i, 1], BF16)
        if i % 2 == 0:
            s5w = (s5_lam_re[li], s5_lam_im[li], s5_log_step[li], s5_b_re[li], s5_b_im[li],
                   s5_c_re[li], s5_c_im[li], s5_d[li], s5_w_glu[li], s5_b_glu[li])
            zeros = jnp.zeros((bp, B_GROUPS, B_STATE), F32)
            xp, kv, ki, hre, him = even_mixer(xp, hp, w_in_even[li], w_idx_qb[li], w_out_even[li], s5w,
                                              bias_table, batch=bp, seq=tp, h0_re=zeros, h0_im=zeros)
            outs_p["kv"].append(kv.reshape(bp, tp, 2, H_KV_A, HEAD_DIM))
            outs_p["ki"].append(ki.reshape(bp, tp, D_IDX))
            outs_p["re"].append(hre)
            outs_p["im"].append(him)
            xs, kv, ki, hre, him = even_mixer(xs, hs, w_in_even[li], w_idx_qb[li], w_out_even[li], s5w,
                                              bias_table, batch=bs, seq=ts, h0_re=state_b_re[li],
                                              h0_im=state_b_im[li], pool_kv=cache_a_kv[li],
                                              pool_kidx=cache_a_kidx[li], page_table=page_table)
            outs_s["kv"].append(kv.reshape(bs, ts, 2, H_KV_A, HEAD_DIM))
            outs_s["ki"].append(ki.reshape(bs, ts, D_IDX))
            outs_s["re"].append(hre)
            outs_s["im"].append(him)
        else:
            xp, rows = odd_mixer(xp, hp, w_in_odd[li], w_out_odd[li], bias_table, batch=bp, seq=tp)
            for g in range(N_C_GROUPS):
                outs_p["c%d" % g].append(rows[g])
            xs, rows = odd_mixer(xs, hs, w_in_odd[li], w_out_odd[li], bias_table, batch=bs, seq=ts,
                                 caches=(cache_c0_kv[li], cache_c1_kv[li], cache_c2_kv[li]))
            for g in range(N_C_GROUPS):
                outs_s["c%d" % g].append(rows[g])
        xp = macaron_half(xp, norm_g[i, 2], ffn2_w1[i], ffn2_w3[i], ffn2_w2[i])
        xs = macaron_half(xs, norm_g[i, 2], ffn2_w1[i], ffn2_w3[i], ffn2_w2[i])
        xp = ple_add(xp, pp[i], norm_g[i, 3], ple_gate[i], ple_proj[i])
        xs = ple_add(xs, ps[i], norm_g[i, 3], ple_gate[i], ple_proj[i])
    y_prompt = rmsnorm(xp, final_g, F32).reshape(bp, tp, D_MODEL)
    y_sample = rmsnorm(xs, final_g, F32).reshape(bs, ts, D_MODEL)
    order = ("kv", "ki", "re", "im", "c0", "c1", "c2")
    return (y_prompt, y_sample, *[jnp.stack(outs_p[k]) for k in order], *[jnp.stack(outs_s[k]) for k in order])
```

```python
import functools
import math

import jax
import jax.numpy as jnp
from jax import lax
from jax.experimental import pallas as pl
from jax.experimental.pallas import tpu as pltpu

D_MODEL = 4096
DEPTH = 2
PAST_LEN = 16384
PAGE_SIZE = 128
HEAD_DIM = 128
EPS = 1e-6
H_A = 24
H_KV_A = 8
H_IDX = 32
D_IDX = 128
R_IDX = 512
TOPK_MAX = 256
B_WIDTH = 1024
B_GROUP = 16
B_GROUPS = 64
B_STATE = 64
C_PAIRS = ((128, 1), (512, 4), (2048, 16))
N_C_GROUPS = 3
H_C_G = 8
H_C = 24
C_WIDTH = H_C * HEAD_DIM
C_OUT = H_C_G * HEAD_DIM
NUM_BUCKETS = 32
MAX_DISTANCE = 2048
D_FF = 11008
D_PLE = 256
Q_A_W = H_A * HEAD_DIM
KV_A_W = 2 * H_KV_A * HEAD_DIM
OFF_KV = Q_A_W
OFF_CQ = OFF_KV + KV_A_W
OFF_W = OFF_CQ + R_IDX
OFF_KI = OFF_W + H_IDX
OFF_U = OFF_KI + D_IDX
IN_EVEN = OFF_U + B_WIDTH

LANES = 128
SUBLANES = 8
VMEM_LIMIT = 56 * 1024 * 1024

S5_SEG = SUBLANES
S5_CHUNK_GROUPS = 8
PAGES_PER_STEP = 8
NEG_INIT = -1e30

F32 = jnp.float32
BF16 = jnp.bfloat16
NT_DIMS = (((1,), (1,)), ((), ()))


def _params(*sem):
    return pltpu.CompilerParams(dimension_semantics=sem, vmem_limit_bytes=VMEM_LIMIT)


def _rms_kernel(x_ref, g_ref, o_ref):
    x = x_ref[...]
    ms = jnp.mean(x * x, axis=-1, keepdims=True)
    o_ref[...] = (x * lax.rsqrt(ms + EPS) * g_ref[...]).astype(o_ref.dtype)


def rmsnorm(x, g, out_dtype):
    m, d = x.shape
    tm = min(m, 256)
    return pl.pallas_call(
        _rms_kernel,
        grid=(m // tm,),
        in_specs=[pl.BlockSpec((tm, d), lambda i: (i, 0)), pl.BlockSpec((1, d), lambda i: (0, 0))],
        out_specs=pl.BlockSpec((tm, d), lambda i: (i, 0)),
        out_shape=jax.ShapeDtypeStruct((m, d), out_dtype),
        compiler_params=_params("parallel"),
        name="rmsnorm",
    )(x, g.reshape(1, d))


def _linear_kernel(*refs, mode, n_out, scale):
    a_ref, w_ref = refs[0], refs[1]
    extras = refs[2:len(refs) - n_out]
    outs = refs[len(refs) - n_out:]
    acc = jnp.dot(a_ref[...].astype(BF16), w_ref[...].astype(BF16), preferred_element_type=F32)
    if mode == "none":
        res = acc
    elif mode == "resid":
        res = extras[0][...] + scale * acc
    elif mode == "glu":
        g = extras[0][...]
        res = g * jax.nn.sigmoid(acc + extras[1][...])
    elif mode == "ple":
        x_ref, p_ref, wp_ref = extras
        proj = jnp.dot(p_ref[...].astype(BF16), wp_ref[...].astype(BF16), preferred_element_type=F32)
        res = x_ref[...] + jax.nn.sigmoid(acc) * proj
    else:
        raise ValueError(mode)
    for o in outs:
        o[...] = res.astype(o.dtype)


def linear(a, w, li, *, n_cols=None, col_off=0, row_off=0, tn, mode="none", extras=(),
           out_dtypes=(F32,), scale=1.0):
    m, k = a.shape
    n_cols = w.shape[2] if n_cols is None else n_cols
    tm = min(m, 1024)
    assert m % tm == 0 and n_cols % tn == 0 and col_off % tn == 0 and row_off % k == 0
    jo, ro = col_off // tn, row_off // k
    tile = pl.BlockSpec((tm, tn), lambda i, j: (i, j))
    in_specs = [pl.BlockSpec((tm, k), lambda i, j: (i, 0)),
                pl.BlockSpec((None, k, tn), lambda i, j: (li, ro, j + jo))]
    if mode == "resid":
        in_specs += [tile]
    elif mode == "glu":
        in_specs += [tile, pl.BlockSpec((1, tn), lambda i, j: (0, j))]
    elif mode == "ple":
        kp = extras[1].shape[1]
        in_specs += [tile, pl.BlockSpec((tm, kp), lambda i, j: (i, 0)),
                     pl.BlockSpec((None, kp, tn), lambda i, j: (li, 0, j))]
    res = pl.pallas_call(
        functools.partial(_linear_kernel, mode=mode, n_out=len(out_dtypes), scale=scale),
        grid=(m // tm, n_cols // tn),
        in_specs=in_specs,
        out_specs=[tile] * len(out_dtypes),
        out_shape=[jax.ShapeDtypeStruct((m, n_cols), dt) for dt in out_dtypes],
        compiler_params=_params("parallel", "arbitrary"),
        name="linear_" + mode,
    )(a, w, *extras)
    return res if len(out_dtypes) > 1 else res[0]


FFN_TN = 256
DOWN_TK = 1024
D_FF_PAD = -(-D_FF // DOWN_TK) * DOWN_TK


def _swiglu_kernel(a_ref, w1_ref, w3_ref, o_ref, *, n_real):
    j = pl.program_id(1)

    @pl.when(j < n_real)
    def _():
        a = a_ref[...]
        g = jnp.dot(a, w1_ref[...].astype(BF16), preferred_element_type=F32)
        u = jnp.dot(a, w3_ref[...].astype(BF16), preferred_element_type=F32)
        o_ref[...] = (jax.nn.silu(g) * u).astype(o_ref.dtype)

    @pl.when(j >= n_real)
    def _():
        o_ref[...] = jnp.zeros(o_ref.shape, o_ref.dtype)


def swiglu_up(h, w1, w3, li, *, n_pad):
    m, k = h.shape
    n = w1.shape[2]
    tn = FFN_TN
    tm = min(m, 1024)
    n_real = n // tn
    w_spec = pl.BlockSpec((None, k, tn), lambda i, j: (li, 0, jnp.minimum(j, n_real - 1)))
    return pl.pallas_call(
        functools.partial(_swiglu_kernel, n_real=n_real),
        grid=(m // tm, n_pad // tn),
        in_specs=[pl.BlockSpec((tm, k), lambda i, j: (i, 0)), w_spec, w_spec],
        out_specs=pl.BlockSpec((tm, tn), lambda i, j: (i, j)),
        out_shape=jax.ShapeDtypeStruct((m, n_pad), BF16),
        compiler_params=_params("parallel", "arbitrary"),
        name="swiglu_up",
    )(h, w1, w3)


def _down_kernel(a_ref, w_ref, x_ref, o_ref, *, scale):
    @pl.when(pl.program_id(2) == 0)
    def _():
        o_ref[...] = x_ref[...]

    o_ref[...] += scale * jnp.dot(a_ref[...], w_ref[...], preferred_element_type=F32)


def down_resid(act, w, x, *, scale):
    m, k = act.shape
    n = w.shape[1]
    tm = min(m, 1024)
    tn = min(n, 1024)
    tk = DOWN_TK
    return pl.pallas_call(
        functools.partial(_down_kernel, scale=scale),
        grid=(m // tm, n // tn, k // tk),
        in_specs=[pl.BlockSpec((tm, tk), lambda i, j, kk: (i, kk)),
                  pl.BlockSpec((tk, tn), lambda i, j, kk: (kk, j)),
                  pl.BlockSpec((tm, tn), lambda i, j, kk: (i, j))],
        out_specs=pl.BlockSpec((tm, tn), lambda i, j, kk: (i, j)),
        out_shape=jax.ShapeDtypeStruct((m, n), F32),
        compiler_params=_params("parallel", "parallel", "arbitrary"),
        name="down_resid",
    )(act, w, x)


def padded_bf16(w, rows):
    return jnp.pad(w.astype(BF16), ((0, rows - w.shape[0]), (0, 0)))


def macaron_half(x, g, w1, w3, w2b, li):
    h = rmsnorm(x, g, BF16)
    act = swiglu_up(h, w1, w3, li, n_pad=w2b.shape[0])
    return down_resid(act, w2b, x, scale=0.5)


def ple_add(x, p, g, w_gate, w_proj, li):
    h = rmsnorm(x, g, BF16)
    return linear(h, w_gate, li, tn=256, mode="ple", extras=(x, p.astype(BF16), w_proj))


def t5_bucket(dist):
    max_exact = NUM_BUCKETS // 2
    d = jnp.maximum(dist, 0)
    ratio = jnp.log(jnp.maximum(d, 1).astype(F32) / max_exact) / math.log(MAX_DISTANCE / max_exact)
    large = jnp.minimum(max_exact + (ratio * (NUM_BUCKETS - max_exact)).astype(jnp.int32), NUM_BUCKETS - 1)
    return jnp.where(d < max_exact, d, large)


def bias_of_dist(bias_cols, dist, valid):
    onehot = t5_bucket(dist)[..., None] == jnp.arange(NUM_BUCKETS)
    b = jnp.sum(jnp.where(onehot[..., None], bias_cols.astype(F32), 0.0), axis=-2)
    b = jnp.where(valid[..., None], b, -jnp.inf)
    return jnp.moveaxis(b, -1, 0)


def sample_bias(bias_cols, base, t, s_pad, valid_fn):
    n = jnp.arange(s_pad + t - 1)
    dist = base + (t - 1) - n
    table = bias_of_dist(bias_cols, dist, valid_fn(dist))
    return jnp.stack([table[:, t - 1 - i:t - 1 - i + s_pad] for i in range(t)], axis=1)


def _toeplitz_kernel(w_ref, o_ref):
    nd, tq, _ = o_ref.shape
    for d in range(nd):
        x = jnp.broadcast_to(w_ref[d], (tq, 2 * tq))
        r = pltpu.roll(x, 0, 1, stride=1, stride_axis=0)
        o_ref[d] = r[:, tq:]


def toeplitz_bias_tiles(bias_cols, n_delta, tq, step, max_steps):
    h = bias_cols.shape[1]
    steps = jnp.arange(n_delta)[:, None] * tq + tq - jnp.arange(2 * tq)[None, :]
    rows = bias_of_dist(bias_cols, steps * step, (steps >= 0) & (steps <= max_steps))
    return pl.pallas_call(
        _toeplitz_kernel,
        grid=(h,),
        in_specs=[pl.BlockSpec((None, n_delta, 1, 2 * tq), lambda i: (i, 0, 0, 0))],
        out_specs=pl.BlockSpec((None, n_delta, tq, tq), lambda i: (i, 0, 0, 0)),
        out_shape=jax.ShapeDtypeStruct((h, n_delta, tq, tq), F32),
        compiler_params=_params("parallel"),
        name="toeplitz_bias_tiles",
    )(rows.reshape(h, n_delta, 1, 2 * tq))


def _flash_init(m_ref, l_ref, acc_ref):
    m_ref[...] = jnp.full(m_ref.shape, NEG_INIT, F32)
    l_ref[...] = jnp.zeros(l_ref.shape, F32)
    acc_ref[...] = jnp.zeros(acc_ref.shape, F32)


def _flash_step(q, k, v, bias_at, sel, m_ref, l_ref, acc_ref, *, hq, hkv):
    rep = hq // hkv
    scale = HEAD_DIM ** -0.5
    for h in range(hq):
        g = h // rep
        hs = slice(h * HEAD_DIM, (h + 1) * HEAD_DIM)
        gs = slice(g * HEAD_DIM, (g + 1) * HEAD_DIM)
        s = lax.dot_general(q[:, hs], k[:, gs], NT_DIMS, preferred_element_type=F32) * scale + bias_at(h)
        if sel is not None:
            s = jnp.where(sel, s, -jnp.inf)
        m_old = m_ref[h]
        m_new = jnp.maximum(m_old, jnp.max(s, axis=1, keepdims=True))
        alpha = jnp.exp(m_old - m_new)
        p = jnp.exp(s - m_new[:, :1])
        l_ref[h] = alpha * l_ref[h] + jnp.sum(p, axis=1, keepdims=True)
        acc_ref[:, hs] = alpha * acc_ref[:, hs] + jnp.dot(p.astype(BF16), v[:, gs], preferred_element_type=F32)
        m_ref[h] = m_new


def _flash_finish(o_ref, lse_ref, m_ref, l_ref, acc_ref, *, hq):
    for h in range(hq):
        hs = slice(h * HEAD_DIM, (h + 1) * HEAD_DIM)
        l = l_ref[h]
        o_ref[:, hs] = (acc_ref[:, hs] / l).astype(o_ref.dtype)
        if lse_ref is not None:
            lse_ref[:, hs] = m_ref[h] + jnp.log(l)


def _flash_scratch(tq, hq):
    return [pltpu.VMEM((hq, tq, LANES), F32), pltpu.VMEM((hq, tq, LANES), F32),
            pltpu.VMEM((tq, hq * HEAD_DIM), F32)]


def _prompt_attn_kernel(*refs, hq, hkv, has_sel, with_lse, first_kb):
    q_ref, k_ref, v_ref, bias_ref = refs[:4]
    pos = 4
    sel_ref = None
    if has_sel:
        sel_ref = refs[pos]
        pos += 1
    o_ref = refs[pos]
    pos += 1
    lse_ref = None
    if with_lse:
        lse_ref = refs[pos]
        pos += 1
    m_ref, l_ref, acc_ref = refs[pos:pos + 3]
    qb = pl.program_id(2)
    kk = pl.program_id(3)
    nk = pl.num_programs(3)

    @pl.when(kk == 0)
    def _():
        _flash_init(m_ref, l_ref, acc_ref)

    @pl.when(first_kb(qb, kk) >= 0)
    def _():
        sel = None if sel_ref is None else sel_ref[...] > 0.0
        _flash_step(q_ref[...].astype(BF16), k_ref[...].astype(BF16), v_ref[...].astype(BF16),
                    lambda h: bias_ref[h], sel, m_ref, l_ref, acc_ref, hq=hq, hkv=hkv)

    @pl.when(kk == nk - 1)
    def _():
        _flash_finish(o_ref, lse_ref, m_ref, l_ref, acc_ref, hq=hq)


def dsa_prompt_attention(q, kvb, bias_lib, sel, *, batch, seq, tq):
    nq = seq // tq
    grid = (batch, 1, nq, nq)

    def kb_of(qb, kk):
        return jnp.minimum(kk, qb)

    in_specs = [
        pl.BlockSpec((tq, Q_A_W), lambda b, r, qb, kk: (b * nq + qb, 0)),
        pl.BlockSpec((tq, H_KV_A * HEAD_DIM), lambda b, r, qb, kk: (b * nq + kb_of(qb, kk), 0)),
        pl.BlockSpec((tq, H_KV_A * HEAD_DIM), lambda b, r, qb, kk: (b * nq + kb_of(qb, kk), 1)),
        pl.BlockSpec((H_A, None, tq, tq), lambda b, r, qb, kk: (0, qb - kb_of(qb, kk), 0, 0)),
        pl.BlockSpec((tq, tq), lambda b, r, qb, kk: (b * nq + qb, kb_of(qb, kk))),
    ]
    return pl.pallas_call(
        functools.partial(_prompt_attn_kernel, hq=H_A, hkv=H_KV_A, has_sel=True, with_lse=False,
                          first_kb=lambda qb, kk: qb - kk),
        grid=grid,
        in_specs=in_specs,
        out_specs=pl.BlockSpec((tq, Q_A_W), lambda b, r, qb, kk: (b * nq + qb, 0)),
        out_shape=jax.ShapeDtypeStruct((batch * seq, Q_A_W), BF16),
        scratch_shapes=_flash_scratch(tq, H_A),
        compiler_params=_params("parallel", "arbitrary", "arbitrary", "arbitrary"),
        name="dsa_prompt_attention",
    )(q, kvb, kvb, bias_lib, sel)


def dilated_prompt_group(q, kvb, bias_lib, *, batch, seq, g, dilation, tq):
    tl = seq // dilation
    nq = tl // tq
    qv = q.reshape(batch, tl, dilation * C_WIDTH)
    kvv = kvb.reshape(batch, tl, dilation * 2 * C_WIDTH)
    qcols, kvcols = C_WIDTH // C_OUT, 2 * C_WIDTH // C_OUT

    def kb_of(qb, kk):
        return jnp.maximum(qb - 1 + kk, 0)

    in_specs = [
        pl.BlockSpec((None, tq, C_OUT), lambda b, r, qb, kk: (b, qb, r * qcols + g)),
        pl.BlockSpec((None, tq, C_OUT), lambda b, r, qb, kk: (b, kb_of(qb, kk), r * kvcols + g)),
        pl.BlockSpec((None, tq, C_OUT), lambda b, r, qb, kk: (b, kb_of(qb, kk), r * kvcols + N_C_GROUPS + g)),
        pl.BlockSpec((H_C_G, None, tq, tq), lambda b, r, qb, kk: (0, 1 - kk, 0, 0)),
    ]
    out_spec = pl.BlockSpec((None, tq, C_OUT), lambda b, r, qb, kk: (b, qb, r))
    o, lse = pl.pallas_call(
        functools.partial(_prompt_attn_kernel, hq=H_C_G, hkv=H_C_G, has_sel=False, with_lse=True,
                          first_kb=lambda qb, kk: qb - 1 + kk),
        grid=(batch, dilation, nq, 2),
        in_specs=in_specs,
        out_specs=[out_spec, out_spec],
        out_shape=[jax.ShapeDtypeStruct((batch, tl, dilation * C_OUT), F32)] * 2,
        scratch_shapes=_flash_scratch(tq, H_C_G),
        compiler_params=_params("parallel", "parallel", "arbitrary", "arbitrary"),
        name="dilated_prompt_group",
    )(qv, kvv, kvv, bias_lib)
    return o.reshape(batch * seq, C_OUT), lse.reshape(batch * seq, C_OUT)


def _sample_attn_kernel(*refs, hq, hkv, n_src, has_sel, with_lse, has_table):
    if has_table:
        refs = refs[1:]
    q_ref = refs[0]
    src_refs = refs[1:1 + n_src]
    new_ref, bias_ref = refs[1 + n_src], refs[2 + n_src]
    pos = 3 + n_src
    sel_ref = None
    if has_sel:
        sel_ref = refs[pos]
        pos += 1
    o_ref = refs[pos]
    pos += 1
    lse_ref = None
    if with_lse:
        lse_ref = refs[pos]
        pos += 1
    m_ref, l_ref, acc_ref = refs[pos:pos + 3]
    kb = pl.program_id(1)
    nk = pl.num_programs(1)
    kw = hkv * HEAD_DIM
    new_rows = new_ref.shape[0]

    @pl.when(kb == 0)
    def _():
        _flash_init(m_ref, l_ref, acc_ref)

    q = q_ref[...].astype(BF16)

    @pl.when(kb < nk - 1)
    def _():
        kv = jnp.concatenate([r[...] for r in src_refs], axis=0) if n_src > 1 else src_refs[0][...]
        sel = None if sel_ref is None else sel_ref[...] > 0.0
        _flash_step(q, kv[:, :kw].astype(BF16), kv[:, kw:].astype(BF16), lambda h: bias_ref[h], sel,
                    m_ref, l_ref, acc_ref, hq=hq, hkv=hkv)

    @pl.when(kb == nk - 1)
    def _():
        kv = new_ref[...]
        sel = None if sel_ref is None else sel_ref[:, :new_rows] > 0.0
        _flash_step(q, kv[:, :kw].astype(BF16), kv[:, kw:].astype(BF16), lambda h: bias_ref[h, :, :new_rows],
                    sel, m_ref, l_ref, acc_ref, hq=hq, hkv=hkv)
        _flash_finish(o_ref, lse_ref, m_ref, l_ref, acc_ref, hq=hq)


def sample_attention(q, src, li, new_kv, bias, sel, *, hq, hkv, page_table=None, tk, with_lse):
    bsz, t, _ = q.shape
    kvw = 2 * hkv * HEAD_DIM
    new_rows = new_kv.shape[1]
    paged = page_table is not None
    if paged:
        n_src = tk // PAGE_SIZE
        nk = page_table.shape[1] // n_src
        src_specs = [pl.BlockSpec((None, None, PAGE_SIZE, kvw),
                                  functools.partial(lambda p, b, kb, pt: (li, pt[b, jnp.minimum(kb, nk - 1) * n_src + p], 0, 0), p))
                     for p in range(n_src)]
        ix = lambda f: (lambda b, kb, pt: f(b, kb))
    else:
        n_src = 1
        nk = src.shape[2] // tk
        src_specs = [pl.BlockSpec((None, None, tk, kvw), lambda b, kb: (li, b, jnp.minimum(kb, nk - 1), 0))]
        ix = lambda f: f
    in_specs = ([pl.BlockSpec((None, t, hq * HEAD_DIM), ix(lambda b, kb: (b, 0, 0)))] + src_specs +
                [pl.BlockSpec((None, new_rows, kvw), ix(lambda b, kb: (b, 0, 0))),
                 pl.BlockSpec((hq, t, tk), ix(lambda b, kb: (0, 0, kb)))])
    args = [q] + [src] * n_src + [new_kv, bias]
    if sel is not None:
        in_specs.append(pl.BlockSpec((None, t, tk), ix(lambda b, kb: (b, 0, kb))))
        args.append(sel)
    out_spec = pl.BlockSpec((None, t, hq * HEAD_DIM), ix(lambda b, kb: (b, 0, 0)))
    n_out = 2 if with_lse else 1
    kern = functools.partial(_sample_attn_kernel, hq=hq, hkv=hkv, n_src=n_src, has_sel=sel is not None,
                             with_lse=with_lse, has_table=paged)
    out_shape = [jax.ShapeDtypeStruct((bsz, t, hq * HEAD_DIM), F32)] * n_out
    scratch = _flash_scratch(t, hq)
    cp = _params("parallel", "arbitrary")
    if paged:
        gs = pltpu.PrefetchScalarGridSpec(num_scalar_prefetch=1, grid=(bsz, nk + 1), in_specs=in_specs,
                                          out_specs=[out_spec] * n_out, scratch_shapes=scratch)
        res = pl.pallas_call(kern, grid_spec=gs, out_shape=out_shape, compiler_params=cp,
                             name="sample_attention_paged")(page_table, *args)
    else:
        res = pl.pallas_call(kern, grid=(bsz, nk + 1), in_specs=in_specs, out_specs=[out_spec] * n_out,
                             out_shape=out_shape, scratch_shapes=scratch, compiler_params=cp,
                             name="sample_attention")(*args)
    return res if with_lse else res[0]


IDX_SCALE = (H_IDX ** -0.5) * (D_IDX ** -0.5)


def _idx_prompt_kernel(qi_ref, ki_ref, wi_ref, o_ref, *, tq):
    qb, kb = pl.program_id(1), pl.program_id(2)

    @pl.when(kb > qb)
    def _():
        o_ref[...] = jnp.full(o_ref.shape, -jnp.inf, F32)

    @pl.when(kb <= qb)
    def _():
        kt = ki_ref[...].astype(BF16)
        w = wi_ref[...] * IDX_SCALE
        acc = jnp.zeros((tq, tq), F32)
        for h in range(H_IDX):
            s = lax.dot_general(qi_ref[:, h * D_IDX:(h + 1) * D_IDX], kt, NT_DIMS, preferred_element_type=F32)
            acc = acc + jnp.maximum(s, 0.0) * w[:, h:h + 1]
        row = lax.broadcasted_iota(jnp.int32, (tq, tq), 0) + qb * tq
        col = lax.broadcasted_iota(jnp.int32, (tq, tq), 1) + kb * tq
        o_ref[...] = jnp.where(col <= row, acc, -jnp.inf)


def indexer_prompt(qi, rest, *, batch, seq, tq, ki_blk, wi_blk):
    nq = seq // tq
    return pl.pallas_call(
        functools.partial(_idx_prompt_kernel, tq=tq),
        grid=(batch, nq, nq),
        in_specs=[pl.BlockSpec((tq, H_IDX * D_IDX), lambda b, qb, kb: (b * nq + qb, 0)),
                  pl.BlockSpec((tq, D_IDX), lambda b, qb, kb: (b * nq + jnp.minimum(kb, qb), ki_blk)),
                  pl.BlockSpec((tq, LANES), lambda b, qb, kb: (b * nq + qb, wi_blk))],
        out_specs=pl.BlockSpec((tq, tq), lambda b, qb, kb: (b * nq + qb, kb)),
        out_shape=jax.ShapeDtypeStruct((batch * seq, seq), F32),
        compiler_params=_params("parallel", "parallel", "arbitrary"),
        name="indexer_prompt",
    )(qi, rest, rest)


def _split_dot_nt(e, x):
    hi = x.astype(BF16)
    lo = (x - hi.astype(F32)).astype(BF16)
    return (lax.dot_general(e, hi, NT_DIMS, preferred_element_type=F32) +
            lax.dot_general(e, lo, NT_DIMS, preferred_element_type=F32))


def _idx_sample_kernel(pt_ref, qit_ref, w_ref, e_ref, *refs, n_src, t):
    del pt_ref
    src_refs = refs[:n_src]
    new_ref, o_ref = refs[n_src], refs[n_src + 1]
    kb = pl.program_id(1)
    nk = pl.num_programs(1)
    tk = o_ref.shape[1]
    new_rows = new_ref.shape[0]

    def scores(keys):
        st = jnp.dot(keys.astype(BF16), qit_ref[...], preferred_element_type=F32)
        x = jnp.maximum(st, 0.0) * w_ref[...]
        return _split_dot_nt(e_ref[...], x)[:t]

    @pl.when(kb < nk - 1)
    def _():
        o_ref[...] = scores(jnp.concatenate([r[...] for r in src_refs], axis=0))

    @pl.when(kb == nk - 1)
    def _():
        sc = scores(new_ref[...])
        row = lax.broadcasted_iota(jnp.int32, (t, new_rows), 0)
        col = lax.broadcasted_iota(jnp.int32, (t, new_rows), 1)
        o_ref[...] = jnp.full(o_ref.shape, -jnp.inf, F32)
        o_ref[:, :new_rows] = jnp.where(col <= row, sc, -jnp.inf)


def indexer_sample(qi, wi, ki_new, pool_kidx, li, page_table, *, tk):
    bsz, t, _ = qi.shape
    n_src = tk // PAGE_SIZE
    nk = page_table.shape[1] // n_src
    qit = qi.reshape(bsz, t * H_IDX, D_IDX).transpose(0, 2, 1)
    w = (wi * IDX_SCALE).reshape(bsz, 1, t * H_IDX)
    e_rows = 2 * SUBLANES
    e = jnp.repeat(jnp.eye(e_rows, t, dtype=BF16), H_IDX, axis=1)
    src_specs = [pl.BlockSpec((None, None, PAGE_SIZE, D_IDX),
                              functools.partial(lambda p, b, kb, pt: (li, pt[b, jnp.minimum(kb, nk - 1) * n_src + p], 0, 0), p))
                 for p in range(n_src)]
    gs = pltpu.PrefetchScalarGridSpec(
        num_scalar_prefetch=1, grid=(bsz, nk + 1),
        in_specs=[pl.BlockSpec((None, D_IDX, t * H_IDX), lambda b, kb, pt: (b, 0, 0)),
                  pl.BlockSpec((None, 1, t * H_IDX), lambda b, kb, pt: (b, 0, 0)),
                  pl.BlockSpec((e_rows, t * H_IDX), lambda b, kb, pt: (0, 0))] + src_specs +
                 [pl.BlockSpec((None, ki_new.shape[1], D_IDX), lambda b, kb, pt: (b, 0, 0))],
        out_specs=pl.BlockSpec((None, t, tk), lambda b, kb, pt: (b, 0, kb)))
    return pl.pallas_call(
        functools.partial(_idx_sample_kernel, n_src=n_src, t=t),
        grid_spec=gs,
        out_shape=jax.ShapeDtypeStruct((bsz, t, (nk + 1) * tk), F32),
        compiler_params=_params("parallel", "arbitrary"),
        name="indexer_sample",
    )(page_table, qit, w, e, *([pool_kidx] * n_src), ki_new)


TOPK_CHUNK = 512
INT_MIN = -2 ** 31


def _topk_kernel(s_ref, u_ref, o_ref, *, k):
    x = s_ref[...]
    rows, n = x.shape
    x = jnp.where(x == 0.0, 0.0, x)
    bits = lax.bitcast_convert_type(x, jnp.int32)
    key = bits ^ ((bits >> 31) & jnp.int32(0x7FFFFFFF))
    kf = jnp.float32(k)

    def body(it, tau_u):
        cand_u = tau_u | jnp.left_shift(jnp.int32(1), 31 - it)
        cand = cand_u ^ jnp.int32(INT_MIN)
        cnt = jnp.sum(jnp.where(key >= cand, 1.0, 0.0), axis=1, keepdims=True)
        return jnp.where(cnt >= kf, cand_u, tau_u)

    tau_u = lax.fori_loop(0, 32, body, jnp.zeros((rows, 1), jnp.int32))
    tau = tau_u ^ jnp.int32(INT_MIN)
    gt = key > tau
    eq = key == tau
    need = kf - jnp.sum(jnp.where(gt, 1.0, 0.0), axis=1, keepdims=True)
    run = jnp.zeros((rows, 1), F32)
    for c in range(n // TOPK_CHUNK):
        cs = slice(c * TOPK_CHUNK, (c + 1) * TOPK_CHUNK)
        e = jnp.where(eq[:, cs], 1.0, 0.0)
        before = jnp.dot(e.astype(BF16), u_ref[...], preferred_element_type=F32) + run
        o_ref[:, cs] = jnp.where(gt[:, cs] | (eq[:, cs] & (before < need)), 1.0, 0.0)
        run = run + jnp.sum(e, axis=1, keepdims=True)


def topk_mask(scores, k, *, tr):
    r, s = scores.shape
    assert s % TOPK_CHUNK == 0 and r % tr == 0
    ii = jnp.arange(TOPK_CHUNK)
    upper = (ii[:, None] < ii[None, :]).astype(BF16)
    return pl.pallas_call(
        functools.partial(_topk_kernel, k=k),
        grid=(r // tr,),
        in_specs=[pl.BlockSpec((tr, s), lambda i: (i, 0)),
                  pl.BlockSpec((TOPK_CHUNK, TOPK_CHUNK), lambda i: (0, 0))],
        out_specs=pl.BlockSpec((tr, s), lambda i: (i, 0)),
        out_shape=jax.ShapeDtypeStruct((r, s), F32),
        compiler_params=_params("parallel"),
        name="topk_mask",
    )(scores, upper)


def _s5_param_kernel(lr_ref, li_ref, ls_ref, br_ref, bi_ref, abr_ref, abi_ref, bbr_ref, bbi_ref):
    lr, li = lr_ref[...], li_ref[...]
    step = jnp.exp(ls_ref[...])
    mag = jnp.exp(lr * step)
    ab_re, ab_im = mag * jnp.cos(li * step), mag * jnp.sin(li * step)
    den = lr * lr + li * li
    nr, ni = ab_re - 1.0, ab_im
    f_re, f_im = (nr * lr + ni * li) / den, (ni * lr - nr * li) / den
    br, bi = br_ref[...], bi_ref[...]
    abr_ref[...] = ab_re
    abi_ref[...] = ab_im
    bbr_ref[...] = f_re * br - f_im * bi
    bbi_ref[...] = f_re * bi + f_im * br


def s5_params(lam_re, lam_im, log_step, b_re, b_im):
    w = B_STATE * B_GROUP
    rep = lambda a: jnp.repeat(a, B_GROUP, axis=1)
    ls = jnp.broadcast_to(log_step[:, None], (B_GROUPS, w))
    shp = jax.ShapeDtypeStruct((B_GROUPS, w), F32)
    abr, abi, bbr, bbi = pl.pallas_call(_s5_param_kernel, out_shape=[shp] * 4, name="s5_params")(
        rep(lam_re), rep(lam_im), ls, b_re.reshape(B_GROUPS, w), b_im.reshape(B_GROUPS, w))
    return (abr[:, ::B_GROUP], abi[:, ::B_GROUP],
            bbr.reshape(B_GROUPS, B_STATE, B_GROUP), bbi.reshape(B_GROUPS, B_STATE, B_GROUP))


def _cmul(ar, ai, br, bi):
    return ar * br - ai * bi, ar * bi + ai * br


def _s5_kernel(u_ref, bbr_ref, bbi_ref, ccr_ref, cci_ref, ar_ref, ai_ref, d_ref, h0r_ref, h0i_ref,
               g_ref, hfr_ref, hfi_ref, xr_ref, xi_ref, *, seg_len, precise):
    u = u_ref[...]
    if precise:
        mm = lambda a, b: jnp.dot(a, b, preferred_element_type=F32, precision=lax.Precision.HIGHEST)
    else:
        mm = lambda a, b: jnp.dot(a.astype(BF16), b.astype(BF16), preferred_element_type=F32)
    xr_ref[...] = mm(u, bbr_ref[...])
    xi_ref[...] = mm(u, bbi_ref[...])
    sw = xr_ref.shape[1]
    ar = jnp.broadcast_to(ar_ref[...], (S5_SEG, sw))
    ai = jnp.broadcast_to(ai_ref[...], (S5_SEG, sw))

    def rows(j):
        return pl.ds(pl.multiple_of(j * S5_SEG, S5_SEG), S5_SEG)

    def local_scan(j, carry):
        hr, hi = _cmul(ar, ai, *carry)
        hr = hr + xr_ref[rows(j), :]
        hi = hi + xi_ref[rows(j), :]
        xr_ref[rows(j), :] = hr
        xi_ref[rows(j), :] = hi
        return hr, hi

    zero = jnp.zeros((S5_SEG, sw), F32)
    fr, fi = lax.fori_loop(0, seg_len, local_scan, (zero, zero))

    pr, pi = ar[:1], ai[:1]
    for _ in range(int(math.log2(seg_len))):
        pr, pi = _cmul(pr, pi, pr, pi)

    cr, ci = h0r_ref[...], h0i_ref[...]
    crs, cis = [], []
    for s in range(S5_SEG):
        crs.append(cr)
        cis.append(ci)
        tr, ti = _cmul(pr, pi, cr, ci)
        cr, ci = fr[s:s + 1] + tr, fi[s:s + 1] + ti
    hfr_ref[...] = cr
    hfi_ref[...] = ci
    c_re = jnp.concatenate(crs, axis=0)
    c_im = jnp.concatenate(cis, axis=0)

    def add_carry(j, pw):
        tr, ti = _cmul(pw[0], pw[1], c_re, c_im)
        xr_ref[rows(j), :] += tr
        xi_ref[rows(j), :] += ti
        return _cmul(pw[0], pw[1], ar, ai)

    lax.fori_loop(0, seg_len, add_carry, (ar, ai))

    y = mm(xr_ref[...], ccr_ref[...]) - mm(xi_ref[...], cci_ref[...]) + d_ref[...] * u
    g_ref[...] = jax.nn.gelu(y)


def s5_scan(u_perm, h0_re, h0_im, ab_re, ab_im, bb_re, bb_im, c_re, c_im, d_skip, *, batch, seq, precise):
    seg_len = seq // S5_SEG
    assert seg_len & (seg_len - 1) == 0
    nch = B_GROUPS // S5_CHUNK_GROUPS
    uw, sw = S5_CHUNK_GROUPS * B_GROUP, S5_CHUNK_GROUPS * B_STATE
    eye = jnp.eye(S5_CHUNK_GROUPS, dtype=F32)

    def in_blockdiag(bb):
        bb = bb.reshape(nch, S5_CHUNK_GROUPS, B_STATE, B_GROUP)
        return jnp.einsum("kgnc,gh->kgchn", bb, eye).reshape(nch, uw, sw)

    def out_blockdiag(cc):
        cc = cc.reshape(nch, S5_CHUNK_GROUPS, B_GROUP, B_STATE)
        return jnp.einsum("kgcn,gh->kgnhc", cc, eye).reshape(nch, sw, uw)

    row = lambda a: a.reshape(nch, 1, sw)
    st = lambda a: a.astype(F32).reshape(batch, nch, 1, sw)
    w_in = pl.BlockSpec((None, uw, sw), lambda b, c: (c, 0, 0))
    w_out = pl.BlockSpec((None, sw, uw), lambda b, c: (c, 0, 0))
    a_spec = pl.BlockSpec((None, 1, sw), lambda b, c: (c, 0, 0))
    st_spec = pl.BlockSpec((None, None, 1, sw), lambda b, c: (b, c, 0, 0))
    u_spec = pl.BlockSpec((seq, uw), lambda b, c: (b, c))
    g, hfr, hfi = pl.pallas_call(
        functools.partial(_s5_kernel, seg_len=seg_len, precise=precise),
        grid=(batch, nch),
        in_specs=[u_spec, w_in, w_in, w_out, w_out, a_spec, a_spec,
                  pl.BlockSpec((1, uw), lambda b, c: (0, c)), st_spec, st_spec],
        out_specs=[u_spec, st_spec, st_spec],
        out_shape=[jax.ShapeDtypeStruct((batch * seq, B_WIDTH), F32),
                   jax.ShapeDtypeStruct((batch, nch, 1, sw), F32),
                   jax.ShapeDtypeStruct((batch, nch, 1, sw), F32)],
        scratch_shapes=[pltpu.VMEM((seq, sw), F32), pltpu.VMEM((seq, sw), F32)],
        compiler_params=_params("parallel", "parallel"),
        name="s5_scan",
    )(u_perm, in_blockdiag(bb_re), in_blockdiag(bb_im), out_blockdiag(c_re), out_blockdiag(c_im),
      row(ab_re), row(ab_im), d_skip.reshape(1, B_WIDTH), st(h0_re), st(h0_im))
    return g, hfr.reshape(batch, B_GROUPS, B_STATE), hfi.reshape(batch, B_GROUPS, B_STATE)


def to_segment_order(a, batch, seq):
    seg_len = seq // S5_SEG
    return a.reshape(batch, S5_SEG, seg_len, -1).swapaxes(1, 2).reshape(batch * seq, -1)


def from_segment_order(a, batch, seq):
    seg_len = seq // S5_SEG
    return a.reshape(batch, seg_len, S5_SEG, -1).swapaxes(1, 2).reshape(batch * seq, -1)


def _merge_kernel(o0, o1, o2, l0, l1, l2, out_ref):
    la, lb, lc = l0[...], l1[...], l2[...]
    m = jnp.maximum(jnp.maximum(la, lb), lc)
    wa, wb, wc = jnp.exp(la - m), jnp.exp(lb - m), jnp.exp(lc - m)
    out = (wa * o0[...] + wb * o1[...] + wc * o2[...]) / (wa + wb + wc)
    out_ref[...] = out.astype(out_ref.dtype)


def merge_groups(outs, lses):
    m, w = outs[0].shape
    tm = min(m, 512)
    spec = pl.BlockSpec((tm, w), lambda i: (i, 0))
    return pl.pallas_call(
        _merge_kernel, grid=(m // tm,), in_specs=[spec] * 6, out_specs=spec,
        out_shape=jax.ShapeDtypeStruct((m, w), BF16), compiler_params=_params("parallel"),
        name="merge_groups",
    )(*outs, *lses)


DSA_TQ = 256
C_TQ = 128
SAMPLE_TK = PAGES_PER_STEP * PAGE_SIZE
NEW_ROWS = 128


def _pad_rows(a, rows):
    return jnp.pad(a, ((0, 0), (0, rows - a.shape[1]), (0, 0)))


def even_mixer(x, h, *, li, w_in, w_rest, w_idx_qb, w_out, w_glu, b_glu, s5c, bias_table, batch, seq, h0_re, h0_im,
               pool_kv=None, pool_kidx=None, page_table=None):
    m = batch * seq
    sample = pool_kv is not None
    ab_re, ab_im, bb_re, bb_im, c_re, c_im, d_skip = s5c

    q = linear(h, w_in, li, n_cols=Q_A_W, col_off=0, tn=512, out_dtypes=(BF16,))
    kv, kvb = linear(h, w_in, li, n_cols=KV_A_W, col_off=OFF_KV, tn=512, out_dtypes=(F32, BF16))
    cq = linear(h, w_in, li, n_cols=R_IDX, col_off=OFF_CQ, tn=512, out_dtypes=(BF16,))
    rest = linear(h, w_rest, 0, tn=256)
    u, ki, wi = rest[:, :B_WIDTH], rest[:, B_WIDTH:B_WIDTH + D_IDX], rest[:, B_WIDTH + D_IDX:B_WIDTH + D_IDX + H_IDX]
    qi = linear(cq, w_idx_qb, li, tn=512, out_dtypes=(BF16,))

    if not sample:
        k_sel = min(TOPK_MAX, seq // 4)
        scores = indexer_prompt(qi, rest, batch=batch, seq=seq, tq=DSA_TQ,
                                ki_blk=B_WIDTH // LANES, wi_blk=B_WIDTH // LANES + 1)
        sel = topk_mask(scores, k_sel, tr=64)
        lib = toeplitz_bias_tiles(bias_table, seq // DSA_TQ, DSA_TQ, 1, seq)
        att = dsa_prompt_attention(q, kvb, lib, sel, batch=batch, seq=seq, tq=DSA_TQ)
    else:
        k_sel = min(TOPK_MAX, (PAST_LEN + seq) // 4)
        ki_new = _pad_rows(ki.reshape(batch, seq, D_IDX), NEW_ROWS)
        scores = indexer_sample(qi.reshape(batch, seq, H_IDX * D_IDX), wi.reshape(batch, seq, H_IDX),
                                ki_new, pool_kidx, li, page_table, tk=SAMPLE_TK)
        s_pad = scores.shape[-1]
        sel = topk_mask(scores.reshape(m, s_pad), k_sel, tr=m).reshape(batch, seq, s_pad)
        bias = sample_bias(bias_table, PAST_LEN, seq, s_pad, lambda d: d >= 0)
        kv_new = _pad_rows(kv.reshape(batch, seq, KV_A_W), NEW_ROWS)
        qf = q.astype(F32).reshape(batch, seq, Q_A_W)
        pool = pool_kv.reshape(pool_kv.shape[0], -1, PAGE_SIZE, KV_A_W)
        att = sample_attention(qf, pool, li, kv_new, bias, sel,
                               hq=H_A, hkv=H_KV_A, page_table=page_table, tk=SAMPLE_TK, with_lse=False)
        att = att.reshape(m, Q_A_W).astype(BF16)

    g_perm, hf_re, hf_im = s5_scan(to_segment_order(u, batch, seq), h0_re, h0_im, ab_re, ab_im, bb_re, bb_im,
                                   c_re, c_im, d_skip, batch=batch, seq=seq, precise=sample)
    g = from_segment_order(g_perm, batch, seq)
    ssm = linear(g.astype(BF16), w_glu, li, tn=512, mode="glu", extras=(g, b_glu.reshape(1, B_WIDTH)),
                 out_dtypes=(BF16,))

    x = linear(att, w_out, li, row_off=0, tn=512, mode="resid", extras=(x,))
    x = linear(ssm, w_out, li, row_off=Q_A_W, tn=512, mode="resid", extras=(x,))
    return x, kv, ki, hf_re, hf_im


def odd_mixer(x, h, li, w_in, w_out, bias_table, *, batch, seq, caches=None):
    m = batch * seq
    sample = caches is not None
    q = linear(h, w_in, li, n_cols=C_WIDTH, col_off=0, tn=512, out_dtypes=(BF16,))
    kv, kvb = linear(h, w_in, li, n_cols=2 * C_WIDTH, col_off=C_WIDTH, tn=512, out_dtypes=(F32, BF16))
    kv5 = kv.reshape(batch, seq, 2, H_C, HEAD_DIM)
    outs, lses, rows = [], [], []
    for g, (window, dilation) in enumerate(C_PAIRS):
        cols = bias_table[:, g * H_C_G:(g + 1) * H_C_G]
        kv_g = kv5[:, :, :, g * H_C_G:(g + 1) * H_C_G]
        if not sample:
            lib = toeplitz_bias_tiles(cols, 2, C_TQ, dilation, window // dilation)
            o, l = dilated_prompt_group(q, kvb, lib, batch=batch, seq=seq, g=g, dilation=dilation, tq=C_TQ)
            rows.append(kv_g[:, seq - min(window, seq):])
        else:
            buf = caches[g]
            wb = buf.shape[2]
            tk = min(wb, 512)
            s_pad = wb + tk
            bias = sample_bias(cols, wb, seq, s_pad,
                               lambda d: (d >= 0) & (d <= window) & (d % dilation == 0))
            new_kv = _pad_rows(kv_g.reshape(batch, seq, 2 * C_OUT), NEW_ROWS)
            qg = q[:, g * C_OUT:(g + 1) * C_OUT].astype(F32).reshape(batch, seq, C_OUT)
            o, l = sample_attention(qg, buf.reshape(buf.shape[0], batch, wb, 2 * C_OUT), li, new_kv, bias, None,
                                    hq=H_C_G, hkv=H_C_G, tk=tk, with_lse=True)
            o, l = o.reshape(m, C_OUT), l.reshape(m, C_OUT)
            rows.append(kv_g)
        outs.append(o)
        lses.append(l)
    att = merge_groups(outs, lses)
    x = linear(att, w_out, li, tn=512, mode="resid", extras=(x,))
    return x, rows


def kernel(x_prompt, x_sample, cache_a_kv, cache_a_kidx, state_b_re, state_b_im, cache_c0_kv, cache_c1_kv, cache_c2_kv, page_table, p_prompt, p_sample, bias_table, norm_g, final_g, ffn1_w1, ffn1_w3, ffn1_w2, ffn2_w1, ffn2_w3, ffn2_w2, ple_gate, ple_proj, w_in_even, w_idx_qb, w_out_even, s5_lam_re, s5_lam_im, s5_log_step, s5_b_re, s5_b_im, s5_c_re, s5_c_im, s5_d, s5_w_glu, s5_b_glu, w_in_odd, w_out_odd):
    bp, tp, _ = x_prompt.shape
    bs, ts, _ = x_sample.shape
    xp = x_prompt.reshape(bp * tp, D_MODEL)
    xs = x_sample.reshape(bs * ts, D_MODEL)
    pp = p_prompt.reshape(DEPTH, bp * tp, D_PLE)
    ps = p_sample.reshape(DEPTH, bs * ts, D_PLE)
    outs_p = {k: [] for k in ("kv", "ki", "re", "im", "c0", "c1", "c2")}
    outs_s = {k: [] for k in ("kv", "ki", "re", "im", "c0", "c1", "c2")}
    for i in range(DEPTH):
        li = i // 2
        w2b = padded_bf16(ffn1_w2[i], D_FF_PAD)
        xp = macaron_half(xp, norm_g[i, 0], ffn1_w1, ffn1_w3, w2b, i)
        xs = macaron_half(xs, norm_g[i, 0], ffn1_w1, ffn1_w3, w2b, i)
        hp = rmsnorm(xp, norm_g[i, 1], BF16)
        hs = rmsnorm(xs, norm_g[i, 1], BF16)
        if i % 2 == 0:
            w_in = w_in_even[li]
            w_rest = jnp.concatenate([w_in[:, OFF_U:], w_in[:, OFF_KI:OFF_U], w_in[:, OFF_W:OFF_KI],
                                      jnp.zeros((D_MODEL, LANES - H_IDX), w_in.dtype)], axis=1)[None]
            s5c = (*s5_params(s5_lam_re[li], s5_lam_im[li], s5_log_step[li], s5_b_re[li], s5_b_im[li]),
                   s5_c_re[li], s5_c_im[li], s5_d[li])
            even = functools.partial(even_mixer, li=li, w_in=w_in_even, w_rest=w_rest, w_idx_qb=w_idx_qb,
                                     w_out=w_out_even, w_glu=s5_w_glu, b_glu=s5_b_glu[li], s5c=s5c,
                                     bias_table=bias_table)
            zeros = jnp.zeros((bp, B_GROUPS, B_STATE), F32)
            xp, kv, ki, hre, him = even(xp, hp, batch=bp, seq=tp, h0_re=zeros, h0_im=zeros)
            outs_p["kv"].append(kv.reshape(bp, tp, 2, H_KV_A, HEAD_DIM))
            outs_p["ki"].append(ki.reshape(bp, tp, D_IDX))
            outs_p["re"].append(hre)
            outs_p["im"].append(him)
            xs, kv, ki, hre, him = even(xs, hs, batch=bs, seq=ts, h0_re=state_b_re[li], h0_im=state_b_im[li],
                                        pool_kv=cache_a_kv, pool_kidx=cache_a_kidx, page_table=page_table)
            outs_s["kv"].append(kv.reshape(bs, ts, 2, H_KV_A, HEAD_DIM))
            outs_s["ki"].append(ki.reshape(bs, ts, D_IDX))
            outs_s["re"].append(hre)
            outs_s["im"].append(him)
        else:
            xp, rows = odd_mixer(xp, hp, li, w_in_odd, w_out_odd, bias_table, batch=bp, seq=tp)
            for g in range(N_C_GROUPS):
                outs_p["c%d" % g].append(rows[g])
            xs, rows = odd_mixer(xs, hs, li, w_in_odd, w_out_odd, bias_table, batch=bs, seq=ts,
                                 caches=(cache_c0_kv, cache_c1_kv, cache_c2_kv))
            for g in range(N_C_GROUPS):
                outs_s["c%d" % g].append(rows[g])
        w2b = padded_bf16(ffn2_w2[i], D_FF_PAD)
        xp = macaron_half(xp, norm_g[i, 2], ffn2_w1, ffn2_w3, w2b, i)
        xs = macaron_half(xs, norm_g[i, 2], ffn2_w1, ffn2_w3, w2b, i)
        xp = ple_add(xp, pp[i], norm_g[i, 3], ple_gate, ple_proj, i)
        xs = ple_add(xs, ps[i], norm_g[i, 3], ple_gate, ple_proj, i)
    y_prompt = rmsnorm(xp, final_g, F32).reshape(bp, tp, D_MODEL)
    y_sample = rmsnorm(xs, final_g, F32).reshape(bs, ts, D_MODEL)
    order = ("kv", "ki", "re", "im", "c0", "c1", "c2")
    return (y_prompt, y_sample, *[jnp.stack(outs_p[k]) for k in order], *[jnp.stack(outs_s[k]) for k in order])
```

```python
import functools
import math

import jax
import jax.numpy as jnp
from jax import lax
from jax.experimental import pallas as pl
from jax.experimental.pallas import tpu as pltpu

D_MODEL = 4096
DEPTH = 2
PAST_LEN = 16384
PAGE_SIZE = 128
HEAD_DIM = 128
EPS = 1e-6
H_A = 24
H_KV_A = 8
H_IDX = 32
D_IDX = 128
R_IDX = 512
TOPK_MAX = 256
B_WIDTH = 1024
B_GROUP = 16
B_GROUPS = 64
B_STATE = 64
C_PAIRS = ((128, 1), (512, 4), (2048, 16))
N_C_GROUPS = 3
H_C_G = 8
H_C = 24
C_WIDTH = H_C * HEAD_DIM
C_OUT = H_C_G * HEAD_DIM
NUM_BUCKETS = 32
MAX_DISTANCE = 2048
D_FF = 11008
D_PLE = 256
Q_A_W = H_A * HEAD_DIM
KV_A_W = 2 * H_KV_A * HEAD_DIM
OFF_KV = Q_A_W
OFF_CQ = OFF_KV + KV_A_W
OFF_W = OFF_CQ + R_IDX
OFF_KI = OFF_W + H_IDX
OFF_U = OFF_KI + D_IDX
IN_EVEN = OFF_U + B_WIDTH

LANES = 128
SUBLANES = 8
VMEM_LIMIT = 56 * 1024 * 1024

S5_SEG = SUBLANES
S5_CHUNK_GROUPS = 8
PAGES_PER_STEP = 8
NEG_INIT = -1e30

F32 = jnp.float32
BF16 = jnp.bfloat16
NT_DIMS = (((1,), (1,)), ((), ()))


def _params(*sem):
    return pltpu.CompilerParams(dimension_semantics=sem, vmem_limit_bytes=VMEM_LIMIT)


def _rms_kernel(x_ref, g_ref, o_ref):
    x = x_ref[...]
    ms = jnp.mean(x * x, axis=-1, keepdims=True)
    o_ref[...] = (x * lax.rsqrt(ms + EPS) * g_ref[...]).astype(o_ref.dtype)


def rmsnorm(x, g, out_dtype):
    m, d = x.shape
    tm = min(m, 256)
    return pl.pallas_call(
        _rms_kernel,
        grid=(m // tm,),
        in_specs=[pl.BlockSpec((tm, d), lambda i: (i, 0)), pl.BlockSpec((1, d), lambda i: (0, 0))],
        out_specs=pl.BlockSpec((tm, d), lambda i: (i, 0)),
        out_shape=jax.ShapeDtypeStruct((m, d), out_dtype),
        compiler_params=_params("parallel"),
        name="rmsnorm",
    )(x, g.reshape(1, d))


def _linear_kernel(*refs, mode, n_out, scale):
    a_ref, w_ref = refs[0], refs[1]
    extras = refs[2:len(refs) - n_out]
    outs = refs[len(refs) - n_out:]
    acc = jnp.dot(a_ref[...].astype(BF16), w_ref[...].astype(BF16), preferred_element_type=F32)
    if mode == "none":
        res = acc
    elif mode == "resid":
        res = extras[0][...] + scale * acc
    elif mode == "glu":
        g = extras[0][...]
        res = g * jax.nn.sigmoid(acc + extras[1][...])
    elif mode == "ple":
        x_ref, p_ref, wp_ref = extras
        proj = jnp.dot(p_ref[...].astype(BF16), wp_ref[...].astype(BF16), preferred_element_type=F32)
        res = x_ref[...] + jax.nn.sigmoid(acc) * proj
    else:
        raise ValueError(mode)
    for o in outs:
        o[...] = res.astype(o.dtype)


def linear(a, w, li, *, n_cols=None, col_off=0, row_off=0, tn, mode="none", extras=(),
           out_dtypes=(F32,), scale=1.0):
    m, k = a.shape
    n_cols = w.shape[2] if n_cols is None else n_cols
    tm = min(m, 1024)
    assert m % tm == 0 and n_cols % tn == 0 and col_off % tn == 0 and row_off % k == 0
    jo, ro = col_off // tn, row_off // k
    tile = pl.BlockSpec((tm, tn), lambda i, j: (i, j))
    in_specs = [pl.BlockSpec((tm, k), lambda i, j: (i, 0)),
                pl.BlockSpec((None, k, tn), lambda i, j: (li, ro, j + jo))]
    if mode == "resid":
        in_specs += [tile]
    elif mode == "glu":
        in_specs += [tile, pl.BlockSpec((1, tn), lambda i, j: (0, j))]
    elif mode == "ple":
        kp = extras[1].shape[1]
        in_specs += [tile, pl.BlockSpec((tm, kp), lambda i, j: (i, 0)),
                     pl.BlockSpec((None, kp, tn), lambda i, j: (li, 0, j))]
    res = pl.pallas_call(
        functools.partial(_linear_kernel, mode=mode, n_out=len(out_dtypes), scale=scale),
        grid=(m // tm, n_cols // tn),
        in_specs=in_specs,
        out_specs=[tile] * len(out_dtypes),
        out_shape=[jax.ShapeDtypeStruct((m, n_cols), dt) for dt in out_dtypes],
        compiler_params=_params("parallel", "arbitrary"),
        name="linear_" + mode,
    )(a, w, *extras)
    return res if len(out_dtypes) > 1 else res[0]


FFN_TN = 256
DOWN_TK = 512


def _swiglu_kernel(a_ref, w1_ref, w3_ref, o_ref, *, n_real):
    j = pl.program_id(1)

    @pl.when(j < n_real)
    def _():
        a = a_ref[...]
        g = jnp.dot(a, w1_ref[...].astype(BF16), preferred_element_type=F32)
        u = jnp.dot(a, w3_ref[...].astype(BF16), preferred_element_type=F32)
        o_ref[...] = (jax.nn.silu(g) * u).astype(o_ref.dtype)

    @pl.when(j >= n_real)
    def _():
        o_ref[...] = jnp.zeros(o_ref.shape, o_ref.dtype)


def swiglu_up(h, w1, w3, li, *, n_pad):
    m, k = h.shape
    n = w1.shape[2]
    tn = FFN_TN
    tm = min(m, 1024)
    n_real = n // tn
    w_spec = pl.BlockSpec((None, k, tn), lambda i, j: (li, 0, jnp.minimum(j, n_real - 1)))
    return pl.pallas_call(
        functools.partial(_swiglu_kernel, n_real=n_real),
        grid=(m // tm, n_pad // tn),
        in_specs=[pl.BlockSpec((tm, k), lambda i, j: (i, 0)), w_spec, w_spec],
        out_specs=pl.BlockSpec((tm, tn), lambda i, j: (i, j)),
        out_shape=jax.ShapeDtypeStruct((m, n_pad), BF16),
        compiler_params=_params("parallel", "arbitrary"),
        name="swiglu_up",
    )(h, w1, w3)


def _down_kernel(a_ref, w_ref, x_ref, o_ref, *, scale, tail_rows):
    kk = pl.program_id(2)
    last = pl.num_programs(2) - 1

    @pl.when(kk == 0)
    def _():
        o_ref[...] = x_ref[...]

    @pl.when(kk < last)
    def _():
        o_ref[...] += scale * jnp.dot(a_ref[...], w_ref[...].astype(BF16), preferred_element_type=F32)

    @pl.when(kk == last)
    def _():
        w = w_ref[...]
        row = lax.broadcasted_iota(jnp.int32, w.shape, 0)
        w = jnp.where(row < tail_rows, w, 0.0).astype(BF16)
        o_ref[...] += scale * jnp.dot(a_ref[...], w, preferred_element_type=F32)


def down_resid(act, w, li, x, *, scale):
    m, k_pad = act.shape
    k, n = w.shape[1:]
    tm = min(m, 2048)
    tn = min(n, 1024)
    tk = DOWN_TK
    assert k_pad % tk == 0 and k_pad - k < tk
    return pl.pallas_call(
        functools.partial(_down_kernel, scale=scale, tail_rows=tk - (k_pad - k)),
        grid=(m // tm, n // tn, k_pad // tk),
        in_specs=[pl.BlockSpec((tm, tk), lambda i, j, kk: (i, kk)),
                  pl.BlockSpec((None, tk, tn), lambda i, j, kk: (li, kk, j)),
                  pl.BlockSpec((tm, tn), lambda i, j, kk: (i, j))],
        out_specs=pl.BlockSpec((tm, tn), lambda i, j, kk: (i, j)),
        out_shape=jax.ShapeDtypeStruct((m, n), F32),
        compiler_params=_params("parallel", "parallel", "arbitrary"),
        name="down_resid",
    )(act, w, x)


def macaron_half(x, g, w1, w3, w2, li):
    h = rmsnorm(x, g, BF16)
    n_pad = -(-w1.shape[2] // DOWN_TK) * DOWN_TK
    act = swiglu_up(h, w1, w3, li, n_pad=n_pad)
    return down_resid(act, w2, li, x, scale=0.5)


def ple_add(x, p, g, w_gate, w_proj, li):
    h = rmsnorm(x, g, BF16)
    return linear(h, w_gate, li, tn=256, mode="ple", extras=(x, p.astype(BF16), w_proj))


def t5_bucket(dist):
    max_exact = NUM_BUCKETS // 2
    d = jnp.maximum(dist, 0)
    ratio = jnp.log(jnp.maximum(d, 1).astype(F32) / max_exact) / math.log(MAX_DISTANCE / max_exact)
    large = jnp.minimum(max_exact + (ratio * (NUM_BUCKETS - max_exact)).astype(jnp.int32), NUM_BUCKETS - 1)
    return jnp.where(d < max_exact, d, large)


def bias_of_dist(bias_cols, dist, valid):
    onehot = t5_bucket(dist)[..., None] == jnp.arange(NUM_BUCKETS)
    b = jnp.sum(jnp.where(onehot[..., None], bias_cols.astype(F32), 0.0), axis=-2)
    b = jnp.where(valid[..., None], b, -jnp.inf)
    return jnp.moveaxis(b, -1, 0)


def sample_bias(bias_cols, base, t, s_pad, valid_fn):
    n = jnp.arange(s_pad + t - 1)
    dist = base + (t - 1) - n
    table = bias_of_dist(bias_cols, dist, valid_fn(dist))
    return jnp.stack([table[:, t - 1 - i:t - 1 - i + s_pad] for i in range(t)], axis=1)


def _toeplitz_kernel(w_ref, o_ref):
    nd, tq, _ = o_ref.shape
    for d in range(nd):
        x = jnp.broadcast_to(w_ref[d], (tq, 2 * tq))
        r = pltpu.roll(x, 0, 1, stride=1, stride_axis=0)
        o_ref[d] = r[:, tq:]


def toeplitz_bias_tiles(bias_cols, n_delta, tq, step, max_steps):
    h = bias_cols.shape[1]
    steps = jnp.arange(n_delta)[:, None] * tq + tq - jnp.arange(2 * tq)[None, :]
    rows = bias_of_dist(bias_cols, steps * step, (steps >= 0) & (steps <= max_steps))
    return pl.pallas_call(
        _toeplitz_kernel,
        grid=(h,),
        in_specs=[pl.BlockSpec((None, n_delta, 1, 2 * tq), lambda i: (i, 0, 0, 0))],
        out_specs=pl.BlockSpec((None, n_delta, tq, tq), lambda i: (i, 0, 0, 0)),
        out_shape=jax.ShapeDtypeStruct((h, n_delta, tq, tq), F32),
        compiler_params=_params("parallel"),
        name="toeplitz_bias_tiles",
    )(rows.reshape(h, n_delta, 1, 2 * tq))


def _flash_init(m_ref, l_ref, acc_ref):
    m_ref[...] = jnp.full(m_ref.shape, NEG_INIT, F32)
    l_ref[...] = jnp.zeros(l_ref.shape, F32)
    acc_ref[...] = jnp.zeros(acc_ref.shape, F32)


def _flash_step(q_ref, k_at, v_at, bias_at, sel, m_ref, l_ref, acc_ref, *, hq, hkv):
    rep = hq // hkv
    tq = q_ref.shape[0]
    scale = HEAD_DIM ** -0.5
    stack = lambda parts: parts[0] if rep == 1 else jnp.concatenate(parts, axis=0)
    sel_rows = None if sel is None else stack([sel] * rep)
    for g in range(hkv):
        heads = range(g * rep, (g + 1) * rep)
        cols = [slice(h * HEAD_DIM, (h + 1) * HEAD_DIM) for h in heads]
        qs = stack([q_ref[:, c] for c in cols]).astype(BF16)
        s = lax.dot_general(qs, k_at(g), NT_DIMS, preferred_element_type=F32) * scale
        s = s + stack([bias_at(h) for h in heads])
        if sel_rows is not None:
            s = jnp.where(sel_rows, s, -jnp.inf)
        gsl = slice(g * rep, (g + 1) * rep)
        m_old = m_ref[gsl].reshape(rep * tq, LANES)
        m_new = jnp.maximum(m_old, jnp.max(s, axis=1, keepdims=True))
        alpha = jnp.exp(m_old - m_new)
        p = jnp.exp(s - m_new[:, :1])
        l_new = alpha * l_ref[gsl].reshape(rep * tq, LANES) + jnp.sum(p, axis=1, keepdims=True)
        pv = jnp.dot(p.astype(BF16), v_at(g), preferred_element_type=F32)
        for r, c in enumerate(cols):
            rows = slice(r * tq, (r + 1) * tq)
            acc_ref[:, c] = alpha[rows] * acc_ref[:, c] + pv[rows]
        l_ref[gsl] = l_new.reshape(rep, tq, LANES)
        m_ref[gsl] = m_new.reshape(rep, tq, LANES)


def _flash_finish(o_ref, lse_ref, m_ref, l_ref, acc_ref, *, hq):
    for h in range(hq):
        hs = slice(h * HEAD_DIM, (h + 1) * HEAD_DIM)
        l = l_ref[h]
        o_ref[:, hs] = (acc_ref[:, hs] / l).astype(o_ref.dtype)
        if lse_ref is not None:
            lse_ref[:, hs] = m_ref[h] + jnp.log(l)


def _flash_scratch(tq, hq):
    return [pltpu.VMEM((hq, tq, LANES), F32), pltpu.VMEM((hq, tq, LANES), F32),
            pltpu.VMEM((tq, hq * HEAD_DIM), F32)]


def _prompt_attn_kernel(*refs, hq, hkv, has_sel, with_lse, first_kb):
    q_ref, k_ref, v_ref, bias_ref = refs[:4]
    pos = 4
    sel_ref = None
    if has_sel:
        sel_ref = refs[pos]
        pos += 1
    o_ref = refs[pos]
    pos += 1
    lse_ref = None
    if with_lse:
        lse_ref = refs[pos]
        pos += 1
    m_ref, l_ref, acc_ref = refs[pos:pos + 3]
    qb = pl.program_id(2)
    kk = pl.program_id(3)
    nk = pl.num_programs(3)

    @pl.when(kk == 0)
    def _():
        _flash_init(m_ref, l_ref, acc_ref)

    @pl.when(first_kb(qb, kk) >= 0)
    def _():
        sel = None if sel_ref is None else sel_ref[...] > 0.0
        head = lambda ref: (lambda g: ref[:, g * HEAD_DIM:(g + 1) * HEAD_DIM].astype(BF16))
        _flash_step(q_ref, head(k_ref), head(v_ref), lambda h: bias_ref[h], sel, m_ref, l_ref, acc_ref,
                    hq=hq, hkv=hkv)

    @pl.when(kk == nk - 1)
    def _():
        _flash_finish(o_ref, lse_ref, m_ref, l_ref, acc_ref, hq=hq)


def dsa_prompt_attention(q, kvb, bias_lib, sel, *, batch, seq, tq):
    nq = seq // tq
    grid = (batch, 1, nq, nq)

    def kb_of(qb, kk):
        return jnp.minimum(kk, qb)

    in_specs = [
        pl.BlockSpec((tq, Q_A_W), lambda b, r, qb, kk: (b * nq + qb, 0)),
        pl.BlockSpec((tq, H_KV_A * HEAD_DIM), lambda b, r, qb, kk: (b * nq + kb_of(qb, kk), 0)),
        pl.BlockSpec((tq, H_KV_A * HEAD_DIM), lambda b, r, qb, kk: (b * nq + kb_of(qb, kk), 1)),
        pl.BlockSpec((H_A, None, tq, tq), lambda b, r, qb, kk: (0, qb - kb_of(qb, kk), 0, 0)),
        pl.BlockSpec((tq, tq), lambda b, r, qb, kk: (b * nq + qb, kb_of(qb, kk))),
    ]
    return pl.pallas_call(
        functools.partial(_prompt_attn_kernel, hq=H_A, hkv=H_KV_A, has_sel=True, with_lse=False,
                          first_kb=lambda qb, kk: qb - kk),
        grid=grid,
        in_specs=in_specs,
        out_specs=pl.BlockSpec((tq, Q_A_W), lambda b, r, qb, kk: (b * nq + qb, 0)),
        out_shape=jax.ShapeDtypeStruct((batch * seq, Q_A_W), BF16),
        scratch_shapes=_flash_scratch(tq, H_A),
        compiler_params=_params("parallel", "arbitrary", "arbitrary", "arbitrary"),
        name="dsa_prompt_attention",
    )(q, kvb, kvb, bias_lib, sel)


def dilated_prompt_group(q, kvb, bias_lib, *, batch, seq, g, dilation, tq):
    tl = seq // dilation
    nq = tl // tq
    qv = q.reshape(batch, tl, dilation * C_WIDTH)
    kvv = kvb.reshape(batch, tl, dilation * 2 * C_WIDTH)
    qcols, kvcols = C_WIDTH // C_OUT, 2 * C_WIDTH // C_OUT

    def kb_of(qb, kk):
        return jnp.maximum(qb - 1 + kk, 0)

    in_specs = [
        pl.BlockSpec((None, tq, C_OUT), lambda b, r, qb, kk: (b, qb, r * qcols + g)),
        pl.BlockSpec((None, tq, C_OUT), lambda b, r, qb, kk: (b, kb_of(qb, kk), r * kvcols + g)),
        pl.BlockSpec((None, tq, C_OUT), lambda b, r, qb, kk: (b, kb_of(qb, kk), r * kvcols + N_C_GROUPS + g)),
        pl.BlockSpec((H_C_G, None, tq, tq), lambda b, r, qb, kk: (0, 1 - kk, 0, 0)),
    ]
    out_spec = pl.BlockSpec((None, tq, C_OUT), lambda b, r, qb, kk: (b, qb, r))
    o, lse = pl.pallas_call(
        functools.partial(_prompt_attn_kernel, hq=H_C_G, hkv=H_C_G, has_sel=False, with_lse=True,
                          first_kb=lambda qb, kk: qb - 1 + kk),
        grid=(batch, dilation, nq, 2),
        in_specs=in_specs,
        out_specs=[out_spec, out_spec],
        out_shape=[jax.ShapeDtypeStruct((batch, tl, dilation * C_OUT), F32)] * 2,
        scratch_shapes=_flash_scratch(tq, H_C_G),
        compiler_params=_params("parallel", "parallel", "arbitrary", "arbitrary"),
        name="dilated_prompt_group",
    )(qv, kvv, kvv, bias_lib)
    return o.reshape(batch * seq, C_OUT), lse.reshape(batch * seq, C_OUT)


def _sample_attn_kernel(*refs, hq, hkv, n_src, has_sel, with_lse, has_table):
    if has_table:
        refs = refs[1:]
    q_ref = refs[0]
    src_refs = refs[1:1 + n_src]
    new_ref, bias_ref = refs[1 + n_src], refs[2 + n_src]
    pos = 3 + n_src
    sel_ref = None
    if has_sel:
        sel_ref = refs[pos]
        pos += 1
    o_ref = refs[pos]
    pos += 1
    lse_ref = None
    if with_lse:
        lse_ref = refs[pos]
        pos += 1
    m_ref, l_ref, acc_ref = refs[pos:pos + 3]
    kb = pl.program_id(1)
    nk = pl.num_programs(1)
    kw = hkv * HEAD_DIM
    new_rows = new_ref.shape[0]

    @pl.when(kb == 0)
    def _():
        _flash_init(m_ref, l_ref, acc_ref)

    def cached(which):
        def at(g):
            parts = [r[:, which, g, :] for r in src_refs]
            return (parts[0] if n_src == 1 else jnp.concatenate(parts, axis=0)).astype(BF16)
        return at

    def fresh(which):
        return lambda g: new_ref[:, which * kw + g * HEAD_DIM:which * kw + (g + 1) * HEAD_DIM].astype(BF16)

    @pl.when(kb < nk - 1)
    def _():
        sel = None if sel_ref is None else sel_ref[...] > 0.0
        _flash_step(q_ref, cached(0), cached(1), lambda h: bias_ref[h], sel,
                    m_ref, l_ref, acc_ref, hq=hq, hkv=hkv)

    @pl.when(kb == nk - 1)
    def _():
        sel = None if sel_ref is None else sel_ref[:, :new_rows] > 0.0
        _flash_step(q_ref, fresh(0), fresh(1), lambda h: bias_ref[h, :, :new_rows],
                    sel, m_ref, l_ref, acc_ref, hq=hq, hkv=hkv)
        _flash_finish(o_ref, lse_ref, m_ref, l_ref, acc_ref, hq=hq)


def sample_attention(q, src, li, new_kv, bias, sel, *, hq, hkv, page_table=None, tk, with_lse):
    bsz, t, _ = q.shape
    kvw = 2 * hkv * HEAD_DIM
    new_rows = new_kv.shape[1]
    paged = page_table is not None
    if paged:
        n_src = tk // PAGE_SIZE
        nk = page_table.shape[1] // n_src
        src_specs = [pl.BlockSpec((None, None, PAGE_SIZE, 2, hkv, HEAD_DIM),
                                  functools.partial(lambda p, b, kb, pt: (li, pt[b, jnp.minimum(kb, nk - 1) * n_src + p], 0, 0, 0, 0), p))
                     for p in range(n_src)]
        ix = lambda f: (lambda b, kb, pt: f(b, kb))
    else:
        n_src = 1
        nk = src.shape[2] // tk
        src_specs = [pl.BlockSpec((None, None, tk, 2, hkv, HEAD_DIM),
                                  lambda b, kb: (li, b, jnp.minimum(kb, nk - 1), 0, 0, 0))]
        ix = lambda f: f
    in_specs = ([pl.BlockSpec((None, t, hq * HEAD_DIM), ix(lambda b, kb: (b, 0, 0)))] + src_specs +
                [pl.BlockSpec((None, new_rows, kvw), ix(lambda b, kb: (b, 0, 0))),
                 pl.BlockSpec((hq, t, tk), ix(lambda b, kb: (0, 0, kb)))])
    args = [q] + [src] * n_src + [new_kv, bias]
    if sel is not None:
        in_specs.append(pl.BlockSpec((None, t, tk), ix(lambda b, kb: (b, 0, kb))))
        args.append(sel)
    out_spec = pl.BlockSpec((None, t, hq * HEAD_DIM), ix(lambda b, kb: (b, 0, 0)))
    n_out = 2 if with_lse else 1
    kern = functools.partial(_sample_attn_kernel, hq=hq, hkv=hkv, n_src=n_src, has_sel=sel is not None,
                             with_lse=with_lse, has_table=paged)
    out_shape = [jax.ShapeDtypeStruct((bsz, t, hq * HEAD_DIM), F32)] * n_out
    scratch = _flash_scratch(t, hq)
    cp = _params("parallel", "arbitrary")
    if paged:
        gs = pltpu.PrefetchScalarGridSpec(num_scalar_prefetch=1, grid=(bsz, nk + 1), in_specs=in_specs,
                                          out_specs=[out_spec] * n_out, scratch_shapes=scratch)
        res = pl.pallas_call(kern, grid_spec=gs, out_shape=out_shape, compiler_params=cp,
                             name="sample_attention_paged")(page_table, *args)
    else:
        res = pl.pallas_call(kern, grid=(bsz, nk + 1), in_specs=in_specs, out_specs=[out_spec] * n_out,
                             out_shape=out_shape, scratch_shapes=scratch, compiler_params=cp,
                             name="sample_attention")(*args)
    return res if with_lse else res[0]


IDX_SCALE = (H_IDX ** -0.5) * (D_IDX ** -0.5)


def _idx_prompt_kernel(qi_ref, ki_ref, wi_ref, o_ref, *, tq):
    qb, kb = pl.program_id(1), pl.program_id(2)

    @pl.when(kb > qb)
    def _():
        o_ref[...] = jnp.full(o_ref.shape, -jnp.inf, F32)

    @pl.when(kb <= qb)
    def _():
        kt = ki_ref[...].astype(BF16)
        w = wi_ref[...] * IDX_SCALE
        acc = jnp.zeros((tq, tq), F32)
        for h in range(H_IDX):
            s = lax.dot_general(qi_ref[:, h * D_IDX:(h + 1) * D_IDX], kt, NT_DIMS, preferred_element_type=F32)
            acc = acc + jnp.maximum(s, 0.0) * w[:, h:h + 1]
        row = lax.broadcasted_iota(jnp.int32, (tq, tq), 0) + qb * tq
        col = lax.broadcasted_iota(jnp.int32, (tq, tq), 1) + kb * tq
        o_ref[...] = jnp.where(col <= row, acc, -jnp.inf)


def indexer_prompt(qi, rest, *, batch, seq, tq, ki_blk, wi_blk):
    nq = seq // tq
    return pl.pallas_call(
        functools.partial(_idx_prompt_kernel, tq=tq),
        grid=(batch, nq, nq),
        in_specs=[pl.BlockSpec((tq, H_IDX * D_IDX), lambda b, qb, kb: (b * nq + qb, 0)),
                  pl.BlockSpec((tq, D_IDX), lambda b, qb, kb: (b * nq + jnp.minimum(kb, qb), ki_blk)),
                  pl.BlockSpec((tq, LANES), lambda b, qb, kb: (b * nq + qb, wi_blk))],
        out_specs=pl.BlockSpec((tq, tq), lambda b, qb, kb: (b * nq + qb, kb)),
        out_shape=jax.ShapeDtypeStruct((batch * seq, seq), F32),
        compiler_params=_params("parallel", "parallel", "arbitrary"),
        name="indexer_prompt",
    )(qi, rest, rest)


def _split_dot_nt(e, x):
    hi = x.astype(BF16)
    lo = (x - hi.astype(F32)).astype(BF16)
    return (lax.dot_general(e, hi, NT_DIMS, preferred_element_type=F32) +
            lax.dot_general(e, lo, NT_DIMS, preferred_element_type=F32))


def _idx_sample_kernel(pt_ref, qit_ref, w_ref, e_ref, *refs, n_src, t):
    del pt_ref
    src_refs = refs[:n_src]
    new_ref, o_ref = refs[n_src], refs[n_src + 1]
    kb = pl.program_id(1)
    nk = pl.num_programs(1)
    tk = o_ref.shape[1]
    new_rows = new_ref.shape[0]

    def scores(keys):
        st = jnp.dot(keys.astype(BF16), qit_ref[...], preferred_element_type=F32)
        x = jnp.maximum(st, 0.0) * w_ref[...]
        return _split_dot_nt(e_ref[...], x)[:t]

    @pl.when(kb < nk - 1)
    def _():
        o_ref[...] = scores(jnp.concatenate([r[...] for r in src_refs], axis=0))

    @pl.when(kb == nk - 1)
    def _():
        sc = scores(new_ref[...])
        row = lax.broadcasted_iota(jnp.int32, (t, new_rows), 0)
        col = lax.broadcasted_iota(jnp.int32, (t, new_rows), 1)
        o_ref[...] = jnp.full(o_ref.shape, -jnp.inf, F32)
        o_ref[:, :new_rows] = jnp.where(col <= row, sc, -jnp.inf)


def indexer_sample(qi, wi, ki_new, pool_kidx, li, page_table, *, tk):
    bsz, t, _ = qi.shape
    n_src = tk // PAGE_SIZE
    nk = page_table.shape[1] // n_src
    qit = qi.reshape(bsz, t * H_IDX, D_IDX).transpose(0, 2, 1)
    w = (wi * IDX_SCALE).reshape(bsz, 1, t * H_IDX)
    e_rows = 2 * SUBLANES
    e = jnp.repeat(jnp.eye(e_rows, t, dtype=BF16), H_IDX, axis=1)
    src_specs = [pl.BlockSpec((None, None, PAGE_SIZE, D_IDX),
                              functools.partial(lambda p, b, kb, pt: (li, pt[b, jnp.minimum(kb, nk - 1) * n_src + p], 0, 0), p))
                 for p in range(n_src)]
    gs = pltpu.PrefetchScalarGridSpec(
        num_scalar_prefetch=1, grid=(bsz, nk + 1),
        in_specs=[pl.BlockSpec((None, D_IDX, t * H_IDX), lambda b, kb, pt: (b, 0, 0)),
                  pl.BlockSpec((None, 1, t * H_IDX), lambda b, kb, pt: (b, 0, 0)),
                  pl.BlockSpec((e_rows, t * H_IDX), lambda b, kb, pt: (0, 0))] + src_specs +
                 [pl.BlockSpec((None, ki_new.shape[1], D_IDX), lambda b, kb, pt: (b, 0, 0))],
        out_specs=pl.BlockSpec((None, t, tk), lambda b, kb, pt: (b, 0, kb)))
    return pl.pallas_call(
        functools.partial(_idx_sample_kernel, n_src=n_src, t=t),
        grid_spec=gs,
        out_shape=jax.ShapeDtypeStruct((bsz, t, (nk + 1) * tk), F32),
        compiler_params=_params("parallel", "arbitrary"),
        name="indexer_sample",
    )(page_table, qit, w, e, *([pool_kidx] * n_src), ki_new)


TOPK_CHUNK = 512
INT_MIN = -2 ** 31


def _topk_kernel(s_ref, u_ref, o_ref, *, k):
    x = s_ref[...]
    rows, n = x.shape
    x = jnp.where(x == 0.0, 0.0, x)
    bits = lax.bitcast_convert_type(x, jnp.int32)
    key = bits ^ ((bits >> 31) & jnp.int32(0x7FFFFFFF))
    kf = jnp.float32(k)

    def body(it, tau_u):
        cand_u = tau_u | jnp.left_shift(jnp.int32(1), 31 - it)
        cand = cand_u ^ jnp.int32(INT_MIN)
        cnt = jnp.sum(jnp.where(key >= cand, 1.0, 0.0), axis=1, keepdims=True)
        return jnp.where(cnt >= kf, cand_u, tau_u)

    tau_u = lax.fori_loop(0, 32, body, jnp.zeros((rows, 1), jnp.int32))
    tau = tau_u ^ jnp.int32(INT_MIN)
    gt = key > tau
    eq = key == tau
    need = kf - jnp.sum(jnp.where(gt, 1.0, 0.0), axis=1, keepdims=True)
    run = jnp.zeros((rows, 1), F32)
    for c in range(n // TOPK_CHUNK):
        cs = slice(c * TOPK_CHUNK, (c + 1) * TOPK_CHUNK)
        e = jnp.where(eq[:, cs], 1.0, 0.0)
        before = jnp.dot(e.astype(BF16), u_ref[...], preferred_element_type=F32) + run
        o_ref[:, cs] = jnp.where(gt[:, cs] | (eq[:, cs] & (before < need)), 1.0, 0.0)
        run = run + jnp.sum(e, axis=1, keepdims=True)


def topk_mask(scores, k, *, tr):
    r, s = scores.shape
    assert s % TOPK_CHUNK == 0 and r % tr == 0
    ii = jnp.arange(TOPK_CHUNK)
    upper = (ii[:, None] < ii[None, :]).astype(BF16)
    return pl.pallas_call(
        functools.partial(_topk_kernel, k=k),
        grid=(r // tr,),
        in_specs=[pl.BlockSpec((tr, s), lambda i: (i, 0)),
                  pl.BlockSpec((TOPK_CHUNK, TOPK_CHUNK), lambda i: (0, 0))],
        out_specs=pl.BlockSpec((tr, s), lambda i: (i, 0)),
        out_shape=jax.ShapeDtypeStruct((r, s), F32),
        compiler_params=_params("parallel"),
        name="topk_mask",
    )(scores, upper)


def _s5_param_kernel(lr_ref, li_ref, ls_ref, br_ref, bi_ref, abr_ref, abi_ref, bbr_ref, bbi_ref):
    lr, li = lr_ref[...], li_ref[...]
    step = jnp.exp(ls_ref[...])
    mag = jnp.exp(lr * step)
    ab_re, ab_im = mag * jnp.cos(li * step), mag * jnp.sin(li * step)
    den = lr * lr + li * li
    nr, ni = ab_re - 1.0, ab_im
    f_re, f_im = (nr * lr + ni * li) / den, (ni * lr - nr * li) / den
    br, bi = br_ref[...], bi_ref[...]
    abr_ref[...] = ab_re
    abi_ref[...] = ab_im
    bbr_ref[...] = f_re * br - f_im * bi
    bbi_ref[...] = f_re * bi + f_im * br


def s5_params(lam_re, lam_im, log_step, b_re, b_im):
    w = B_STATE * B_GROUP
    rep = lambda a: jnp.repeat(a, B_GROUP, axis=1)
    ls = jnp.broadcast_to(log_step[:, None], (B_GROUPS, w))
    shp = jax.ShapeDtypeStruct((B_GROUPS, w), F32)
    abr, abi, bbr, bbi = pl.pallas_call(_s5_param_kernel, out_shape=[shp] * 4, name="s5_params")(
        rep(lam_re), rep(lam_im), ls, b_re.reshape(B_GROUPS, w), b_im.reshape(B_GROUPS, w))
    return (abr[:, ::B_GROUP], abi[:, ::B_GROUP],
            bbr.reshape(B_GROUPS, B_STATE, B_GROUP), bbi.reshape(B_GROUPS, B_STATE, B_GROUP))


def _cmul(ar, ai, br, bi):
    return ar * br - ai * bi, ar * bi + ai * br


def _s5_kernel(u_ref, bbr_ref, bbi_ref, ccr_ref, cci_ref, ar_ref, ai_ref, d_ref, h0r_ref, h0i_ref,
               g_ref, hfr_ref, hfi_ref, xr_ref, xi_ref, *, seg_len, precise):
    u = u_ref[...]
    if precise:
        mm = lambda a, b: jnp.dot(a, b, preferred_element_type=F32, precision=lax.Precision.HIGHEST)
    else:
        mm = lambda a, b: jnp.dot(a.astype(BF16), b.astype(BF16), preferred_element_type=F32)
    xr_ref[...] = mm(u, bbr_ref[...])
    xi_ref[...] = mm(u, bbi_ref[...])
    sw = xr_ref.shape[1]
    ar = jnp.broadcast_to(ar_ref[...], (S5_SEG, sw))
    ai = jnp.broadcast_to(ai_ref[...], (S5_SEG, sw))

    def rows(j):
        return pl.ds(pl.multiple_of(j * S5_SEG, S5_SEG), S5_SEG)

    def local_scan(j, carry):
        hr, hi = _cmul(ar, ai, *carry)
        hr = hr + xr_ref[rows(j), :]
        hi = hi + xi_ref[rows(j), :]
        xr_ref[rows(j), :] = hr
        xi_ref[rows(j), :] = hi
        return hr, hi

    zero = jnp.zeros((S5_SEG, sw), F32)
    fr, fi = lax.fori_loop(0, seg_len, local_scan, (zero, zero))

    pr, pi = ar[:1], ai[:1]
    for _ in range(int(math.log2(seg_len))):
        pr, pi = _cmul(pr, pi, pr, pi)

    cr, ci = h0r_ref[...], h0i_ref[...]
    crs, cis = [], []
    for s in range(S5_SEG):
        crs.append(cr)
        cis.append(ci)
        tr, ti = _cmul(pr, pi, cr, ci)
        cr, ci = fr[s:s + 1] + tr, fi[s:s + 1] + ti
    hfr_ref[...] = cr
    hfi_ref[...] = ci
    c_re = jnp.concatenate(crs, axis=0)
    c_im = jnp.concatenate(cis, axis=0)

    def add_carry(j, pw):
        tr, ti = _cmul(pw[0], pw[1], c_re, c_im)
        xr_ref[rows(j), :] += tr
        xi_ref[rows(j), :] += ti
        return _cmul(pw[0], pw[1], ar, ai)

    lax.fori_loop(0, seg_len, add_carry, (ar, ai))

    y = mm(xr_ref[...], ccr_ref[...]) - mm(xi_ref[...], cci_ref[...]) + d_ref[...] * u
    g_ref[...] = jax.nn.gelu(y)


def s5_scan(u_perm, h0_re, h0_im, ab_re, ab_im, bb_re, bb_im, c_re, c_im, d_skip, *, batch, seq, precise):
    seg_len = seq // S5_SEG
    assert seg_len & (seg_len - 1) == 0
    nch = B_GROUPS // S5_CHUNK_GROUPS
    uw, sw = S5_CHUNK_GROUPS * B_GROUP, S5_CHUNK_GROUPS * B_STATE
    eye = jnp.eye(S5_CHUNK_GROUPS, dtype=F32)

    def in_blockdiag(bb):
        bb = bb.reshape(nch, S5_CHUNK_GROUPS, B_STATE, B_GROUP)
        return jnp.einsum("kgnc,gh->kgchn", bb, eye).reshape(nch, uw, sw)

    def out_blockdiag(cc):
        cc = cc.reshape(nch, S5_CHUNK_GROUPS, B_GROUP, B_STATE)
        return jnp.einsum("kgcn,gh->kgnhc", cc, eye).reshape(nch, sw, uw)

    row = lambda a: a.reshape(nch, 1, sw)
    st = lambda a: a.astype(F32).reshape(batch, nch, 1, sw)
    w_in = pl.BlockSpec((None, uw, sw), lambda b, c: (c, 0, 0))
    w_out = pl.BlockSpec((None, sw, uw), lambda b, c: (c, 0, 0))
    a_spec = pl.BlockSpec((None, 1, sw), lambda b, c: (c, 0, 0))
    st_spec = pl.BlockSpec((None, None, 1, sw), lambda b, c: (b, c, 0, 0))
    u_spec = pl.BlockSpec((seq, uw), lambda b, c: (b, c))
    g, hfr, hfi = pl.pallas_call(
        functools.partial(_s5_kernel, seg_len=seg_len, precise=precise),
        grid=(batch, nch),
        in_specs=[u_spec, w_in, w_in, w_out, w_out, a_spec, a_spec,
                  pl.BlockSpec((1, uw), lambda b, c: (0, c)), st_spec, st_spec],
        out_specs=[u_spec, st_spec, st_spec],
        out_shape=[jax.ShapeDtypeStruct((batch * seq, B_WIDTH), F32),
                   jax.ShapeDtypeStruct((batch, nch, 1, sw), F32),
                   jax.ShapeDtypeStruct((batch, nch, 1, sw), F32)],
        scratch_shapes=[pltpu.VMEM((seq, sw), F32), pltpu.VMEM((seq, sw), F32)],
        compiler_params=_params("parallel", "parallel"),
        name="s5_scan",
    )(u_perm, in_blockdiag(bb_re), in_blockdiag(bb_im), out_blockdiag(c_re), out_blockdiag(c_im),
      row(ab_re), row(ab_im), d_skip.reshape(1, B_WIDTH), st(h0_re), st(h0_im))
    return g, hfr.reshape(batch, B_GROUPS, B_STATE), hfi.reshape(batch, B_GROUPS, B_STATE)


def to_segment_order(a, batch, seq):
    seg_len = seq // S5_SEG
    return a.reshape(batch, S5_SEG, seg_len, -1).swapaxes(1, 2).reshape(batch * seq, -1)


def from_segment_order(a, batch, seq):
    seg_len = seq // S5_SEG
    return a.reshape(batch, seg_len, S5_SEG, -1).swapaxes(1, 2).reshape(batch * seq, -1)


def _merge_kernel(o0, o1, o2, l0, l1, l2, out_ref):
    la, lb, lc = l0[...], l1[...], l2[...]
    m = jnp.maximum(jnp.maximum(la, lb), lc)
    wa, wb, wc = jnp.exp(la - m), jnp.exp(lb - m), jnp.exp(lc - m)
    out = (wa * o0[...] + wb * o1[...] + wc * o2[...]) / (wa + wb + wc)
    out_ref[...] = out.astype(out_ref.dtype)


def merge_groups(outs, lses):
    m, w = outs[0].shape
    tm = min(m, 512)
    spec = pl.BlockSpec((tm, w), lambda i: (i, 0))
    return pl.pallas_call(
        _merge_kernel, grid=(m // tm,), in_specs=[spec] * 6, out_specs=spec,
        out_shape=jax.ShapeDtypeStruct((m, w), BF16), compiler_params=_params("parallel"),
        name="merge_groups",
    )(*outs, *lses)


DSA_TQ = 256
C_TQ = 128
SAMPLE_TK = PAGES_PER_STEP * PAGE_SIZE
NEW_ROWS = 128


def _pad_rows(a, rows):
    return jnp.pad(a, ((0, 0), (0, rows - a.shape[1]), (0, 0)))


def even_mixer(x, h, *, li, w_in, w_rest, w_idx_qb, w_out, w_glu, b_glu, s5c, bias_table, batch, seq, h0_re, h0_im,
               pool_kv=None, pool_kidx=None, page_table=None):
    m = batch * seq
    sample = pool_kv is not None
    ab_re, ab_im, bb_re, bb_im, c_re, c_im, d_skip = s5c

    q = linear(h, w_in, li, n_cols=Q_A_W, col_off=0, tn=512, out_dtypes=(BF16,))
    kv, kvb = linear(h, w_in, li, n_cols=KV_A_W, col_off=OFF_KV, tn=512, out_dtypes=(F32, BF16))
    cq = linear(h, w_in, li, n_cols=R_IDX, col_off=OFF_CQ, tn=512, out_dtypes=(BF16,))
    rest = linear(h, w_rest, 0, tn=256)
    u, ki, wi = rest[:, :B_WIDTH], rest[:, B_WIDTH:B_WIDTH + D_IDX], rest[:, B_WIDTH + D_IDX:B_WIDTH + D_IDX + H_IDX]
    qi = linear(cq, w_idx_qb, li, tn=512, out_dtypes=(BF16,))

    if not sample:
        k_sel = min(TOPK_MAX, seq // 4)
        scores = indexer_prompt(qi, rest, batch=batch, seq=seq, tq=DSA_TQ,
                                ki_blk=B_WIDTH // LANES, wi_blk=B_WIDTH // LANES + 1)
        sel = topk_mask(scores, k_sel, tr=256)
        lib = toeplitz_bias_tiles(bias_table, seq // DSA_TQ, DSA_TQ, 1, seq)
        att = dsa_prompt_attention(q, kvb, lib, sel, batch=batch, seq=seq, tq=DSA_TQ)
    else:
        k_sel = min(TOPK_MAX, (PAST_LEN + seq) // 4)
        ki_new = _pad_rows(ki.reshape(batch, seq, D_IDX), NEW_ROWS)
        scores = indexer_sample(qi.reshape(batch, seq, H_IDX * D_IDX), wi.reshape(batch, seq, H_IDX),
                                ki_new, pool_kidx, li, page_table, tk=SAMPLE_TK)
        s_pad = scores.shape[-1]
        sel = topk_mask(scores.reshape(m, s_pad), k_sel, tr=m).reshape(batch, seq, s_pad)
        bias = sample_bias(bias_table, PAST_LEN, seq, s_pad, lambda d: d >= 0)
        kv_new = _pad_rows(kv.reshape(batch, seq, KV_A_W), NEW_ROWS)
        qf = q.astype(F32).reshape(batch, seq, Q_A_W)
        att = sample_attention(qf, pool_kv, li, kv_new, bias, sel,
                               hq=H_A, hkv=H_KV_A, page_table=page_table, tk=SAMPLE_TK, with_lse=False)
        att = att.reshape(m, Q_A_W).astype(BF16)

    g_perm, hf_re, hf_im = s5_scan(to_segment_order(u, batch, seq), h0_re, h0_im, ab_re, ab_im, bb_re, bb_im,
                                   c_re, c_im, d_skip, batch=batch, seq=seq, precise=sample)
    g = from_segment_order(g_perm, batch, seq)
    ssm = linear(g.astype(BF16), w_glu, li, tn=512, mode="glu", extras=(g, b_glu.reshape(1, B_WIDTH)),
                 out_dtypes=(BF16,))

    x = linear(att, w_out, li, row_off=0, tn=512, mode="resid", extras=(x,))
    x = linear(ssm, w_out, li, row_off=Q_A_W, tn=512, mode="resid", extras=(x,))
    return x, kv, ki, hf_re, hf_im


def odd_mixer(x, h, li, w_in, w_out, bias_table, *, batch, seq, caches=None):
    m = batch * seq
    sample = caches is not None
    q = linear(h, w_in, li, n_cols=C_WIDTH, col_off=0, tn=512, out_dtypes=(BF16,))
    kv, kvb = linear(h, w_in, li, n_cols=2 * C_WIDTH, col_off=C_WIDTH, tn=512, out_dtypes=(F32, BF16))
    kv5 = kv.reshape(batch, seq, 2, H_C, HEAD_DIM)
    outs, lses, rows = [], [], []
    for g, (window, dilation) in enumerate(C_PAIRS):
        cols = bias_table[:, g * H_C_G:(g + 1) * H_C_G]
        kv_g = kv5[:, :, :, g * H_C_G:(g + 1) * H_C_G]
        if not sample:
            lib = toeplitz_bias_tiles(cols, 2, C_TQ, dilation, window // dilation)
            o, l = dilated_prompt_group(q, kvb, lib, batch=batch, seq=seq, g=g, dilation=dilation, tq=C_TQ)
            rows.append(kv_g[:, seq - min(window, seq):])
        else:
            buf = caches[g]
            wb = buf.shape[2]
            tk = min(wb, 512)
            s_pad = wb + tk
            bias = sample_bias(cols, wb, seq, s_pad,
                               lambda d: (d >= 0) & (d <= window) & (d % dilation == 0))
            new_kv = _pad_rows(kv_g.reshape(batch, seq, 2 * C_OUT), NEW_ROWS)
            qg = q[:, g * C_OUT:(g + 1) * C_OUT].astype(F32).reshape(batch, seq, C_OUT)
            o, l = sample_attention(qg, buf, li, new_kv, bias, None,
                                    hq=H_C_G, hkv=H_C_G, tk=tk, with_lse=True)
            o, l = o.reshape(m, C_OUT), l.reshape(m, C_OUT)
            rows.append(kv_g)
        outs.append(o)
        lses.append(l)
    att = merge_groups(outs, lses)
    x = linear(att, w_out, li, tn=512, mode="resid", extras=(x,))
    return x, rows


def kernel(x_prompt, x_sample, cache_a_kv, cache_a_kidx, state_b_re, state_b_im, cache_c0_kv, cache_c1_kv, cache_c2_kv, page_table, p_prompt, p_sample, bias_table, norm_g, final_g, ffn1_w1, ffn1_w3, ffn1_w2, ffn2_w1, ffn2_w3, ffn2_w2, ple_gate, ple_proj, w_in_even, w_idx_qb, w_out_even, s5_lam_re, s5_lam_im, s5_log_step, s5_b_re, s5_b_im, s5_c_re, s5_c_im, s5_d, s5_w_glu, s5_b_glu, w_in_odd, w_out_odd):
    bp, tp, _ = x_prompt.shape
    bs, ts, _ = x_sample.shape
    xp = x_prompt.reshape(bp * tp, D_MODEL)
    xs = x_sample.reshape(bs * ts, D_MODEL)
    pp = p_prompt.reshape(DEPTH, bp * tp, D_PLE)
    ps = p_sample.reshape(DEPTH, bs * ts, D_PLE)
    outs_p = {k: [] for k in ("kv", "ki", "re", "im", "c0", "c1", "c2")}
    outs_s = {k: [] for k in ("kv", "ki", "re", "im", "c0", "c1", "c2")}
    for i in range(DEPTH):
        li = i // 2
        xp = macaron_half(xp, norm_g[i, 0], ffn1_w1, ffn1_w3, ffn1_w2, i)
        xs = macaron_half(xs, norm_g[i, 0], ffn1_w1, ffn1_w3, ffn1_w2, i)
        hp = rmsnorm(xp, norm_g[i, 1], BF16)
        hs = rmsnorm(xs, norm_g[i, 1], BF16)
        if i % 2 == 0:
            w_in = w_in_even[li]
            w_rest = jnp.concatenate([w_in[:, OFF_U:], w_in[:, OFF_KI:OFF_U], w_in[:, OFF_W:OFF_KI],
                                      jnp.zeros((D_MODEL, LANES - H_IDX), w_in.dtype)], axis=1)[None]
            s5c = (*s5_params(s5_lam_re[li], s5_lam_im[li], s5_log_step[li], s5_b_re[li], s5_b_im[li]),
                   s5_c_re[li], s5_c_im[li], s5_d[li])
            even = functools.partial(even_mixer, li=li, w_in=w_in_even, w_rest=w_rest, w_idx_qb=w_idx_qb,
                                     w_out=w_out_even, w_glu=s5_w_glu, b_glu=s5_b_glu[li], s5c=s5c,
                                     bias_table=bias_table)
            zeros = jnp.zeros((bp, B_GROUPS, B_STATE), F32)
            xp, kv, ki, hre, him = even(xp, hp, batch=bp, seq=tp, h0_re=zeros, h0_im=zeros)
            outs_p["kv"].append(kv.reshape(bp, tp, 2, H_KV_A, HEAD_DIM))
            outs_p["ki"].append(ki.reshape(bp, tp, D_IDX))
            outs_p["re"].append(hre)
            outs_p["im"].append(him)
            xs, kv, ki, hre, him = even(xs, hs, batch=bs, seq=ts, h0_re=state_b_re[li], h0_im=state_b_im[li],
                                        pool_kv=cache_a_kv, pool_kidx=cache_a_kidx, page_table=page_table)
            outs_s["kv"].append(kv.reshape(bs, ts, 2, H_KV_A, HEAD_DIM))
            outs_s["ki"].append(ki.reshape(bs, ts, D_IDX))
            outs_s["re"].append(hre)
            outs_s["im"].append(him)
        else:
            xp, rows = odd_mixer(xp, hp, li, w_in_odd, w_out_odd, bias_table, batch=bp, seq=tp)
            for g in range(N_C_GROUPS):
                outs_p["c%d" % g].append(rows[g])
            xs, rows = odd_mixer(xs, hs, li, w_in_odd, w_out_odd, bias_table, batch=bs, seq=ts,
                                 caches=(cache_c0_kv, cache_c1_kv, cache_c2_kv))
            for g in range(N_C_GROUPS):
                outs_s["c%d" % g].append(rows[g])
        xp = macaron_half(xp, norm_g[i, 2], ffn2_w1, ffn2_w3, ffn2_w2, i)
        xs = macaron_half(xs, norm_g[i, 2], ffn2_w1, ffn2_w3, ffn2_w2, i)
        xp = ple_add(xp, pp[i], norm_g[i, 3], ple_gate, ple_proj, i)
        xs = ple_add(xs, ps[i], norm_g[i, 3], ple_gate, ple_proj, i)
    y_prompt = rmsnorm(xp, final_g, F32).reshape(bp, tp, D_MODEL)
    y_sample = rmsnorm(xs, final_g, F32).reshape(bs, ts, D_MODEL)
    order = ("kv", "ki", "re", "im", "c0", "c1", "c2")
    return (y_prompt, y_sample, *[jnp.stack(outs_p[k]) for k in order], *[jnp.stack(outs_s[k]) for k in order])
```

```python
import functools
import math

import jax
import jax.numpy as jnp
from jax import lax
from jax.experimental import pallas as pl
from jax.experimental.pallas import tpu as pltpu

D_MODEL = 4096
DEPTH = 2
PAST_LEN = 16384
PAGE_SIZE = 128
HEAD_DIM = 128
EPS = 1e-6
H_A = 24
H_KV_A = 8
H_IDX = 32
D_IDX = 128
R_IDX = 512
TOPK_MAX = 256
B_WIDTH = 1024
B_GROUP = 16
B_GROUPS = 64
B_STATE = 64
C_PAIRS = ((128, 1), (512, 4), (2048, 16))
N_C_GROUPS = 3
H_C_G = 8
H_C = 24
C_WIDTH = H_C * HEAD_DIM
C_OUT = H_C_G * HEAD_DIM
NUM_BUCKETS = 32
MAX_DISTANCE = 2048
D_FF = 11008
D_PLE = 256
Q_A_W = H_A * HEAD_DIM
KV_A_W = 2 * H_KV_A * HEAD_DIM
OFF_KV = Q_A_W
OFF_CQ = OFF_KV + KV_A_W
OFF_W = OFF_CQ + R_IDX
OFF_KI = OFF_W + H_IDX
OFF_U = OFF_KI + D_IDX
IN_EVEN = OFF_U + B_WIDTH

LANES = 128
SUBLANES = 8
VMEM_LIMIT = 56 * 1024 * 1024

LINEAR_VMEM_BUDGET = 44 * 1024 * 1024
ROW_TILES = (1376, 1024, 512, 256, 128, 64, 16)

S5_SEG = SUBLANES
S5_CHUNK_GROUPS = 8
PAGES_PER_STEP = 8
NEG_INIT = -1e30

F32 = jnp.float32
BF16 = jnp.bfloat16
NT_DIMS = (((1,), (1,)), ((), ()))


def _params(*sem):
    return pltpu.CompilerParams(dimension_semantics=sem, vmem_limit_bytes=VMEM_LIMIT)


def _rms_kernel(x_ref, g_ref, o_ref):
    x = x_ref[...]
    ms = jnp.mean(x * x, axis=-1, keepdims=True)
    o_ref[...] = (x * lax.rsqrt(ms + EPS) * g_ref[...]).astype(o_ref.dtype)


def _row_tile(m, candidates):
    for c in candidates:
        if m % c == 0:
            return c
    return m


def rmsnorm(x, g, out_dtype, *, row_off=0, rows=None):
    d = x.shape[1]
    rows = x.shape[0] if rows is None else rows
    tm = _row_tile(math.gcd(rows, row_off) if row_off else rows, (688, 256, 64, 16))
    blk0 = row_off // tm
    return pl.pallas_call(
        _rms_kernel,
        grid=(rows // tm,),
        in_specs=[pl.BlockSpec((tm, d), lambda i: (i + blk0, 0)), pl.BlockSpec((1, d), lambda i: (0, 0))],
        out_specs=pl.BlockSpec((tm, d), lambda i: (i, 0)),
        out_shape=jax.ShapeDtypeStruct((rows, d), out_dtype),
        compiler_params=_params("parallel"),
        name="rmsnorm",
    )(x, g.reshape(1, d))


def _linear_kernel(*refs, mode, n_out, scale):
    a_ref, w_ref = refs[0], refs[1]
    extras = refs[2:len(refs) - n_out]
    outs = refs[len(refs) - n_out:]
    acc = jnp.dot(a_ref[...].astype(BF16), w_ref[...].astype(BF16), preferred_element_type=F32)
    if mode == "none":
        res = acc
    elif mode == "resid":
        res = extras[0][...] + scale * acc
    elif mode == "glu":
        g = extras[0][...]
        res = g * jax.nn.sigmoid(acc + extras[1][...])
    elif mode == "ple":
        x_ref, p_ref, wp_ref = extras
        proj = jnp.dot(p_ref[...].astype(BF16), wp_ref[...].astype(BF16), preferred_element_type=F32)
        res = x_ref[...] + jax.nn.sigmoid(acc) * proj
    else:
        raise ValueError(mode)
    for o in outs:
        o[...] = res.astype(o.dtype)


def linear(a, w, li, *, n_cols=None, col_off=0, row_off=0, tn, mode="none", extras=(),
           out_dtypes=(F32,), scale=1.0):
    m, k = a.shape
    n_cols = w.shape[2] if n_cols is None else n_cols
    tm = _row_tile(m, ROW_TILES)
    n_tiles = len(out_dtypes) + (0 if mode == "none" else 1)
    while (4 * tm * k + 10 * k * tn + 4 * tm * tn * (1 + 2 * n_tiles) > LINEAR_VMEM_BUDGET and tn > LANES
           and col_off % (tn // 2) == 0):
        tn //= 2
    assert m % tm == 0 and n_cols % tn == 0 and col_off % tn == 0 and row_off % k == 0
    jo, ro = col_off // tn, row_off // k
    tile = pl.BlockSpec((tm, tn), lambda i, j: (i, j))
    in_specs = [pl.BlockSpec((tm, k), lambda i, j: (i, 0)),
                pl.BlockSpec((None, k, tn), lambda i, j: (li, ro, j + jo))]
    if mode == "resid":
        in_specs += [tile]
    elif mode == "glu":
        in_specs += [tile, pl.BlockSpec((1, tn), lambda i, j: (0, j))]
    elif mode == "ple":
        kp = extras[1].shape[1]
        in_specs += [tile, pl.BlockSpec((tm, kp), lambda i, j: (i, 0)),
                     pl.BlockSpec((None, kp, tn), lambda i, j: (li, 0, j))]
    res = pl.pallas_call(
        functools.partial(_linear_kernel, mode=mode, n_out=len(out_dtypes), scale=scale),
        grid=(m // tm, n_cols // tn),
        in_specs=in_specs,
        out_specs=[tile] * len(out_dtypes),
        out_shape=[jax.ShapeDtypeStruct((m, n_cols), dt) for dt in out_dtypes],
        compiler_params=_params("parallel", "arbitrary"),
        name="linear_" + mode,
    )(a, w, *extras)
    return res if len(out_dtypes) > 1 else res[0]


FFN_TN = 256
DOWN_TK = 512


def _swiglu_kernel(a_ref, w1_ref, w3_ref, o_ref, *, n_real):
    j = pl.program_id(1)

    @pl.when(j < n_real)
    def _():
        a = a_ref[...]
        g = jnp.dot(a, w1_ref[...].astype(BF16), preferred_element_type=F32)
        u = jnp.dot(a, w3_ref[...].astype(BF16), preferred_element_type=F32)
        o_ref[...] = (jax.nn.silu(g) * u).astype(o_ref.dtype)

    @pl.when(j >= n_real)
    def _():
        o_ref[...] = jnp.zeros(o_ref.shape, o_ref.dtype)


def swiglu_up(h, w1, w3, li, *, n_pad):
    m, k = h.shape
    n = w1.shape[2]
    tn = FFN_TN
    tm = _row_tile(m, ROW_TILES)
    n_real = n // tn
    w_spec = pl.BlockSpec((None, k, tn), lambda i, j: (li, 0, jnp.minimum(j, n_real - 1)))
    return pl.pallas_call(
        functools.partial(_swiglu_kernel, n_real=n_real),
        grid=(m // tm, n_pad // tn),
        in_specs=[pl.BlockSpec((tm, k), lambda i, j: (i, 0)), w_spec, w_spec],
        out_specs=pl.BlockSpec((tm, tn), lambda i, j: (i, j)),
        out_shape=jax.ShapeDtypeStruct((m, n_pad), BF16),
        compiler_params=_params("parallel", "arbitrary"),
        name="swiglu_up",
    )(h, w1, w3)


def _down_kernel(a_ref, w_ref, x_ref, o_ref, *, scale, tail_rows):
    kk = pl.program_id(2)
    last = pl.num_programs(2) - 1

    @pl.when(kk == 0)
    def _():
        o_ref[...] = x_ref[...]

    @pl.when(kk < last)
    def _():
        o_ref[...] += scale * jnp.dot(a_ref[...], w_ref[...].astype(BF16), preferred_element_type=F32)

    @pl.when(kk == last)
    def _():
        w = w_ref[...]
        row = lax.broadcasted_iota(jnp.int32, w.shape, 0)
        w = jnp.where(row < tail_rows, w, 0.0).astype(BF16)
        o_ref[...] += scale * jnp.dot(a_ref[...], w, preferred_element_type=F32)


def down_resid(act, w, li, x, *, scale):
    m, k_pad = act.shape
    k, n = w.shape[1:]
    tm = _row_tile(m, (2048,) + ROW_TILES)
    tn = min(n, 1024)
    tk = DOWN_TK
    assert k_pad % tk == 0 and k_pad - k < tk
    return pl.pallas_call(
        functools.partial(_down_kernel, scale=scale, tail_rows=tk - (k_pad - k)),
        grid=(m // tm, n // tn, k_pad // tk),
        in_specs=[pl.BlockSpec((tm, tk), lambda i, j, kk: (i, kk)),
                  pl.BlockSpec((None, tk, tn), lambda i, j, kk: (li, kk, j)),
                  pl.BlockSpec((tm, tn), lambda i, j, kk: (i, j))],
        out_specs=pl.BlockSpec((tm, tn), lambda i, j, kk: (i, j)),
        out_shape=jax.ShapeDtypeStruct((m, n), F32),
        compiler_params=_params("parallel", "parallel", "arbitrary"),
        name="down_resid",
    )(act, w, x)


def macaron_half(x, g, w1, w3, w2, li):
    h = rmsnorm(x, g, BF16)
    n_pad = -(-w1.shape[2] // DOWN_TK) * DOWN_TK
    act = swiglu_up(h, w1, w3, li, n_pad=n_pad)
    return down_resid(act, w2, li, x, scale=0.5)


def ple_add(x, p, g, w_gate, w_proj, li):
    h = rmsnorm(x, g, BF16)
    return linear(h, w_gate, li, tn=256, mode="ple", extras=(x, p.astype(BF16), w_proj))


def t5_bucket(dist):
    max_exact = NUM_BUCKETS // 2
    d = jnp.maximum(dist, 0)
    ratio = jnp.log(jnp.maximum(d, 1).astype(F32) / max_exact) / math.log(MAX_DISTANCE / max_exact)
    large = jnp.minimum(max_exact + (ratio * (NUM_BUCKETS - max_exact)).astype(jnp.int32), NUM_BUCKETS - 1)
    return jnp.where(d < max_exact, d, large)


def bias_of_dist(bias_cols, dist, valid):
    onehot = t5_bucket(dist)[..., None] == jnp.arange(NUM_BUCKETS)
    b = jnp.sum(jnp.where(onehot[..., None], bias_cols.astype(F32), 0.0), axis=-2)
    b = jnp.where(valid[..., None], b, -jnp.inf)
    return jnp.moveaxis(b, -1, 0)


def sample_bias(bias_cols, base, t, s_pad, valid_fn):
    n = jnp.arange(s_pad + t - 1)
    dist = base + (t - 1) - n
    table = bias_of_dist(bias_cols, dist, valid_fn(dist))
    return jnp.stack([table[:, t - 1 - i:t - 1 - i + s_pad] for i in range(t)], axis=1)


def _toeplitz_kernel(w_ref, o_ref):
    nd, tq, _ = o_ref.shape
    for d in range(nd):
        x = jnp.broadcast_to(w_ref[d], (tq, 2 * tq))
        r = pltpu.roll(x, 0, 1, stride=1, stride_axis=0)
        o_ref[d] = r[:, tq:]


def toeplitz_bias_tiles(bias_cols, n_delta, tq, step, max_steps):
    h = bias_cols.shape[1]
    steps = jnp.arange(n_delta)[:, None] * tq + tq - jnp.arange(2 * tq)[None, :]
    rows = bias_of_dist(bias_cols, steps * step, (steps >= 0) & (steps <= max_steps))
    return pl.pallas_call(
        _toeplitz_kernel,
        grid=(h,),
        in_specs=[pl.BlockSpec((None, n_delta, 1, 2 * tq), lambda i: (i, 0, 0, 0))],
        out_specs=pl.BlockSpec((None, n_delta, tq, tq), lambda i: (i, 0, 0, 0)),
        out_shape=jax.ShapeDtypeStruct((h, n_delta, tq, tq), F32),
        compiler_params=_params("parallel"),
        name="toeplitz_bias_tiles",
    )(rows.reshape(h, n_delta, 1, 2 * tq))


def _flash_init(m_ref, l_ref, acc_ref):
    m_ref[...] = jnp.full(m_ref.shape, NEG_INIT, F32)
    l_ref[...] = jnp.zeros(l_ref.shape, F32)
    acc_ref[...] = jnp.zeros(acc_ref.shape, F32)


def _flash_step(q_ref, k_at, v_at, bias_at, sel, m_ref, l_ref, acc_ref, *, hq, hkv):
    rep = hq // hkv
    tq = q_ref.shape[0]
    scale = HEAD_DIM ** -0.5
    stack = lambda parts: parts[0] if rep == 1 else jnp.concatenate(parts, axis=0)
    sel_rows = None if sel is None else stack([sel] * rep)
    for g in range(hkv):
        heads = range(g * rep, (g + 1) * rep)
        cols = [slice(h * HEAD_DIM, (h + 1) * HEAD_DIM) for h in heads]
        qs = stack([q_ref[:, c] for c in cols]).astype(BF16)
        s = lax.dot_general(qs, k_at(g), NT_DIMS, preferred_element_type=F32) * scale
        s = s + stack([bias_at(h) for h in heads])
        if sel_rows is not None:
            s = jnp.where(sel_rows, s, -jnp.inf)
        gsl = slice(g * rep, (g + 1) * rep)
        m_old = m_ref[gsl].reshape(rep * tq, LANES)
        m_new = jnp.maximum(m_old, jnp.max(s, axis=1, keepdims=True))
        alpha = jnp.exp(m_old - m_new)
        p = jnp.exp(s - m_new[:, :1])
        l_new = alpha * l_ref[gsl].reshape(rep * tq, LANES) + jnp.sum(p, axis=1, keepdims=True)
        pv = jnp.dot(p.astype(BF16), v_at(g), preferred_element_type=F32)
        for r, c in enumerate(cols):
            rows = slice(r * tq, (r + 1) * tq)
            acc_ref[:, c] = alpha[rows] * acc_ref[:, c] + pv[rows]
        l_ref[gsl] = l_new.reshape(rep, tq, LANES)
        m_ref[gsl] = m_new.reshape(rep, tq, LANES)


def _flash_finish(o_ref, lse_ref, m_ref, l_ref, acc_ref, *, hq):
    for h in range(hq):
        hs = slice(h * HEAD_DIM, (h + 1) * HEAD_DIM)
        l = l_ref[h]
        o_ref[:, hs] = (acc_ref[:, hs] / l).astype(o_ref.dtype)
        if lse_ref is not None:
            lse_ref[:, hs] = m_ref[h] + jnp.log(l)


def _flash_scratch(tq, hq):
    return [pltpu.VMEM((hq, tq, LANES), F32), pltpu.VMEM((hq, tq, LANES), F32),
            pltpu.VMEM((tq, hq * HEAD_DIM), F32)]


def _prompt_attn_kernel(*refs, hq, hkv, has_sel, with_lse, first_kb):
    q_ref, k_ref, v_ref, bias_ref = refs[:4]
    pos = 4
    sel_ref = None
    if has_sel:
        sel_ref = refs[pos]
        pos += 1
    o_ref = refs[pos]
    pos += 1
    lse_ref = None
    if with_lse:
        lse_ref = refs[pos]
        pos += 1
    m_ref, l_ref, acc_ref = refs[pos:pos + 3]
    qb = pl.program_id(2)
    kk = pl.program_id(3)
    nk = pl.num_programs(3)

    @pl.when(kk == 0)
    def _():
        _flash_init(m_ref, l_ref, acc_ref)

    @pl.when(first_kb(qb, kk) >= 0)
    def _():
        sel = None if sel_ref is None else sel_ref[...] > 0.0
        head = lambda ref: (lambda g: ref[:, g * HEAD_DIM:(g + 1) * HEAD_DIM].astype(BF16))
        _flash_step(q_ref, head(k_ref), head(v_ref), lambda h: bias_ref[h], sel, m_ref, l_ref, acc_ref,
                    hq=hq, hkv=hkv)

    @pl.when(kk == nk - 1)
    def _():
        _flash_finish(o_ref, lse_ref, m_ref, l_ref, acc_ref, hq=hq)


def dsa_prompt_attention(q, kvb, bias_lib, sel, *, batch, seq, tq):
    nq = seq // tq
    grid = (batch, 1, nq, nq)

    def kb_of(qb, kk):
        return jnp.minimum(kk, qb)

    in_specs = [
        pl.BlockSpec((tq, Q_A_W), lambda b, r, qb, kk: (b * nq + qb, 0)),
        pl.BlockSpec((tq, H_KV_A * HEAD_DIM), lambda b, r, qb, kk: (b * nq + kb_of(qb, kk), 0)),
        pl.BlockSpec((tq, H_KV_A * HEAD_DIM), lambda b, r, qb, kk: (b * nq + kb_of(qb, kk), 1)),
        pl.BlockSpec((H_A, None, tq, tq), lambda b, r, qb, kk: (0, qb - kb_of(qb, kk), 0, 0)),
        pl.BlockSpec((tq, tq), lambda b, r, qb, kk: (b * nq + qb, kb_of(qb, kk))),
    ]
    return pl.pallas_call(
        functools.partial(_prompt_attn_kernel, hq=H_A, hkv=H_KV_A, has_sel=True, with_lse=False,
                          first_kb=lambda qb, kk: qb - kk),
        grid=grid,
        in_specs=in_specs,
        out_specs=pl.BlockSpec((tq, Q_A_W), lambda b, r, qb, kk: (b * nq + qb, 0)),
        out_shape=jax.ShapeDtypeStruct((batch * seq, Q_A_W), BF16),
        scratch_shapes=_flash_scratch(tq, H_A),
        compiler_params=_params("parallel", "arbitrary", "arbitrary", "arbitrary"),
        name="dsa_prompt_attention",
    )(q, kvb, kvb, bias_lib, sel)


def _dilated_kernel(q_ref, kp_ref, kc_ref, vp_ref, vc_ref, bias_ref, o_ref, lse_ref):
    tq = q_ref.shape[0]
    scale = HEAD_DIM ** -0.5
    col = lax.broadcasted_iota(jnp.int32, (tq, 2 * tq), 1)
    usable = (col >= tq) | (pl.program_id(2) > 0)
    for h in range(H_C_G):
        hs = slice(h * HEAD_DIM, (h + 1) * HEAD_DIM)
        k = jnp.concatenate([kp_ref[:, hs], kc_ref[:, hs]], axis=0)
        v = jnp.concatenate([vp_ref[:, hs], vc_ref[:, hs]], axis=0)
        s = lax.dot_general(q_ref[:, hs], k, NT_DIMS, preferred_element_type=F32) * scale + bias_ref[h]
        s = jnp.where(usable, s, -jnp.inf)
        m = jnp.max(s, axis=1, keepdims=True)
        p = jnp.exp(s - m)
        l = jnp.sum(p, axis=1, keepdims=True)
        o_ref[:, hs] = jnp.dot(p.astype(BF16), v, preferred_element_type=F32) / l
        lse_ref[:, hs] = jnp.broadcast_to(m + jnp.log(l), (tq, HEAD_DIM))


def dilated_prompt_group(q, kvb, bias_lib, *, batch, seq, g, dilation, tq):
    tl = seq // dilation
    nq = tl // tq
    qv = q.reshape(batch, tl, dilation * C_WIDTH)
    kvv = kvb.reshape(batch, tl, dilation * 2 * C_WIDTH)
    qcols, kvcols = C_WIDTH // C_OUT, 2 * C_WIDTH // C_OUT
    bias = jnp.concatenate([bias_lib[:, 1], bias_lib[:, 0]], axis=-1)
    prev = lambda qb: jnp.maximum(qb - 1, 0)
    blk = lambda f: pl.BlockSpec((None, tq, C_OUT), f)
    in_specs = [
        blk(lambda b, r, qb: (b, qb, r * qcols + g)),
        blk(lambda b, r, qb: (b, prev(qb), r * kvcols + g)),
        blk(lambda b, r, qb: (b, qb, r * kvcols + g)),
        blk(lambda b, r, qb: (b, prev(qb), r * kvcols + N_C_GROUPS + g)),
        blk(lambda b, r, qb: (b, qb, r * kvcols + N_C_GROUPS + g)),
        pl.BlockSpec((H_C_G, tq, 2 * tq), lambda b, r, qb: (0, 0, 0)),
    ]
    out_spec = blk(lambda b, r, qb: (b, qb, r))
    o, lse = pl.pallas_call(
        _dilated_kernel,
        grid=(batch, dilation, nq),
        in_specs=in_specs,
        out_specs=[out_spec, out_spec],
        out_shape=[jax.ShapeDtypeStruct((batch, tl, dilation * C_OUT), F32)] * 2,
        compiler_params=_params("parallel", "parallel", "arbitrary"),
        name="dilated_prompt_group",
    )(qv, kvv, kvv, kvv, kvv, bias)
    return o.reshape(batch * seq, C_OUT), lse.reshape(batch * seq, C_OUT)


def _sample_attn_kernel(*refs, hq, hkv, n_src, has_sel, with_lse, has_table):
    if has_table:
        refs = refs[1:]
    q_ref = refs[0]
    src_refs = refs[1:1 + n_src]
    new_ref, bias_ref = refs[1 + n_src], refs[2 + n_src]
    pos = 3 + n_src
    sel_ref = None
    if has_sel:
        sel_ref = refs[pos]
        pos += 1
    o_ref = refs[pos]
    pos += 1
    lse_ref = None
    if with_lse:
        lse_ref = refs[pos]
        pos += 1
    m_ref, l_ref, acc_ref = refs[pos:pos + 3]
    kb = pl.program_id(1)
    nk = pl.num_programs(1)
    kw = hkv * HEAD_DIM
    new_rows = new_ref.shape[0]

    @pl.when(kb == 0)
    def _():
        _flash_init(m_ref, l_ref, acc_ref)

    def cached(which):
        def at(g):
            parts = [r[:, which, g, :] for r in src_refs]
            return (parts[0] if n_src == 1 else jnp.concatenate(parts, axis=0)).astype(BF16)
        return at

    def fresh(which):
        return lambda g: new_ref[:, which * kw + g * HEAD_DIM:which * kw + (g + 1) * HEAD_DIM].astype(BF16)

    @pl.when(kb < nk - 1)
    def _():
        sel = None if sel_ref is None else sel_ref[...] > 0.0
        _flash_step(q_ref, cached(0), cached(1), lambda h: bias_ref[h], sel,
                    m_ref, l_ref, acc_ref, hq=hq, hkv=hkv)

    @pl.when(kb == nk - 1)
    def _():
        sel = None if sel_ref is None else sel_ref[:, :new_rows] > 0.0
        _flash_step(q_ref, fresh(0), fresh(1), lambda h: bias_ref[h, :, :new_rows],
                    sel, m_ref, l_ref, acc_ref, hq=hq, hkv=hkv)
        _flash_finish(o_ref, lse_ref, m_ref, l_ref, acc_ref, hq=hq)


def sample_attention(q, src, li, new_kv, bias, sel, *, hq, hkv, page_table=None, tk, with_lse):
    bsz, t, _ = q.shape
    kvw = 2 * hkv * HEAD_DIM
    new_rows = new_kv.shape[1]
    paged = page_table is not None
    if paged:
        n_src = tk // PAGE_SIZE
        nk = page_table.shape[1] // n_src
        src_specs = [pl.BlockSpec((None, None, PAGE_SIZE, 2, hkv, HEAD_DIM),
                                  functools.partial(lambda p, b, kb, pt: (li, pt[b, jnp.minimum(kb, nk - 1) * n_src + p], 0, 0, 0, 0), p))
                     for p in range(n_src)]
        ix = lambda f: (lambda b, kb, pt: f(b, kb))
    else:
        n_src = 1
        nk = src.shape[2] // tk
        src_specs = [pl.BlockSpec((None, None, tk, 2, hkv, HEAD_DIM),
                                  lambda b, kb: (li, b, jnp.minimum(kb, nk - 1), 0, 0, 0))]
        ix = lambda f: f
    in_specs = ([pl.BlockSpec((None, t, hq * HEAD_DIM), ix(lambda b, kb: (b, 0, 0)))] + src_specs +
                [pl.BlockSpec((None, new_rows, kvw), ix(lambda b, kb: (b, 0, 0))),
                 pl.BlockSpec((hq, t, tk), ix(lambda b, kb: (0, 0, kb)))])
    args = [q] + [src] * n_src + [new_kv, bias]
    if sel is not None:
        in_specs.append(pl.BlockSpec((None, t, tk), ix(lambda b, kb: (b, 0, kb))))
        args.append(sel)
    out_spec = pl.BlockSpec((None, t, hq * HEAD_DIM), ix(lambda b, kb: (b, 0, 0)))
    n_out = 2 if with_lse else 1
    kern = functools.partial(_sample_attn_kernel, hq=hq, hkv=hkv, n_src=n_src, has_sel=sel is not None,
                             with_lse=with_lse, has_table=paged)
    out_shape = [jax.ShapeDtypeStruct((bsz, t, hq * HEAD_DIM), F32)] * n_out
    scratch = _flash_scratch(t, hq)
    cp = _params("parallel", "arbitrary")
    if paged:
        gs = pltpu.PrefetchScalarGridSpec(num_scalar_prefetch=1, grid=(bsz, nk + 1), in_specs=in_specs,
                                          out_specs=[out_spec] * n_out, scratch_shapes=scratch)
        res = pl.pallas_call(kern, grid_spec=gs, out_shape=out_shape, compiler_params=cp,
                             name="sample_attention_paged")(page_table, *args)
    else:
        res = pl.pallas_call(kern, grid=(bsz, nk + 1), in_specs=in_specs, out_specs=[out_spec] * n_out,
                             out_shape=out_shape, scratch_shapes=scratch, compiler_params=cp,
                             name="sample_attention")(*args)
    return res if with_lse else res[0]


IDX_SCALE = (H_IDX ** -0.5) * (D_IDX ** -0.5)


def _idx_prompt_kernel(qi_ref, ki_ref, wi_ref, o_ref, *, tq):
    qb, kb = pl.program_id(1), pl.program_id(2)

    @pl.when(kb > qb)
    def _():
        o_ref[...] = jnp.full(o_ref.shape, -jnp.inf, F32)

    @pl.when(kb <= qb)
    def _():
        kt = ki_ref[...].astype(BF16)
        w = wi_ref[...] * IDX_SCALE
        acc = jnp.zeros((tq, tq), F32)
        for h in range(H_IDX):
            s = lax.dot_general(qi_ref[:, h * D_IDX:(h + 1) * D_IDX], kt, NT_DIMS, preferred_element_type=F32)
            acc = acc + jnp.maximum(s, 0.0) * w[:, h:h + 1]
        row = lax.broadcasted_iota(jnp.int32, (tq, tq), 0) + qb * tq
        col = lax.broadcasted_iota(jnp.int32, (tq, tq), 1) + kb * tq
        o_ref[...] = jnp.where(col <= row, acc, -jnp.inf)


def indexer_prompt(qi, rest, *, batch, seq, tq, ki_blk, wi_blk):
    nq = seq // tq
    return pl.pallas_call(
        functools.partial(_idx_prompt_kernel, tq=tq),
        grid=(batch, nq, nq),
        in_specs=[pl.BlockSpec((tq, H_IDX * D_IDX), lambda b, qb, kb: (b * nq + qb, 0)),
                  pl.BlockSpec((tq, D_IDX), lambda b, qb, kb: (b * nq + jnp.minimum(kb, qb), ki_blk)),
                  pl.BlockSpec((tq, LANES), lambda b, qb, kb: (b * nq + qb, wi_blk))],
        out_specs=pl.BlockSpec((tq, tq), lambda b, qb, kb: (b * nq + qb, kb)),
        out_shape=jax.ShapeDtypeStruct((batch * seq, seq), F32),
        compiler_params=_params("parallel", "parallel", "arbitrary"),
        name="indexer_prompt",
    )(qi, rest, rest)


def _split_dot_nt(e, x):
    hi = x.astype(BF16)
    lo = (x - hi.astype(F32)).astype(BF16)
    return (lax.dot_general(e, hi, NT_DIMS, preferred_element_type=F32) +
            lax.dot_general(e, lo, NT_DIMS, preferred_element_type=F32))


def _idx_sample_kernel(pt_ref, qit_ref, w_ref, e_ref, *refs, n_src, t):
    del pt_ref
    src_refs = refs[:n_src]
    new_ref, o_ref = refs[n_src], refs[n_src + 1]
    kb = pl.program_id(1)
    nk = pl.num_programs(1)
    tk = o_ref.shape[1]
    new_rows = new_ref.shape[0]

    def scores(keys):
        st = jnp.dot(keys.astype(BF16), qit_ref[...], preferred_element_type=F32)
        x = jnp.maximum(st, 0.0) * w_ref[...]
        return _split_dot_nt(e_ref[...], x)[:t]

    @pl.when(kb < nk - 1)
    def _():
        o_ref[...] = scores(jnp.concatenate([r[...] for r in src_refs], axis=0))

    @pl.when(kb == nk - 1)
    def _():
        sc = scores(new_ref[...])
        row = lax.broadcasted_iota(jnp.int32, (t, new_rows), 0)
        col = lax.broadcasted_iota(jnp.int32, (t, new_rows), 1)
        o_ref[...] = jnp.full(o_ref.shape, -jnp.inf, F32)
        o_ref[:, :new_rows] = jnp.where(col <= row, sc, -jnp.inf)


def indexer_sample(qi, wi, ki_new, pool_kidx, li, page_table, *, tk):
    bsz, t, _ = qi.shape
    n_src = tk // PAGE_SIZE
    nk = page_table.shape[1] // n_src
    qit = qi.reshape(bsz, t * H_IDX, D_IDX).transpose(0, 2, 1)
    w = (wi * IDX_SCALE).reshape(bsz, 1, t * H_IDX)
    e_rows = 2 * SUBLANES
    e = jnp.repeat(jnp.eye(e_rows, t, dtype=BF16), H_IDX, axis=1)
    src_specs = [pl.BlockSpec((None, None, PAGE_SIZE, D_IDX),
                              functools.partial(lambda p, b, kb, pt: (li, pt[b, jnp.minimum(kb, nk - 1) * n_src + p], 0, 0), p))
                 for p in range(n_src)]
    gs = pltpu.PrefetchScalarGridSpec(
        num_scalar_prefetch=1, grid=(bsz, nk + 1),
        in_specs=[pl.BlockSpec((None, D_IDX, t * H_IDX), lambda b, kb, pt: (b, 0, 0)),
                  pl.BlockSpec((None, 1, t * H_IDX), lambda b, kb, pt: (b, 0, 0)),
                  pl.BlockSpec((e_rows, t * H_IDX), lambda b, kb, pt: (0, 0))] + src_specs +
                 [pl.BlockSpec((None, ki_new.shape[1], D_IDX), lambda b, kb, pt: (b, 0, 0))],
        out_specs=pl.BlockSpec((None, t, tk), lambda b, kb, pt: (b, 0, kb)))
    return pl.pallas_call(
        functools.partial(_idx_sample_kernel, n_src=n_src, t=t),
        grid_spec=gs,
        out_shape=jax.ShapeDtypeStruct((bsz, t, (nk + 1) * tk), F32),
        compiler_params=_params("parallel", "arbitrary"),
        name="indexer_sample",
    )(page_table, qit, w, e, *([pool_kidx] * n_src), ki_new)


TOPK_CHUNK = 512
INT_MIN = -2 ** 31


def _topk_kernel(s_ref, u_ref, o_ref, *, k):
    x = s_ref[...]
    rows, n = x.shape
    x = jnp.where(x == 0.0, 0.0, x)
    bits = lax.bitcast_convert_type(x, jnp.int32)
    key = bits ^ ((bits >> 31) & jnp.int32(0x7FFFFFFF))
    kf = jnp.float32(k)

    def body(it, tau_u):
        cand_u = tau_u | jnp.left_shift(jnp.int32(1), 31 - it)
        cand = cand_u ^ jnp.int32(INT_MIN)
        cnt = jnp.sum(jnp.where(key >= cand, 1.0, 0.0), axis=1, keepdims=True)
        return jnp.where(cnt >= kf, cand_u, tau_u)

    tau_u = lax.fori_loop(0, 32, body, jnp.zeros((rows, 1), jnp.int32))
    tau = tau_u ^ jnp.int32(INT_MIN)
    gt = key > tau
    eq = key == tau
    need = kf - jnp.sum(jnp.where(gt, 1.0, 0.0), axis=1, keepdims=True)
    run = jnp.zeros((rows, 1), F32)
    for c in range(n // TOPK_CHUNK):
        cs = slice(c * TOPK_CHUNK, (c + 1) * TOPK_CHUNK)
        e = jnp.where(eq[:, cs], 1.0, 0.0)
        before = jnp.dot(e.astype(BF16), u_ref[...], preferred_element_type=F32) + run
        o_ref[:, cs] = jnp.where(gt[:, cs] | (eq[:, cs] & (before < need)), 1.0, 0.0)
        run = run + jnp.sum(e, axis=1, keepdims=True)


def topk_mask(scores, k, *, tr):
    r, s = scores.shape
    assert s % TOPK_CHUNK == 0 and r % tr == 0
    ii = jnp.arange(TOPK_CHUNK)
    upper = (ii[:, None] < ii[None, :]).astype(BF16)
    return pl.pallas_call(
        functools.partial(_topk_kernel, k=k),
        grid=(r // tr,),
        in_specs=[pl.BlockSpec((tr, s), lambda i: (i, 0)),
                  pl.BlockSpec((TOPK_CHUNK, TOPK_CHUNK), lambda i: (0, 0))],
        out_specs=pl.BlockSpec((tr, s), lambda i: (i, 0)),
        out_shape=jax.ShapeDtypeStruct((r, s), F32),
        compiler_params=_params("parallel"),
        name="topk_mask",
    )(scores, upper)


def _s5_param_kernel(lr_ref, li_ref, ls_ref, br_ref, bi_ref, abr_ref, abi_ref, bbr_ref, bbi_ref):
    lr, li = lr_ref[...], li_ref[...]
    step = jnp.exp(ls_ref[...])
    mag = jnp.exp(lr * step)
    ab_re, ab_im = mag * jnp.cos(li * step), mag * jnp.sin(li * step)
    den = lr * lr + li * li
    nr, ni = ab_re - 1.0, ab_im
    f_re, f_im = (nr * lr + ni * li) / den, (ni * lr - nr * li) / den
    br, bi = br_ref[...], bi_ref[...]
    abr_ref[...] = ab_re
    abi_ref[...] = ab_im
    bbr_ref[...] = f_re * br - f_im * bi
    bbi_ref[...] = f_re * bi + f_im * br


def s5_params(lam_re, lam_im, log_step, b_re, b_im):
    w = B_STATE * B_GROUP
    rep = lambda a: jnp.repeat(a, B_GROUP, axis=1)
    ls = jnp.broadcast_to(log_step[:, None], (B_GROUPS, w))
    shp = jax.ShapeDtypeStruct((B_GROUPS, w), F32)
    abr, abi, bbr, bbi = pl.pallas_call(_s5_param_kernel, out_shape=[shp] * 4, name="s5_params")(
        rep(lam_re), rep(lam_im), ls, b_re.reshape(B_GROUPS, w), b_im.reshape(B_GROUPS, w))
    return (abr[:, ::B_GROUP], abi[:, ::B_GROUP],
            bbr.reshape(B_GROUPS, B_STATE, B_GROUP), bbi.reshape(B_GROUPS, B_STATE, B_GROUP))


def _cmul(ar, ai, br, bi):
    return ar * br - ai * bi, ar * bi + ai * br


def _s5_kernel(u_ref, bbr_ref, bbi_ref, ccr_ref, cci_ref, ar_ref, ai_ref, d_ref, h0r_ref, h0i_ref,
               g_ref, hfr_ref, hfi_ref, xr_ref, xi_ref, *, seg_len, precise):
    u = u_ref[...]
    if precise:
        mm = lambda a, b: jnp.dot(a, b, preferred_element_type=F32, precision=lax.Precision.HIGHEST)
    else:
        mm = lambda a, b: jnp.dot(a.astype(BF16), b.astype(BF16), preferred_element_type=F32)
    xr_ref[...] = mm(u, bbr_ref[...])
    xi_ref[...] = mm(u, bbi_ref[...])
    sw = xr_ref.shape[1]
    ar = jnp.broadcast_to(ar_ref[...], (S5_SEG, sw))
    ai = jnp.broadcast_to(ai_ref[...], (S5_SEG, sw))

    def rows(j):
        return pl.ds(pl.multiple_of(j * S5_SEG, S5_SEG), S5_SEG)

    def local_scan(j, carry):
        hr, hi = _cmul(ar, ai, *carry)
        hr = hr + xr_ref[rows(j), :]
        hi = hi + xi_ref[rows(j), :]
        xr_ref[rows(j), :] = hr
        xi_ref[rows(j), :] = hi
        return hr, hi

    zero = jnp.zeros((S5_SEG, sw), F32)
    fr, fi = lax.fori_loop(0, seg_len, local_scan, (zero, zero))

    pr, pi = ar[:1], ai[:1]
    for _ in range(int(math.log2(seg_len))):
        pr, pi = _cmul(pr, pi, pr, pi)

    cr, ci = h0r_ref[...], h0i_ref[...]
    crs, cis = [], []
    for s in range(S5_SEG):
        crs.append(cr)
        cis.append(ci)
        tr, ti = _cmul(pr, pi, cr, ci)
        cr, ci = fr[s:s + 1] + tr, fi[s:s + 1] + ti
    hfr_ref[...] = cr
    hfi_ref[...] = ci
    c_re = jnp.concatenate(crs, axis=0)
    c_im = jnp.concatenate(cis, axis=0)

    def add_carry(j, pw):
        tr, ti = _cmul(pw[0], pw[1], c_re, c_im)
        xr_ref[rows(j), :] += tr
        xi_ref[rows(j), :] += ti
        return _cmul(pw[0], pw[1], ar, ai)

    lax.fori_loop(0, seg_len, add_carry, (ar, ai))

    y = mm(xr_ref[...], ccr_ref[...]) - mm(xi_ref[...], cci_ref[...]) + d_ref[...] * u
    g_ref[...] = jax.nn.gelu(y)


def s5_scan(u_perm, h0_re, h0_im, ab_re, ab_im, bb_re, bb_im, c_re, c_im, d_skip, *, batch, seq, precise):
    seg_len = seq // S5_SEG
    assert seg_len & (seg_len - 1) == 0
    nch = B_GROUPS // S5_CHUNK_GROUPS
    uw, sw = S5_CHUNK_GROUPS * B_GROUP, S5_CHUNK_GROUPS * B_STATE
    eye = jnp.eye(S5_CHUNK_GROUPS, dtype=F32)

    def in_blockdiag(bb):
        bb = bb.reshape(nch, S5_CHUNK_GROUPS, B_STATE, B_GROUP)
        return jnp.einsum("kgnc,gh->kgchn", bb, eye).reshape(nch, uw, sw)

    def out_blockdiag(cc):
        cc = cc.reshape(nch, S5_CHUNK_GROUPS, B_GROUP, B_STATE)
        return jnp.einsum("kgcn,gh->kgnhc", cc, eye).reshape(nch, sw, uw)

    row = lambda a: a.reshape(nch, 1, sw)
    st = lambda a: a.astype(F32).reshape(batch, nch, 1, sw)
    w_in = pl.BlockSpec((None, uw, sw), lambda b, c: (c, 0, 0))
    w_out = pl.BlockSpec((None, sw, uw), lambda b, c: (c, 0, 0))
    a_spec = pl.BlockSpec((None, 1, sw), lambda b, c: (c, 0, 0))
    st_spec = pl.BlockSpec((None, None, 1, sw), lambda b, c: (b, c, 0, 0))
    u_spec = pl.BlockSpec((seq, uw), lambda b, c: (b, c))
    g, hfr, hfi = pl.pallas_call(
        functools.partial(_s5_kernel, seg_len=seg_len, precise=precise),
        grid=(batch, nch),
        in_specs=[u_spec, w_in, w_in, w_out, w_out, a_spec, a_spec,
                  pl.BlockSpec((1, uw), lambda b, c: (0, c)), st_spec, st_spec],
        out_specs=[u_spec, st_spec, st_spec],
        out_shape=[jax.ShapeDtypeStruct((batch * seq, B_WIDTH), F32),
                   jax.ShapeDtypeStruct((batch, nch, 1, sw), F32),
                   jax.ShapeDtypeStruct((batch, nch, 1, sw), F32)],
        scratch_shapes=[pltpu.VMEM((seq, sw), F32), pltpu.VMEM((seq, sw), F32)],
        compiler_params=_params("parallel", "parallel"),
        name="s5_scan",
    )(u_perm, in_blockdiag(bb_re), in_blockdiag(bb_im), out_blockdiag(c_re), out_blockdiag(c_im),
      row(ab_re), row(ab_im), d_skip.reshape(1, B_WIDTH), st(h0_re), st(h0_im))
    return g, hfr.reshape(batch, B_GROUPS, B_STATE), hfi.reshape(batch, B_GROUPS, B_STATE)


def to_segment_order(a, batch, seq):
    seg_len = seq // S5_SEG
    return a.reshape(batch, S5_SEG, seg_len, -1).swapaxes(1, 2).reshape(batch * seq, -1)


def from_segment_order(a, batch, seq):
    seg_len = seq // S5_SEG
    return a.reshape(batch, seg_len, S5_SEG, -1).swapaxes(1, 2).reshape(batch * seq, -1)


def _merge_kernel(o0, o1, o2, l0, l1, l2, out_ref):
    la, lb, lc = l0[...], l1[...], l2[...]
    m = jnp.maximum(jnp.maximum(la, lb), lc)
    wa, wb, wc = jnp.exp(la - m), jnp.exp(lb - m), jnp.exp(lc - m)
    out = (wa * o0[...] + wb * o1[...] + wc * o2[...]) / (wa + wb + wc)
    out_ref[...] = out.astype(out_ref.dtype)


def merge_groups(outs, lses):
    m, w = outs[0].shape
    tm = min(m, 512)
    spec = pl.BlockSpec((tm, w), lambda i: (i, 0))
    return pl.pallas_call(
        _merge_kernel, grid=(m // tm,), in_specs=[spec] * 6, out_specs=spec,
        out_shape=jax.ShapeDtypeStruct((m, w), BF16), compiler_params=_params("parallel"),
        name="merge_groups",
    )(*outs, *lses)


DSA_TQ = 256
C_TQ = 128
SAMPLE_TK = PAGES_PER_STEP * PAGE_SIZE
NEW_ROWS = 128


def _pad_rows(a, rows):
    return jnp.pad(a, ((0, 0), (0, rows - a.shape[1]), (0, 0)))


def even_mixer(x, h, *, dims, li, w_in, w_rest, w_idx_qb, w_out, w_glu, b_glu, s5c, bias_table, h0_re, h0_im,
               pool_kv, pool_kidx, page_table):
    bp, tp, bs, ts = dims
    mp, ms = bp * tp, bs * ts
    ab_re, ab_im, bb_re, bb_im, c_re, c_im, d_skip = s5c

    q = linear(h, w_in, li, n_cols=Q_A_W, col_off=0, tn=512, out_dtypes=(BF16,))
    kv, kvb = linear(h, w_in, li, n_cols=KV_A_W, col_off=OFF_KV, tn=512, out_dtypes=(F32, BF16))
    cq = linear(h, w_in, li, n_cols=R_IDX, col_off=OFF_CQ, tn=512, out_dtypes=(BF16,))
    rest = linear(h, w_rest, 0, tn=256)
    u, ki, wi = rest[:, :B_WIDTH], rest[:, B_WIDTH:B_WIDTH + D_IDX], rest[:, B_WIDTH + D_IDX:B_WIDTH + D_IDX + H_IDX]
    qi = linear(cq, w_idx_qb, li, tn=512, out_dtypes=(BF16,))

    scores = indexer_prompt(qi, rest, batch=bp, seq=tp, tq=DSA_TQ,
                            ki_blk=B_WIDTH // LANES, wi_blk=B_WIDTH // LANES + 1)
    sel = topk_mask(scores, min(TOPK_MAX, tp // 4), tr=256)
    lib = toeplitz_bias_tiles(bias_table, tp // DSA_TQ, DSA_TQ, 1, tp)
    att_p = dsa_prompt_attention(q, kvb, lib, sel, batch=bp, seq=tp, tq=DSA_TQ)
    zeros = jnp.zeros((bp, B_GROUPS, B_STATE), F32)
    g_perm, hp_re, hp_im = s5_scan(to_segment_order(u[:mp], bp, tp), zeros, zeros, ab_re, ab_im, bb_re, bb_im,
                                   c_re, c_im, d_skip, batch=bp, seq=tp, precise=False)
    g_p = from_segment_order(g_perm, bp, tp)

    ki_new = _pad_rows(ki[mp:].reshape(bs, ts, D_IDX), NEW_ROWS)
    scores = indexer_sample(qi[mp:].reshape(bs, ts, H_IDX * D_IDX), wi[mp:].reshape(bs, ts, H_IDX),
                            ki_new, pool_kidx, li, page_table, tk=SAMPLE_TK)
    s_pad = scores.shape[-1]
    sel = topk_mask(scores.reshape(ms, s_pad), min(TOPK_MAX, (PAST_LEN + ts) // 4), tr=ms).reshape(bs, ts, s_pad)
    bias = sample_bias(bias_table, PAST_LEN, ts, s_pad, lambda d: d >= 0)
    kv_new = _pad_rows(kv[mp:].reshape(bs, ts, KV_A_W), NEW_ROWS)
    att_s = sample_attention(q[mp:].astype(F32).reshape(bs, ts, Q_A_W), pool_kv, li, kv_new, bias, sel,
                             hq=H_A, hkv=H_KV_A, page_table=page_table, tk=SAMPLE_TK, with_lse=False)
    g_s, hs_re, hs_im = s5_scan(u[mp:], h0_re, h0_im, ab_re, ab_im, bb_re, bb_im,
                                c_re, c_im, d_skip, batch=bs, seq=ts, precise=True)

    att = jnp.concatenate([att_p, att_s.reshape(ms, Q_A_W).astype(BF16)], axis=0)
    g = jnp.concatenate([g_p, g_s], axis=0)
    ssm = linear(g.astype(BF16), w_glu, li, tn=512, mode="glu", extras=(g, b_glu.reshape(1, B_WIDTH)),
                 out_dtypes=(BF16,))
    x = linear(att, w_out, li, row_off=0, tn=512, mode="resid", extras=(x,))
    x = linear(ssm, w_out, li, row_off=Q_A_W, tn=512, mode="resid", extras=(x,))
    return x, kv, ki, (hp_re, hp_im), (hs_re, hs_im)


def odd_mixer(x, h, li, w_in, w_out, bias_table, *, dims, caches):
    bp, tp, bs, ts = dims
    mp, ms = bp * tp, bs * ts
    q = linear(h, w_in, li, n_cols=C_WIDTH, col_off=0, tn=512, out_dtypes=(BF16,))
    kv, kvb = linear(h, w_in, li, n_cols=2 * C_WIDTH, col_off=C_WIDTH, tn=512, out_dtypes=(F32, BF16))
    q_p, kvb_p = q[:mp], kvb[:mp]
    kv5_p = kv[:mp].reshape(bp, tp, 2, H_C, HEAD_DIM)
    kv5_s = kv[mp:].reshape(bs, ts, 2, H_C, HEAD_DIM)
    outs_p, lses_p, outs_s, lses_s, rows_p, rows_s = [], [], [], [], [], []
    for g, (window, dilation) in enumerate(C_PAIRS):
        cols = bias_table[:, g * H_C_G:(g + 1) * H_C_G]
        heads = slice(g * H_C_G, (g + 1) * H_C_G)
        lib = toeplitz_bias_tiles(cols, 2, C_TQ, dilation, window // dilation)
        o, l = dilated_prompt_group(q_p, kvb_p, lib, batch=bp, seq=tp, g=g, dilation=dilation, tq=C_TQ)
        outs_p.append(o)
        lses_p.append(l)
        rows_p.append(kv5_p[:, tp - min(window, tp):, :, heads])

        buf = caches[g]
        wb = buf.shape[2]
        tk = min(wb, 512)
        s_pad = wb + tk
        bias = sample_bias(cols, wb, ts, s_pad,
                           lambda d: (d >= 0) & (d <= window) & (d % dilation == 0))
        kv_g = kv5_s[:, :, :, heads]
        new_kv = _pad_rows(kv_g.reshape(bs, ts, 2 * C_OUT), NEW_ROWS)
        qg = q[mp:, g * C_OUT:(g + 1) * C_OUT].astype(F32).reshape(bs, ts, C_OUT)
        o, l = sample_attention(qg, buf, li, new_kv, bias, None, hq=H_C_G, hkv=H_C_G, tk=tk, with_lse=True)
        outs_s.append(o.reshape(ms, C_OUT))
        lses_s.append(l.reshape(ms, C_OUT))
        rows_s.append(kv_g)
    att = jnp.concatenate([merge_groups(outs_p, lses_p), merge_groups(outs_s, lses_s)], axis=0)
    x = linear(att, w_out, li, tn=512, mode="resid", extras=(x,))
    return x, rows_p, rows_s


def kernel(x_prompt, x_sample, cache_a_kv, cache_a_kidx, state_b_re, state_b_im, cache_c0_kv, cache_c1_kv, cache_c2_kv, page_table, p_prompt, p_sample, bias_table, norm_g, final_g, ffn1_w1, ffn1_w3, ffn1_w2, ffn2_w1, ffn2_w3, ffn2_w2, ple_gate, ple_proj, w_in_even, w_idx_qb, w_out_even, s5_lam_re, s5_lam_im, s5_log_step, s5_b_re, s5_b_im, s5_c_re, s5_c_im, s5_d, s5_w_glu, s5_b_glu, w_in_odd, w_out_odd):
    bp, tp, _ = x_prompt.shape
    bs, ts, _ = x_sample.shape
    mp, ms = bp * tp, bs * ts
    dims = (bp, tp, bs, ts)
    x = jnp.concatenate([x_prompt.reshape(mp, D_MODEL), x_sample.reshape(ms, D_MODEL)], axis=0)
    p_all = jnp.concatenate([p_prompt.reshape(DEPTH, mp, D_PLE), p_sample.reshape(DEPTH, ms, D_PLE)], axis=1)
    outs_p = {k: [] for k in ("kv", "ki", "re", "im", "c0", "c1", "c2")}
    outs_s = {k: [] for k in ("kv", "ki", "re", "im", "c0", "c1", "c2")}
    for i in range(DEPTH):
        li = i // 2
        x = macaron_half(x, norm_g[i, 0], ffn1_w1, ffn1_w3, ffn1_w2, i)
        h = rmsnorm(x, norm_g[i, 1], BF16)
        if i % 2 == 0:
            w_in = w_in_even[li]
            w_rest = jnp.concatenate([w_in[:, OFF_U:], w_in[:, OFF_KI:OFF_U], w_in[:, OFF_W:OFF_KI],
                                      jnp.zeros((D_MODEL, LANES - H_IDX), w_in.dtype)], axis=1)[None]
            s5c = (*s5_params(s5_lam_re[li], s5_lam_im[li], s5_log_step[li], s5_b_re[li], s5_b_im[li]),
                   s5_c_re[li], s5_c_im[li], s5_d[li])
            x, kv, ki, st_p, st_s = even_mixer(
                x, h, dims=dims, li=li, w_in=w_in_even, w_rest=w_rest, w_idx_qb=w_idx_qb, w_out=w_out_even,
                w_glu=s5_w_glu, b_glu=s5_b_glu[li], s5c=s5c, bias_table=bias_table, h0_re=state_b_re[li],
                h0_im=state_b_im[li], pool_kv=cache_a_kv, pool_kidx=cache_a_kidx, page_table=page_table)
            outs_p["kv"].append(kv[:mp].reshape(bp, tp, 2, H_KV_A, HEAD_DIM))
            outs_p["ki"].append(ki[:mp].reshape(bp, tp, D_IDX))
            outs_p["re"].append(st_p[0])
            outs_p["im"].append(st_p[1])
            outs_s["kv"].append(kv[mp:].reshape(bs, ts, 2, H_KV_A, HEAD_DIM))
            outs_s["ki"].append(ki[mp:].reshape(bs, ts, D_IDX))
            outs_s["re"].append(st_s[0])
            outs_s["im"].append(st_s[1])
        else:
            x, rows_p, rows_s = odd_mixer(x, h, li, w_in_odd, w_out_odd, bias_table, dims=dims,
                                          caches=(cache_c0_kv, cache_c1_kv, cache_c2_kv))
            for g in range(N_C_GROUPS):
                outs_p["c%d" % g].append(rows_p[g])
                outs_s["c%d" % g].append(rows_s[g])
        x = macaron_half(x, norm_g[i, 2], ffn2_w1, ffn2_w3, ffn2_w2, i)
        x = ple_add(x, p_all[i], norm_g[i, 3], ple_gate, ple_proj, i)
    y_prompt = rmsnorm(x, final_g, F32, rows=mp).reshape(bp, tp, D_MODEL)
    y_sample = rmsnorm(x, final_g, F32, row_off=mp, rows=ms).reshape(bs, ts, D_MODEL)
    order = ("kv", "ki", "re", "im", "c0", "c1", "c2")
    return (y_prompt, y_sample, *[jnp.stack(outs_p[k]) for k in order], *[jnp.stack(outs_s[k]) for k in order])
```

```python
import functools
import math

import jax
import jax.numpy as jnp
from jax import lax
from jax.experimental import pallas as pl
from jax.experimental.pallas import tpu as pltpu

D_MODEL = 4096
DEPTH = 2
PAST_LEN = 16384
PAGE_SIZE = 128
HEAD_DIM = 128
EPS = 1e-6
H_A = 24
H_KV_A = 8
H_IDX = 32
D_IDX = 128
R_IDX = 512
TOPK_MAX = 256
B_WIDTH = 1024
B_GROUP = 16
B_GROUPS = 64
B_STATE = 64
C_PAIRS = ((128, 1), (512, 4), (2048, 16))
N_C_GROUPS = 3
H_C_G = 8
H_C = 24
C_WIDTH = H_C * HEAD_DIM
C_OUT = H_C_G * HEAD_DIM
NUM_BUCKETS = 32
MAX_DISTANCE = 2048
D_FF = 11008
D_PLE = 256
Q_A_W = H_A * HEAD_DIM
KV_A_W = 2 * H_KV_A * HEAD_DIM
OFF_KV = Q_A_W
OFF_CQ = OFF_KV + KV_A_W
OFF_W = OFF_CQ + R_IDX
OFF_KI = OFF_W + H_IDX
OFF_U = OFF_KI + D_IDX
IN_EVEN = OFF_U + B_WIDTH

LANES = 128
SUBLANES = 8
VMEM_LIMIT = 56 * 1024 * 1024

LINEAR_VMEM_BUDGET = 44 * 1024 * 1024
ROW_TILES = (1376, 1024, 512, 256, 128, 64, 16)

S5_SEG = SUBLANES
S5_CHUNK_GROUPS = 8
PAGES_PER_STEP = 8
NEG_INIT = -1e30

F32 = jnp.float32
BF16 = jnp.bfloat16
NT_DIMS = (((1,), (1,)), ((), ()))


def _params(*sem):
    return pltpu.CompilerParams(dimension_semantics=sem, vmem_limit_bytes=VMEM_LIMIT)


def _rms_kernel(x_ref, g_ref, o_ref):
    x = x_ref[...]
    ms = jnp.mean(x * x, axis=-1, keepdims=True)
    o_ref[...] = (x * lax.rsqrt(ms + EPS) * g_ref[...]).astype(o_ref.dtype)


def _row_tile(m, candidates):
    for c in candidates:
        if m % c == 0:
            return c
    return m


def rmsnorm(x, g, out_dtype, *, row_off=0, rows=None):
    d = x.shape[1]
    rows = x.shape[0] if rows is None else rows
    tm = _row_tile(math.gcd(rows, row_off) if row_off else rows, (688, 256, 64, 16))
    blk0 = row_off // tm
    return pl.pallas_call(
        _rms_kernel,
        grid=(rows // tm,),
        in_specs=[pl.BlockSpec((tm, d), lambda i: (i + blk0, 0)), pl.BlockSpec((1, d), lambda i: (0, 0))],
        out_specs=pl.BlockSpec((tm, d), lambda i: (i, 0)),
        out_shape=jax.ShapeDtypeStruct((rows, d), out_dtype),
        compiler_params=_params("parallel"),
        name="rmsnorm",
    )(x, g.reshape(1, d))


def _linear_kernel(*refs, mode, n_out, scale):
    a_ref, w_ref = refs[0], refs[1]
    extras = refs[2:len(refs) - n_out]
    outs = refs[len(refs) - n_out:]
    acc = jnp.dot(a_ref[...].astype(BF16), w_ref[...].astype(BF16), preferred_element_type=F32)
    if mode == "none":
        res = acc
    elif mode == "resid":
        res = extras[0][...] + scale * acc
    elif mode == "glu":
        g = extras[0][...]
        res = g * jax.nn.sigmoid(acc + extras[1][...])
    elif mode == "ple":
        x_ref, p_ref, wp_ref = extras
        proj = jnp.dot(p_ref[...].astype(BF16), wp_ref[...].astype(BF16), preferred_element_type=F32)
        res = x_ref[...] + jax.nn.sigmoid(acc) * proj
    else:
        raise ValueError(mode)
    for o in outs:
        o[...] = res.astype(o.dtype)


def linear(a, w, li, *, n_cols=None, col_off=0, row_off=0, tn, mode="none", extras=(),
           out_dtypes=(F32,), scale=1.0):
    m, k = a.shape
    n_cols = w.shape[2] if n_cols is None else n_cols
    tm = _row_tile(m, ROW_TILES)
    n_tiles = len(out_dtypes) + (0 if mode == "none" else 1)
    while (4 * tm * k + 10 * k * tn + 4 * tm * tn * (1 + 2 * n_tiles) > LINEAR_VMEM_BUDGET and tn > LANES
           and col_off % (tn // 2) == 0):
        tn //= 2
    assert m % tm == 0 and n_cols % tn == 0 and col_off % tn == 0 and row_off % k == 0
    jo, ro = col_off // tn, row_off // k
    tile = pl.BlockSpec((tm, tn), lambda i, j: (i, j))
    in_specs = [pl.BlockSpec((tm, k), lambda i, j: (i, 0)),
                pl.BlockSpec((None, k, tn), lambda i, j: (li, ro, j + jo))]
    if mode == "resid":
        in_specs += [tile]
    elif mode == "glu":
        in_specs += [tile, pl.BlockSpec((1, tn), lambda i, j: (0, j))]
    elif mode == "ple":
        kp = extras[1].shape[1]
        in_specs += [tile, pl.BlockSpec((tm, kp), lambda i, j: (i, 0)),
                     pl.BlockSpec((None, kp, tn), lambda i, j: (li, 0, j))]
    res = pl.pallas_call(
        functools.partial(_linear_kernel, mode=mode, n_out=len(out_dtypes), scale=scale),
        grid=(m // tm, n_cols // tn),
        in_specs=in_specs,
        out_specs=[tile] * len(out_dtypes),
        out_shape=[jax.ShapeDtypeStruct((m, n_cols), dt) for dt in out_dtypes],
        compiler_params=_params("parallel", "arbitrary"),
        name="linear_" + mode,
    )(a, w, *extras)
    return res if len(out_dtypes) > 1 else res[0]


FFN_TN = 256
DOWN_TK = 1024


def _swiglu_kernel(a_ref, w1_ref, w3_ref, o_ref, *, n_real):
    j = pl.program_id(1)

    @pl.when(j < n_real)
    def _():
        a = a_ref[...]
        g = jnp.dot(a, w1_ref[...].astype(BF16), preferred_element_type=F32)
        u = jnp.dot(a, w3_ref[...].astype(BF16), preferred_element_type=F32)
        o_ref[...] = (jax.nn.silu(g) * u).astype(o_ref.dtype)

    @pl.when(j >= n_real)
    def _():
        o_ref[...] = jnp.zeros(o_ref.shape, o_ref.dtype)


def swiglu_up(h, w1, w3, li, *, n_pad):
    m, k = h.shape
    n = w1.shape[2]
    tn = FFN_TN
    tm = _row_tile(m, ROW_TILES)
    n_real = n // tn
    w_spec = pl.BlockSpec((None, k, tn), lambda i, j: (li, 0, jnp.minimum(j, n_real - 1)))
    return pl.pallas_call(
        functools.partial(_swiglu_kernel, n_real=n_real),
        grid=(m // tm, n_pad // tn),
        in_specs=[pl.BlockSpec((tm, k), lambda i, j: (i, 0)), w_spec, w_spec],
        out_specs=pl.BlockSpec((tm, tn), lambda i, j: (i, j)),
        out_shape=jax.ShapeDtypeStruct((m, n_pad), BF16),
        compiler_params=_params("parallel", "arbitrary"),
        name="swiglu_up",
    )(h, w1, w3)


def _down_kernel(a_ref, w_ref, x_ref, o_ref, *, scale, tail_rows):
    kk = pl.program_id(2)
    last = pl.num_programs(2) - 1

    @pl.when(kk == 0)
    def _():
        o_ref[...] = x_ref[...]

    @pl.when(kk < last)
    def _():
        o_ref[...] += scale * jnp.dot(a_ref[...], w_ref[...].astype(BF16), preferred_element_type=F32)

    @pl.when(kk == last)
    def _():
        w = w_ref[...]
        row = lax.broadcasted_iota(jnp.int32, w.shape, 0)
        w = jnp.where(row < tail_rows, w, 0.0).astype(BF16)
        o_ref[...] += scale * jnp.dot(a_ref[...], w, preferred_element_type=F32)


def down_resid(act, w, li, x, *, scale):
    m, k_pad = act.shape
    k, n = w.shape[1:]
    tm = _row_tile(m, (2048,) + ROW_TILES)
    tn = min(n, 1024)
    tk = DOWN_TK
    assert k_pad % tk == 0 and k_pad - k < tk
    return pl.pallas_call(
        functools.partial(_down_kernel, scale=scale, tail_rows=tk - (k_pad - k)),
        grid=(m // tm, n // tn, k_pad // tk),
        in_specs=[pl.BlockSpec((tm, tk), lambda i, j, kk: (i, kk)),
                  pl.BlockSpec((None, tk, tn), lambda i, j, kk: (li, kk, j)),
                  pl.BlockSpec((tm, tn), lambda i, j, kk: (i, j))],
        out_specs=pl.BlockSpec((tm, tn), lambda i, j, kk: (i, j)),
        out_shape=jax.ShapeDtypeStruct((m, n), F32),
        compiler_params=_params("parallel", "parallel", "arbitrary"),
        name="down_resid",
    )(act, w, x)


def macaron_half(x, g, w1, w3, w2, li):
    h = rmsnorm(x, g, BF16)
    n_pad = -(-w1.shape[2] // DOWN_TK) * DOWN_TK
    act = swiglu_up(h, w1, w3, li, n_pad=n_pad)
    return down_resid(act, w2, li, x, scale=0.5)


def ple_add(x, p, g, w_gate, w_proj, li):
    h = rmsnorm(x, g, BF16)
    return linear(h, w_gate, li, tn=256, mode="ple", extras=(x, p.astype(BF16), w_proj))


def t5_bucket(dist):
    max_exact = NUM_BUCKETS // 2
    d = jnp.maximum(dist, 0)
    ratio = jnp.log(jnp.maximum(d, 1).astype(F32) / max_exact) / math.log(MAX_DISTANCE / max_exact)
    large = jnp.minimum(max_exact + (ratio * (NUM_BUCKETS - max_exact)).astype(jnp.int32), NUM_BUCKETS - 1)
    return jnp.where(d < max_exact, d, large)


def bias_of_dist(bias_cols, dist, valid):
    onehot = t5_bucket(dist)[..., None] == jnp.arange(NUM_BUCKETS)
    b = jnp.sum(jnp.where(onehot[..., None], bias_cols.astype(F32), 0.0), axis=-2)
    b = jnp.where(valid[..., None], b, -jnp.inf)
    return jnp.moveaxis(b, -1, 0)


def sample_bias(bias_cols, base, t, s_pad, valid_fn):
    n = jnp.arange(s_pad + t - 1)
    dist = base + (t - 1) - n
    table = bias_of_dist(bias_cols, dist, valid_fn(dist))
    return jnp.stack([table[:, t - 1 - i:t - 1 - i + s_pad] for i in range(t)], axis=1)


def _toeplitz_kernel(w_ref, o_ref):
    nd, tq, _ = o_ref.shape
    for d in range(nd):
        x = jnp.broadcast_to(w_ref[d], (tq, 2 * tq))
        r = pltpu.roll(x, 0, 1, stride=1, stride_axis=0)
        o_ref[d] = r[:, tq:]


def toeplitz_bias_tiles(bias_cols, n_delta, tq, step, max_steps):
    h = bias_cols.shape[1]
    steps = jnp.arange(n_delta)[:, None] * tq + tq - jnp.arange(2 * tq)[None, :]
    rows = bias_of_dist(bias_cols, steps * step, (steps >= 0) & (steps <= max_steps))
    return pl.pallas_call(
        _toeplitz_kernel,
        grid=(h,),
        in_specs=[pl.BlockSpec((None, n_delta, 1, 2 * tq), lambda i: (i, 0, 0, 0))],
        out_specs=pl.BlockSpec((None, n_delta, tq, tq), lambda i: (i, 0, 0, 0)),
        out_shape=jax.ShapeDtypeStruct((h, n_delta, tq, tq), F32),
        compiler_params=_params("parallel"),
        name="toeplitz_bias_tiles",
    )(rows.reshape(h, n_delta, 1, 2 * tq))


def _flash_init(m_ref, l_ref, acc_ref):
    m_ref[...] = jnp.full(m_ref.shape, NEG_INIT, F32)
    l_ref[...] = jnp.zeros(l_ref.shape, F32)
    acc_ref[...] = jnp.zeros(acc_ref.shape, F32)


def _flash_step(q_ref, k_at, v_at, bias_at, sel, m_ref, l_ref, acc_ref, *, hq, hkv):
    rep = hq // hkv
    tq = q_ref.shape[0]
    scale = HEAD_DIM ** -0.5
    stack = lambda parts: parts[0] if rep == 1 else jnp.concatenate(parts, axis=0)
    sel_rows = None if sel is None else stack([sel] * rep)
    for g in range(hkv):
        heads = range(g * rep, (g + 1) * rep)
        cols = [slice(h * HEAD_DIM, (h + 1) * HEAD_DIM) for h in heads]
        qs = stack([q_ref[:, c] for c in cols]).astype(BF16)
        s = lax.dot_general(qs, k_at(g), NT_DIMS, preferred_element_type=F32) * scale
        s = s + stack([bias_at(h) for h in heads])
        if sel_rows is not None:
            s = jnp.where(sel_rows, s, -jnp.inf)
        gsl = slice(g * rep, (g + 1) * rep)
        m_old = m_ref[gsl].reshape(rep * tq, LANES)
        m_new = jnp.maximum(m_old, jnp.max(s, axis=1, keepdims=True))
        alpha = jnp.exp(m_old - m_new)
        p = jnp.exp(s - m_new[:, :1])
        l_new = alpha * l_ref[gsl].reshape(rep * tq, LANES) + jnp.sum(p, axis=1, keepdims=True)
        pv = jnp.dot(p.astype(BF16), v_at(g), preferred_element_type=F32)
        for r, c in enumerate(cols):
            rows = slice(r * tq, (r + 1) * tq)
            acc_ref[:, c] = alpha[rows] * acc_ref[:, c] + pv[rows]
        l_ref[gsl] = l_new.reshape(rep, tq, LANES)
        m_ref[gsl] = m_new.reshape(rep, tq, LANES)


def _flash_finish(o_ref, lse_ref, m_ref, l_ref, acc_ref, *, hq):
    for h in range(hq):
        hs = slice(h * HEAD_DIM, (h + 1) * HEAD_DIM)
        l = l_ref[h]
        o_ref[:, hs] = (acc_ref[:, hs] / l).astype(o_ref.dtype)
        if lse_ref is not None:
            lse_ref[:, hs] = m_ref[h] + jnp.log(l)


def _flash_scratch(tq, hq):
    return [pltpu.VMEM((hq, tq, LANES), F32), pltpu.VMEM((hq, tq, LANES), F32),
            pltpu.VMEM((tq, hq * HEAD_DIM), F32)]


def _prompt_attn_kernel(*refs, hq, hkv, has_sel, with_lse, first_kb):
    q_ref, k_ref, v_ref, bias_ref = refs[:4]
    pos = 4
    sel_ref = None
    if has_sel:
        sel_ref = refs[pos]
        pos += 1
    o_ref = refs[pos]
    pos += 1
    lse_ref = None
    if with_lse:
        lse_ref = refs[pos]
        pos += 1
    m_ref, l_ref, acc_ref = refs[pos:pos + 3]
    qb = pl.program_id(2)
    kk = pl.program_id(3)
    nk = pl.num_programs(3)

    @pl.when(kk == 0)
    def _():
        _flash_init(m_ref, l_ref, acc_ref)

    @pl.when(first_kb(qb, kk) >= 0)
    def _():
        sel = None if sel_ref is None else sel_ref[...] > 0.0
        head = lambda ref: (lambda g: ref[:, g * HEAD_DIM:(g + 1) * HEAD_DIM].astype(BF16))
        _flash_step(q_ref, head(k_ref), head(v_ref), lambda h: bias_ref[h], sel, m_ref, l_ref, acc_ref,
                    hq=hq, hkv=hkv)

    @pl.when(kk == nk - 1)
    def _():
        _flash_finish(o_ref, lse_ref, m_ref, l_ref, acc_ref, hq=hq)


def dsa_prompt_attention(q, kvb, bias_lib, sel, *, batch, seq, tq):
    nq = seq // tq
    grid = (batch, 1, nq, nq)

    def kb_of(qb, kk):
        return jnp.minimum(kk, qb)

    in_specs = [
        pl.BlockSpec((tq, Q_A_W), lambda b, r, qb, kk: (b * nq + qb, 0)),
        pl.BlockSpec((tq, H_KV_A * HEAD_DIM), lambda b, r, qb, kk: (b * nq + kb_of(qb, kk), 0)),
        pl.BlockSpec((tq, H_KV_A * HEAD_DIM), lambda b, r, qb, kk: (b * nq + kb_of(qb, kk), 1)),
        pl.BlockSpec((H_A, None, tq, tq), lambda b, r, qb, kk: (0, qb - kb_of(qb, kk), 0, 0)),
        pl.BlockSpec((tq, tq), lambda b, r, qb, kk: (b * nq + qb, kb_of(qb, kk))),
    ]
    return pl.pallas_call(
        functools.partial(_prompt_attn_kernel, hq=H_A, hkv=H_KV_A, has_sel=True, with_lse=False,
                          first_kb=lambda qb, kk: qb - kk),
        grid=grid,
        in_specs=in_specs,
        out_specs=pl.BlockSpec((tq, Q_A_W), lambda b, r, qb, kk: (b * nq + qb, 0)),
        out_shape=jax.ShapeDtypeStruct((batch * seq, Q_A_W), BF16),
        scratch_shapes=_flash_scratch(tq, H_A),
        compiler_params=_params("parallel", "arbitrary", "arbitrary", "arbitrary"),
        name="dsa_prompt_attention",
    )(q, kvb, kvb, bias_lib, sel)


def _dilated_kernel(q_ref, kp_ref, kc_ref, vp_ref, vc_ref, bias_ref, o_ref, lse_ref, *, dilation, tq):
    scale = HEAD_DIM ** -0.5
    col = lax.broadcasted_iota(jnp.int32, (tq, 2 * tq), 1)
    usable = (col >= tq) | (pl.program_id(1) > 0)
    for h in range(q_ref.shape[1] // HEAD_DIM):
        hs = slice(h * HEAD_DIM, (h + 1) * HEAD_DIM)
        for r in range(dilation):
            rows = pl.ds(r, tq, stride=dilation) if dilation > 1 else slice(None)
            k = jnp.concatenate([kp_ref[rows, hs], kc_ref[rows, hs]], axis=0).astype(BF16)
            v = jnp.concatenate([vp_ref[rows, hs], vc_ref[rows, hs]], axis=0).astype(BF16)
            s = lax.dot_general(q_ref[rows, hs].astype(BF16), k, NT_DIMS, preferred_element_type=F32)
            s = jnp.where(usable, s * scale + bias_ref[h], -jnp.inf)
            m = jnp.max(s, axis=1, keepdims=True)
            p = jnp.exp(s - m)
            l = jnp.sum(p, axis=1, keepdims=True)
            o_ref[rows, hs] = jnp.dot(p.astype(BF16), v, preferred_element_type=F32) / l
            lse_ref[rows, hs] = jnp.broadcast_to(m + jnp.log(l), (tq, HEAD_DIM))


def dilated_prompt_group(q, kv, bias_lib, *, batch, seq, g, dilation, tq):
    span = tq * dilation
    nq = seq // span
    hps = H_C_G if dilation == 1 else 1
    cw = hps * HEAD_DIM
    hblocks = H_C_G // hps
    bias = jnp.concatenate([bias_lib[:, 1], bias_lib[:, 0]], axis=-1)
    prev = lambda qb: jnp.maximum(qb - 1, 0)
    blk = lambda f: pl.BlockSpec((span, cw), f)
    k0, v0 = g * hblocks, (H_C // hps) + g * hblocks
    in_specs = [
        blk(lambda b, qb, hb: (b * nq + qb, g * hblocks + hb)),
        blk(lambda b, qb, hb: (b * nq + prev(qb), k0 + hb)),
        blk(lambda b, qb, hb: (b * nq + qb, k0 + hb)),
        blk(lambda b, qb, hb: (b * nq + prev(qb), v0 + hb)),
        blk(lambda b, qb, hb: (b * nq + qb, v0 + hb)),
        pl.BlockSpec((hps, tq, 2 * tq), lambda b, qb, hb: (hb, 0, 0)),
    ]
    out_spec = blk(lambda b, qb, hb: (b * nq + qb, hb))
    return pl.pallas_call(
        functools.partial(_dilated_kernel, dilation=dilation, tq=tq),
        grid=(batch, nq, hblocks),
        in_specs=in_specs,
        out_specs=[out_spec, out_spec],
        out_shape=[jax.ShapeDtypeStruct((batch * seq, C_OUT), F32)] * 2,
        compiler_params=_params("parallel", "arbitrary", "arbitrary"),
        name="dilated_prompt_group",
    )(q, kv, kv, kv, kv, bias)


def _sample_attn_kernel(*refs, hq, hkv, n_src, has_sel, with_lse, has_table):
    if has_table:
        refs = refs[1:]
    q_ref = refs[0]
    src_refs = refs[1:1 + n_src]
    new_ref, bias_ref = refs[1 + n_src], refs[2 + n_src]
    pos = 3 + n_src
    sel_ref = None
    if has_sel:
        sel_ref = refs[pos]
        pos += 1
    o_ref = refs[pos]
    pos += 1
    lse_ref = None
    if with_lse:
        lse_ref = refs[pos]
        pos += 1
    m_ref, l_ref, acc_ref = refs[pos:pos + 3]
    kb = pl.program_id(1)
    nk = pl.num_programs(1)
    kw = hkv * HEAD_DIM
    new_rows = new_ref.shape[0]

    @pl.when(kb == 0)
    def _():
        _flash_init(m_ref, l_ref, acc_ref)

    def cached(which):
        def at(g):
            parts = [r[:, which, g, :] for r in src_refs]
            return (parts[0] if n_src == 1 else jnp.concatenate(parts, axis=0)).astype(BF16)
        return at

    def fresh(which):
        return lambda g: new_ref[:, which * kw + g * HEAD_DIM:which * kw + (g + 1) * HEAD_DIM].astype(BF16)

    @pl.when(kb < nk - 1)
    def _():
        sel = None if sel_ref is None else sel_ref[...] > 0.0
        _flash_step(q_ref, cached(0), cached(1), lambda h: bias_ref[h], sel,
                    m_ref, l_ref, acc_ref, hq=hq, hkv=hkv)

    @pl.when(kb == nk - 1)
    def _():
        sel = None if sel_ref is None else sel_ref[:, :new_rows] > 0.0
        _flash_step(q_ref, fresh(0), fresh(1), lambda h: bias_ref[h, :, :new_rows],
                    sel, m_ref, l_ref, acc_ref, hq=hq, hkv=hkv)
        _flash_finish(o_ref, lse_ref, m_ref, l_ref, acc_ref, hq=hq)


def sample_attention(q, src, li, new_kv, bias, sel, *, hq, hkv, page_table=None, tk, with_lse):
    bsz, t, _ = q.shape
    kvw = 2 * hkv * HEAD_DIM
    new_rows = new_kv.shape[1]
    paged = page_table is not None
    if paged:
        n_src = tk // PAGE_SIZE
        nk = page_table.shape[1] // n_src
        src_specs = [pl.BlockSpec((None, None, PAGE_SIZE, 2, hkv, HEAD_DIM),
                                  functools.partial(lambda p, b, kb, pt: (li, pt[b, jnp.minimum(kb, nk - 1) * n_src + p], 0, 0, 0, 0), p))
                     for p in range(n_src)]
        ix = lambda f: (lambda b, kb, pt: f(b, kb))
    else:
        n_src = 1
        nk = src.shape[2] // tk
        src_specs = [pl.BlockSpec((None, None, tk, 2, hkv, HEAD_DIM),
                                  lambda b, kb: (li, b, jnp.minimum(kb, nk - 1), 0, 0, 0))]
        ix = lambda f: f
    in_specs = ([pl.BlockSpec((None, t, hq * HEAD_DIM), ix(lambda b, kb: (b, 0, 0)))] + src_specs +
                [pl.BlockSpec((None, new_rows, kvw), ix(lambda b, kb: (b, 0, 0))),
                 pl.BlockSpec((hq, t, tk), ix(lambda b, kb: (0, 0, kb)))])
    args = [q] + [src] * n_src + [new_kv, bias]
    if sel is not None:
        in_specs.append(pl.BlockSpec((None, t, tk), ix(lambda b, kb: (b, 0, kb))))
        args.append(sel)
    out_spec = pl.BlockSpec((None, t, hq * HEAD_DIM), ix(lambda b, kb: (b, 0, 0)))
    n_out = 2 if with_lse else 1
    kern = functools.partial(_sample_attn_kernel, hq=hq, hkv=hkv, n_src=n_src, has_sel=sel is not None,
                             with_lse=with_lse, has_table=paged)
    out_shape = [jax.ShapeDtypeStruct((bsz, t, hq * HEAD_DIM), F32)] * n_out
    scratch = _flash_scratch(t, hq)
    cp = _params("parallel", "arbitrary")
    if paged:
        gs = pltpu.PrefetchScalarGridSpec(num_scalar_prefetch=1, grid=(bsz, nk + 1), in_specs=in_specs,
                                          out_specs=[out_spec] * n_out, scratch_shapes=scratch)
        res = pl.pallas_call(kern, grid_spec=gs, out_shape=out_shape, compiler_params=cp,
                             name="sample_attention_paged")(page_table, *args)
    else:
        res = pl.pallas_call(kern, grid=(bsz, nk + 1), in_specs=in_specs, out_specs=[out_spec] * n_out,
                             out_shape=out_shape, scratch_shapes=scratch, compiler_params=cp,
                             name="sample_attention")(*args)
    return res if with_lse else res[0]


IDX_SCALE = (H_IDX ** -0.5) * (D_IDX ** -0.5)


def _idx_prompt_kernel(qi_ref, ki_ref, wi_ref, o_ref, *, tq):
    qb, kb = pl.program_id(1), pl.program_id(2)

    @pl.when(kb > qb)
    def _():
        o_ref[...] = jnp.full(o_ref.shape, -jnp.inf, F32)

    @pl.when(kb <= qb)
    def _():
        kt = ki_ref[...].astype(BF16)
        w = wi_ref[...] * IDX_SCALE
        acc = jnp.zeros((tq, tq), F32)
        for h in range(H_IDX):
            s = lax.dot_general(qi_ref[:, h * D_IDX:(h + 1) * D_IDX], kt, NT_DIMS, preferred_element_type=F32)
            acc = acc + jnp.maximum(s, 0.0) * w[:, h:h + 1]
        row = lax.broadcasted_iota(jnp.int32, (tq, tq), 0) + qb * tq
        col = lax.broadcasted_iota(jnp.int32, (tq, tq), 1) + kb * tq
        o_ref[...] = jnp.where(col <= row, acc, -jnp.inf)


def indexer_prompt(qi, rest, *, batch, seq, tq, ki_blk, wi_blk):
    nq = seq // tq
    return pl.pallas_call(
        functools.partial(_idx_prompt_kernel, tq=tq),
        grid=(batch, nq, nq),
        in_specs=[pl.BlockSpec((tq, H_IDX * D_IDX), lambda b, qb, kb: (b * nq + qb, 0)),
                  pl.BlockSpec((tq, D_IDX), lambda b, qb, kb: (b * nq + jnp.minimum(kb, qb), ki_blk)),
                  pl.BlockSpec((tq, LANES), lambda b, qb, kb: (b * nq + qb, wi_blk))],
        out_specs=pl.BlockSpec((tq, tq), lambda b, qb, kb: (b * nq + qb, kb)),
        out_shape=jax.ShapeDtypeStruct((batch * seq, seq), F32),
        compiler_params=_params("parallel", "parallel", "arbitrary"),
        name="indexer_prompt",
    )(qi, rest, rest)


def _split_dot_nt(e, x):
    hi = x.astype(BF16)
    lo = (x - hi.astype(F32)).astype(BF16)
    return (lax.dot_general(e, hi, NT_DIMS, preferred_element_type=F32) +
            lax.dot_general(e, lo, NT_DIMS, preferred_element_type=F32))


def _idx_sample_kernel(pt_ref, qit_ref, w_ref, e_ref, *refs, n_src, t):
    del pt_ref
    src_refs = refs[:n_src]
    new_ref, o_ref = refs[n_src], refs[n_src + 1]
    kb = pl.program_id(1)
    nk = pl.num_programs(1)
    tk = o_ref.shape[1]
    new_rows = new_ref.shape[0]

    def scores(keys):
        st = jnp.dot(keys.astype(BF16), qit_ref[...], preferred_element_type=F32)
        x = jnp.maximum(st, 0.0) * w_ref[...]
        return _split_dot_nt(e_ref[...], x)[:t]

    @pl.when(kb < nk - 1)
    def _():
        o_ref[...] = scores(jnp.concatenate([r[...] for r in src_refs], axis=0))

    @pl.when(kb == nk - 1)
    def _():
        sc = scores(new_ref[...])
        row = lax.broadcasted_iota(jnp.int32, (t, new_rows), 0)
        col = lax.broadcasted_iota(jnp.int32, (t, new_rows), 1)
        o_ref[...] = jnp.full(o_ref.shape, -jnp.inf, F32)
        o_ref[:, :new_rows] = jnp.where(col <= row, sc, -jnp.inf)


def indexer_sample(qi, wi, ki_new, pool_kidx, li, page_table, *, tk):
    bsz, t, _ = qi.shape
    n_src = tk // PAGE_SIZE
    nk = page_table.shape[1] // n_src
    qit = qi.reshape(bsz, t * H_IDX, D_IDX).transpose(0, 2, 1)
    w = (wi * IDX_SCALE).reshape(bsz, 1, t * H_IDX)
    e_rows = 2 * SUBLANES
    e = jnp.repeat(jnp.eye(e_rows, t, dtype=BF16), H_IDX, axis=1)
    src_specs = [pl.BlockSpec((None, None, PAGE_SIZE, D_IDX),
                              functools.partial(lambda p, b, kb, pt: (li, pt[b, jnp.minimum(kb, nk - 1) * n_src + p], 0, 0), p))
                 for p in range(n_src)]
    gs = pltpu.PrefetchScalarGridSpec(
        num_scalar_prefetch=1, grid=(bsz, nk + 1),
        in_specs=[pl.BlockSpec((None, D_IDX, t * H_IDX), lambda b, kb, pt: (b, 0, 0)),
                  pl.BlockSpec((None, 1, t * H_IDX), lambda b, kb, pt: (b, 0, 0)),
                  pl.BlockSpec((e_rows, t * H_IDX), lambda b, kb, pt: (0, 0))] + src_specs +
                 [pl.BlockSpec((None, ki_new.shape[1], D_IDX), lambda b, kb, pt: (b, 0, 0))],
        out_specs=pl.BlockSpec((None, t, tk), lambda b, kb, pt: (b, 0, kb)))
    return pl.pallas_call(
        functools.partial(_idx_sample_kernel, n_src=n_src, t=t),
        grid_spec=gs,
        out_shape=jax.ShapeDtypeStruct((bsz, t, (nk + 1) * tk), F32),
        compiler_params=_params("parallel", "arbitrary"),
        name="indexer_sample",
    )(page_table, qit, w, e, *([pool_kidx] * n_src), ki_new)


TOPK_CHUNK = 512
INT_MIN = -2 ** 31


def _topk_kernel(s_ref, u_ref, o_ref, *, k):
    x = s_ref[...]
    rows, n = x.shape
    x = jnp.where(x == 0.0, 0.0, x)
    bits = lax.bitcast_convert_type(x, jnp.int32)
    key = bits ^ ((bits >> 31) & jnp.int32(0x7FFFFFFF))
    kf = jnp.float32(k)

    def body(it, tau_u):
        cand_u = tau_u | jnp.left_shift(jnp.int32(1), 31 - it)
        cand = cand_u ^ jnp.int32(INT_MIN)
        cnt = jnp.sum(jnp.where(key >= cand, 1.0, 0.0), axis=1, keepdims=True)
        return jnp.where(cnt >= kf, cand_u, tau_u)

    tau_u = lax.fori_loop(0, 32, body, jnp.zeros((rows, 1), jnp.int32))
    tau = tau_u ^ jnp.int32(INT_MIN)
    gt = key > tau
    eq = key == tau
    need = kf - jnp.sum(jnp.where(gt, 1.0, 0.0), axis=1, keepdims=True)
    run = jnp.zeros((rows, 1), F32)
    for c in range(n // TOPK_CHUNK):
        cs = slice(c * TOPK_CHUNK, (c + 1) * TOPK_CHUNK)
        e = jnp.where(eq[:, cs], 1.0, 0.0)
        before = jnp.dot(e.astype(BF16), u_ref[...], preferred_element_type=F32) + run
        o_ref[:, cs] = jnp.where(gt[:, cs] | (eq[:, cs] & (before < need)), 1.0, 0.0)
        run = run + jnp.sum(e, axis=1, keepdims=True)


def topk_mask(scores, k, *, tr):
    r, s = scores.shape
    assert s % TOPK_CHUNK == 0 and r % tr == 0
    ii = jnp.arange(TOPK_CHUNK)
    upper = (ii[:, None] < ii[None, :]).astype(BF16)
    return pl.pallas_call(
        functools.partial(_topk_kernel, k=k),
        grid=(r // tr,),
        in_specs=[pl.BlockSpec((tr, s), lambda i: (i, 0)),
                  pl.BlockSpec((TOPK_CHUNK, TOPK_CHUNK), lambda i: (0, 0))],
        out_specs=pl.BlockSpec((tr, s), lambda i: (i, 0)),
        out_shape=jax.ShapeDtypeStruct((r, s), F32),
        compiler_params=_params("parallel"),
        name="topk_mask",
    )(scores, upper)


def _s5_param_kernel(lr_ref, li_ref, ls_ref, br_ref, bi_ref, abr_ref, abi_ref, bbr_ref, bbi_ref):
    lr, li = lr_ref[...], li_ref[...]
    step = jnp.exp(ls_ref[...])
    mag = jnp.exp(lr * step)
    ab_re, ab_im = mag * jnp.cos(li * step), mag * jnp.sin(li * step)
    den = lr * lr + li * li
    nr, ni = ab_re - 1.0, ab_im
    f_re, f_im = (nr * lr + ni * li) / den, (ni * lr - nr * li) / den
    br, bi = br_ref[...], bi_ref[...]
    abr_ref[...] = ab_re
    abi_ref[...] = ab_im
    bbr_ref[...] = f_re * br - f_im * bi
    bbi_ref[...] = f_re * bi + f_im * br


def s5_params(lam_re, lam_im, log_step, b_re, b_im):
    w = B_STATE * B_GROUP
    rep = lambda a: jnp.repeat(a, B_GROUP, axis=1)
    ls = jnp.broadcast_to(log_step[:, None], (B_GROUPS, w))
    shp = jax.ShapeDtypeStruct((B_GROUPS, w), F32)
    abr, abi, bbr, bbi = pl.pallas_call(_s5_param_kernel, out_shape=[shp] * 4, name="s5_params")(
        rep(lam_re), rep(lam_im), ls, b_re.reshape(B_GROUPS, w), b_im.reshape(B_GROUPS, w))
    return (abr[:, ::B_GROUP], abi[:, ::B_GROUP],
            bbr.reshape(B_GROUPS, B_STATE, B_GROUP), bbi.reshape(B_GROUPS, B_STATE, B_GROUP))


def _cmul(ar, ai, br, bi):
    return ar * br - ai * bi, ar * bi + ai * br


def _s5_kernel(u_ref, bbr_ref, bbi_ref, ccr_ref, cci_ref, ar_ref, ai_ref, d_ref, h0r_ref, h0i_ref,
               g_ref, hfr_ref, hfi_ref, xr_ref, xi_ref, *, seg_len, precise):
    u = u_ref[...]
    if precise:
        mm = lambda a, b: jnp.dot(a, b, preferred_element_type=F32, precision=lax.Precision.HIGHEST)
    else:
        mm = lambda a, b: jnp.dot(a.astype(BF16), b.astype(BF16), preferred_element_type=F32)
    xr_ref[...] = mm(u, bbr_ref[...])
    xi_ref[...] = mm(u, bbi_ref[...])
    sw = xr_ref.shape[1]
    ar = jnp.broadcast_to(ar_ref[...], (S5_SEG, sw))
    ai = jnp.broadcast_to(ai_ref[...], (S5_SEG, sw))

    def rows(j):
        return pl.ds(pl.multiple_of(j * S5_SEG, S5_SEG), S5_SEG)

    def local_scan(j, carry):
        hr, hi = _cmul(ar, ai, *carry)
        hr = hr + xr_ref[rows(j), :]
        hi = hi + xi_ref[rows(j), :]
        xr_ref[rows(j), :] = hr
        xi_ref[rows(j), :] = hi
        return hr, hi

    zero = jnp.zeros((S5_SEG, sw), F32)
    fr, fi = lax.fori_loop(0, seg_len, local_scan, (zero, zero))

    pr, pi = ar[:1], ai[:1]
    for _ in range(int(math.log2(seg_len))):
        pr, pi = _cmul(pr, pi, pr, pi)

    cr, ci = h0r_ref[...], h0i_ref[...]
    crs, cis = [], []
    for s in range(S5_SEG):
        crs.append(cr)
        cis.append(ci)
        tr, ti = _cmul(pr, pi, cr, ci)
        cr, ci = fr[s:s + 1] + tr, fi[s:s + 1] + ti
    hfr_ref[...] = cr
    hfi_ref[...] = ci
    c_re = jnp.concatenate(crs, axis=0)
    c_im = jnp.concatenate(cis, axis=0)

    def add_carry(j, pw):
        tr, ti = _cmul(pw[0], pw[1], c_re, c_im)
        xr_ref[rows(j), :] += tr
        xi_ref[rows(j), :] += ti
        return _cmul(pw[0], pw[1], ar, ai)

    lax.fori_loop(0, seg_len, add_carry, (ar, ai))

    y = mm(xr_ref[...], ccr_ref[...]) - mm(xi_ref[...], cci_ref[...]) + d_ref[...] * u
    g_ref[...] = jax.nn.gelu(y)


def s5_scan(u_perm, h0_re, h0_im, ab_re, ab_im, bb_re, bb_im, c_re, c_im, d_skip, *, batch, seq, precise):
    seg_len = seq // S5_SEG
    assert seg_len & (seg_len - 1) == 0
    nch = B_GROUPS // S5_CHUNK_GROUPS
    uw, sw = S5_CHUNK_GROUPS * B_GROUP, S5_CHUNK_GROUPS * B_STATE
    eye = jnp.eye(S5_CHUNK_GROUPS, dtype=F32)

    def in_blockdiag(bb):
        bb = bb.reshape(nch, S5_CHUNK_GROUPS, B_STATE, B_GROUP)
        return jnp.einsum("kgnc,gh->kgchn", bb, eye).reshape(nch, uw, sw)

    def out_blockdiag(cc):
        cc = cc.reshape(nch, S5_CHUNK_GROUPS, B_GROUP, B_STATE)
        return jnp.einsum("kgcn,gh->kgnhc", cc, eye).reshape(nch, sw, uw)

    row = lambda a: a.reshape(nch, 1, sw)
    st = lambda a: a.astype(F32).reshape(batch, nch, 1, sw)
    w_in = pl.BlockSpec((None, uw, sw), lambda b, c: (c, 0, 0))
    w_out = pl.BlockSpec((None, sw, uw), lambda b, c: (c, 0, 0))
    a_spec = pl.BlockSpec((None, 1, sw), lambda b, c: (c, 0, 0))
    st_spec = pl.BlockSpec((None, None, 1, sw), lambda b, c: (b, c, 0, 0))
    u_spec = pl.BlockSpec((seq, uw), lambda b, c: (b, c))
    g, hfr, hfi = pl.pallas_call(
        functools.partial(_s5_kernel, seg_len=seg_len, precise=precise),
        grid=(batch, nch),
        in_specs=[u_spec, w_in, w_in, w_out, w_out, a_spec, a_spec,
                  pl.BlockSpec((1, uw), lambda b, c: (0, c)), st_spec, st_spec],
        out_specs=[u_spec, st_spec, st_spec],
        out_shape=[jax.ShapeDtypeStruct((batch * seq, B_WIDTH), F32),
                   jax.ShapeDtypeStruct((batch, nch, 1, sw), F32),
                   jax.ShapeDtypeStruct((batch, nch, 1, sw), F32)],
        scratch_shapes=[pltpu.VMEM((seq, sw), F32), pltpu.VMEM((seq, sw), F32)],
        compiler_params=_params("parallel", "parallel"),
        name="s5_scan",
    )(u_perm, in_blockdiag(bb_re), in_blockdiag(bb_im), out_blockdiag(c_re), out_blockdiag(c_im),
      row(ab_re), row(ab_im), d_skip.reshape(1, B_WIDTH), st(h0_re), st(h0_im))
    return g, hfr.reshape(batch, B_GROUPS, B_STATE), hfi.reshape(batch, B_GROUPS, B_STATE)


def to_segment_order(a, batch, seq):
    seg_len = seq // S5_SEG
    return a.reshape(batch, S5_SEG, seg_len, -1).swapaxes(1, 2).reshape(batch * seq, -1)


def from_segment_order(a, batch, seq):
    seg_len = seq // S5_SEG
    return a.reshape(batch, seg_len, S5_SEG, -1).swapaxes(1, 2).reshape(batch * seq, -1)


def _merge_kernel(o0, o1, o2, l0, l1, l2, out_ref):
    la, lb, lc = l0[...], l1[...], l2[...]
    m = jnp.maximum(jnp.maximum(la, lb), lc)
    wa, wb, wc = jnp.exp(la - m), jnp.exp(lb - m), jnp.exp(lc - m)
    out = (wa * o0[...] + wb * o1[...] + wc * o2[...]) / (wa + wb + wc)
    out_ref[...] = out.astype(out_ref.dtype)


def merge_groups(outs, lses):
    m, w = outs[0].shape
    tm = min(m, 512)
    spec = pl.BlockSpec((tm, w), lambda i: (i, 0))
    return pl.pallas_call(
        _merge_kernel, grid=(m // tm,), in_specs=[spec] * 6, out_specs=spec,
        out_shape=jax.ShapeDtypeStruct((m, w), BF16), compiler_params=_params("parallel"),
        name="merge_groups",
    )(*outs, *lses)


DSA_TQ = 256
C_TQ = 128
SAMPLE_TK = PAGES_PER_STEP * PAGE_SIZE
NEW_ROWS = 128


def _pad_rows(a, rows):
    return jnp.pad(a, ((0, 0), (0, rows - a.shape[1]), (0, 0)))


def even_mixer(x, h, *, dims, li, w_in, w_rest, w_idx_qb, w_out, w_glu, b_glu, s5c, bias_table, h0_re, h0_im,
               pool_kv, pool_kidx, page_table):
    bp, tp, bs, ts = dims
    mp, ms = bp * tp, bs * ts
    ab_re, ab_im, bb_re, bb_im, c_re, c_im, d_skip = s5c

    q = linear(h, w_in, li, n_cols=Q_A_W, col_off=0, tn=512, out_dtypes=(BF16,))
    kv, kvb = linear(h, w_in, li, n_cols=KV_A_W, col_off=OFF_KV, tn=512, out_dtypes=(F32, BF16))
    cq = linear(h, w_in, li, n_cols=R_IDX, col_off=OFF_CQ, tn=512, out_dtypes=(BF16,))
    rest = linear(h, w_rest, 0, tn=256)
    u, ki, wi = rest[:, :B_WIDTH], rest[:, B_WIDTH:B_WIDTH + D_IDX], rest[:, B_WIDTH + D_IDX:B_WIDTH + D_IDX + H_IDX]
    qi = linear(cq, w_idx_qb, li, tn=512, out_dtypes=(BF16,))

    scores = indexer_prompt(qi, rest, batch=bp, seq=tp, tq=DSA_TQ,
                            ki_blk=B_WIDTH // LANES, wi_blk=B_WIDTH // LANES + 1)
    sel = topk_mask(scores, min(TOPK_MAX, tp // 4), tr=256)
    lib = toeplitz_bias_tiles(bias_table, tp // DSA_TQ, DSA_TQ, 1, tp)
    att_p = dsa_prompt_attention(q, kvb, lib, sel, batch=bp, seq=tp, tq=DSA_TQ)
    zeros = jnp.zeros((bp, B_GROUPS, B_STATE), F32)
    g_perm, hp_re, hp_im = s5_scan(to_segment_order(u[:mp], bp, tp), zeros, zeros, ab_re, ab_im, bb_re, bb_im,
                                   c_re, c_im, d_skip, batch=bp, seq=tp, precise=False)
    g_p = from_segment_order(g_perm, bp, tp)

    ki_new = _pad_rows(ki[mp:].reshape(bs, ts, D_IDX), NEW_ROWS)
    scores = indexer_sample(qi[mp:].reshape(bs, ts, H_IDX * D_IDX), wi[mp:].reshape(bs, ts, H_IDX),
                            ki_new, pool_kidx, li, page_table, tk=SAMPLE_TK)
    s_pad = scores.shape[-1]
    sel = topk_mask(scores.reshape(ms, s_pad), min(TOPK_MAX, (PAST_LEN + ts) // 4), tr=ms).reshape(bs, ts, s_pad)
    bias = sample_bias(bias_table, PAST_LEN, ts, s_pad, lambda d: d >= 0)
    kv_new = _pad_rows(kv[mp:].reshape(bs, ts, KV_A_W), NEW_ROWS)
    att_s = sample_attention(q[mp:].astype(F32).reshape(bs, ts, Q_A_W), pool_kv, li, kv_new, bias, sel,
                             hq=H_A, hkv=H_KV_A, page_table=page_table, tk=SAMPLE_TK, with_lse=False)
    g_s, hs_re, hs_im = s5_scan(u[mp:], h0_re, h0_im, ab_re, ab_im, bb_re, bb_im,
                                c_re, c_im, d_skip, batch=bs, seq=ts, precise=True)

    att = jnp.concatenate([att_p, att_s.reshape(ms, Q_A_W).astype(BF16)], axis=0)
    g = jnp.concatenate([g_p, g_s], axis=0)
    ssm = linear(g.astype(BF16), w_glu, li, tn=512, mode="glu", extras=(g, b_glu.reshape(1, B_WIDTH)),
                 out_dtypes=(BF16,))
    x = linear(att, w_out, li, row_off=0, tn=512, mode="resid", extras=(x,))
    x = linear(ssm, w_out, li, row_off=Q_A_W, tn=512, mode="resid", extras=(x,))
    return x, kv, ki, (hp_re, hp_im), (hs_re, hs_im)


def odd_mixer(x, h, li, w_in, w_out, bias_table, *, dims, caches):
    bp, tp, bs, ts = dims
    mp, ms = bp * tp, bs * ts
    q = linear(h, w_in, li, n_cols=C_WIDTH, col_off=0, tn=512)
    kv = linear(h, w_in, li, n_cols=2 * C_WIDTH, col_off=C_WIDTH, tn=512)
    kv3_p = kv[:mp].reshape(bp, tp, 2 * C_WIDTH)
    kv5_s = kv[mp:].reshape(bs, ts, 2, H_C, HEAD_DIM)
    outs_p, lses_p, outs_s, lses_s, rows_p, rows_s = [], [], [], [], [], []
    for g, (window, dilation) in enumerate(C_PAIRS):
        cols = bias_table[:, g * H_C_G:(g + 1) * H_C_G]
        heads = slice(g * H_C_G, (g + 1) * H_C_G)
        lib = toeplitz_bias_tiles(cols, 2, C_TQ, dilation, window // dilation)
        o, l = dilated_prompt_group(q, kv, lib, batch=bp, seq=tp, g=g, dilation=dilation, tq=C_TQ)
        outs_p.append(o)
        lses_p.append(l)
        last = kv3_p[:, tp - min(window, tp):]
        rows_p.append(jnp.stack([last[:, :, g * C_OUT:(g + 1) * C_OUT].reshape(bp, -1, H_C_G, HEAD_DIM),
                                 last[:, :, C_WIDTH + g * C_OUT:C_WIDTH + (g + 1) * C_OUT].reshape(bp, -1, H_C_G, HEAD_DIM)],
                                axis=2))

        buf = caches[g]
        wb = buf.shape[2]
        tk = min(wb, 512)
        s_pad = wb + tk
        bias = sample_bias(cols, wb, ts, s_pad,
                           lambda d: (d >= 0) & (d <= window) & (d % dilation == 0))
        kv_g = kv5_s[:, :, :, heads]
        new_kv = _pad_rows(kv_g.reshape(bs, ts, 2 * C_OUT), NEW_ROWS)
        qg = q[mp:, g * C_OUT:(g + 1) * C_OUT].astype(F32).reshape(bs, ts, C_OUT)
        o, l = sample_attention(qg, buf, li, new_kv, bias, None, hq=H_C_G, hkv=H_C_G, tk=tk, with_lse=True)
        outs_s.append(o.reshape(ms, C_OUT))
        lses_s.append(l.reshape(ms, C_OUT))
        rows_s.append(kv_g)
    att = jnp.concatenate([merge_groups(outs_p, lses_p), merge_groups(outs_s, lses_s)], axis=0)
    x = linear(att, w_out, li, tn=512, mode="resid", extras=(x,))
    return x, rows_p, rows_s


def kernel(x_prompt, x_sample, cache_a_kv, cache_a_kidx, state_b_re, state_b_im, cache_c0_kv, cache_c1_kv, cache_c2_kv, page_table, p_prompt, p_sample, bias_table, norm_g, final_g, ffn1_w1, ffn1_w3, ffn1_w2, ffn2_w1, ffn2_w3, ffn2_w2, ple_gate, ple_proj, w_in_even, w_idx_qb, w_out_even, s5_lam_re, s5_lam_im, s5_log_step, s5_b_re, s5_b_im, s5_c_re, s5_c_im, s5_d, s5_w_glu, s5_b_glu, w_in_odd, w_out_odd):
    bp, tp, _ = x_prompt.shape
    bs, ts, _ = x_sample.shape
    mp, ms = bp * tp, bs * ts
    dims = (bp, tp, bs, ts)
    x = jnp.concatenate([x_prompt.reshape(mp, D_MODEL), x_sample.reshape(ms, D_MODEL)], axis=0)
    p_all = jnp.concatenate([p_prompt.reshape(DEPTH, mp, D_PLE), p_sample.reshape(DEPTH, ms, D_PLE)], axis=1)
    outs_p = {k: [] for k in ("kv", "ki", "re", "im", "c0", "c1", "c2")}
    outs_s = {k: [] for k in ("kv", "ki", "re", "im", "c0", "c1", "c2")}
    for i in range(DEPTH):
        li = i // 2
        x = macaron_half(x, norm_g[i, 0], ffn1_w1, ffn1_w3, ffn1_w2, i)
        h = rmsnorm(x, norm_g[i, 1], BF16)
        if i % 2 == 0:
            w_in = w_in_even[li]
            w_rest = jnp.concatenate([w_in[:, OFF_U:], w_in[:, OFF_KI:OFF_U], w_in[:, OFF_W:OFF_KI],
                                      jnp.zeros((D_MODEL, LANES - H_IDX), w_in.dtype)], axis=1)[None]
            s5c = (*s5_params(s5_lam_re[li], s5_lam_im[li], s5_log_step[li], s5_b_re[li], s5_b_im[li]),
                   s5_c_re[li], s5_c_im[li], s5_d[li])
            x, kv, ki, st_p, st_s = even_mixer(
                x, h, dims=dims, li=li, w_in=w_in_even, w_rest=w_rest, w_idx_qb=w_idx_qb, w_out=w_out_even,
                w_glu=s5_w_glu, b_glu=s5_b_glu[li], s5c=s5c, bias_table=bias_table, h0_re=state_b_re[li],
                h0_im=state_b_im[li], pool_kv=cache_a_kv, pool_kidx=cache_a_kidx, page_table=page_table)
            outs_p["kv"].append(kv[:mp].reshape(bp, tp, 2, H_KV_A, HEAD_DIM))
            outs_p["ki"].append(ki[:mp].reshape(bp, tp, D_IDX))
            outs_p["re"].append(st_p[0])
            outs_p["im"].append(st_p[1])
            outs_s["kv"].append(kv[mp:].reshape(bs, ts, 2, H_KV_A, HEAD_DIM))
            outs_s["ki"].append(ki[mp:].reshape(bs, ts, D_IDX))
            outs_s["re"].append(st_s[0])
            outs_s["im"].append(st_s[1])
        else:
            x, rows_p, rows_s = odd_mixer(x, h, li, w_in_odd, w_out_odd, bias_table, dims=dims,
                                          caches=(cache_c0_kv, cache_c1_kv, cache_c2_kv))
            for g in range(N_C_GROUPS):
                outs_p["c%d" % g].append(rows_p[g])
                outs_s["c%d" % g].append(rows_s[g])
        x = macaron_half(x, norm_g[i, 2], ffn2_w1, ffn2_w3, ffn2_w2, i)
        x = ple_add(x, p_all[i], norm_g[i, 3], ple_gate, ple_proj, i)
    y_prompt = rmsnorm(x, final_g, F32, rows=mp).reshape(bp, tp, D_MODEL)
    y_sample = rmsnorm(x, final_g, F32, row_off=mp, rows=ms).reshape(bs, ts, D_MODEL)
    order = ("kv", "ki", "re", "im", "c0", "c1", "c2")
    return (y_prompt, y_sample, *[jnp.stack(outs_p[k]) for k in order], *[jnp.stack(outs_s[k]) for k in order])
```

```python
import functools
import math

import jax
import jax.numpy as jnp
from jax import lax
from jax.experimental import pallas as pl
from jax.experimental.pallas import tpu as pltpu

D_MODEL = 4096
DEPTH = 2
PAST_LEN = 16384
PAGE_SIZE = 128
HEAD_DIM = 128
EPS = 1e-6
H_A = 24
H_KV_A = 8
H_IDX = 32
D_IDX = 128
R_IDX = 512
TOPK_MAX = 256
B_WIDTH = 1024
B_GROUP = 16
B_GROUPS = 64
B_STATE = 64
C_PAIRS = ((128, 1), (512, 4), (2048, 16))
N_C_GROUPS = 3
H_C_G = 8
H_C = 24
C_WIDTH = H_C * HEAD_DIM
C_OUT = H_C_G * HEAD_DIM
NUM_BUCKETS = 32
MAX_DISTANCE = 2048
D_FF = 11008
D_PLE = 256
Q_A_W = H_A * HEAD_DIM
KV_A_W = 2 * H_KV_A * HEAD_DIM
OFF_KV = Q_A_W
OFF_CQ = OFF_KV + KV_A_W
OFF_W = OFF_CQ + R_IDX
OFF_KI = OFF_W + H_IDX
OFF_U = OFF_KI + D_IDX
IN_EVEN = OFF_U + B_WIDTH

LANES = 128
SUBLANES = 8
VMEM_LIMIT = 56 * 1024 * 1024

LINEAR_VMEM_BUDGET = 44 * 1024 * 1024
ROW_TILES = (1376, 1024, 512, 256, 128, 64, 16)

S5_SEG = SUBLANES
S5_CHUNK_GROUPS = 8
PAGES_PER_STEP = 8
NEG_INIT = -1e30

F32 = jnp.float32
BF16 = jnp.bfloat16
NT_DIMS = (((1,), (1,)), ((), ()))


def _params(*sem):
    return pltpu.CompilerParams(dimension_semantics=sem, vmem_limit_bytes=VMEM_LIMIT)


def _rms_kernel(x_ref, g_ref, o_ref):
    x = x_ref[...]
    ms = jnp.mean(x * x, axis=-1, keepdims=True)
    o_ref[...] = (x * lax.rsqrt(ms + EPS) * g_ref[...]).astype(o_ref.dtype)


def _row_tile(m, candidates):
    for c in candidates:
        if m % c == 0:
            return c
    return m


def rmsnorm(x, g, out_dtype, *, row_off=0, rows=None):
    d = x.shape[1]
    rows = x.shape[0] if rows is None else rows
    tm = _row_tile(math.gcd(rows, row_off) if row_off else rows, (688, 256, 64, 16))
    blk0 = row_off // tm
    return pl.pallas_call(
        _rms_kernel,
        grid=(rows // tm,),
        in_specs=[pl.BlockSpec((tm, d), lambda i: (i + blk0, 0)), pl.BlockSpec((1, d), lambda i: (0, 0))],
        out_specs=pl.BlockSpec((tm, d), lambda i: (i, 0)),
        out_shape=jax.ShapeDtypeStruct((rows, d), out_dtype),
        compiler_params=_params("parallel"),
        name="rmsnorm",
    )(x, g.reshape(1, d))


def _linear_kernel(*refs, mode, n_out, scale):
    a_ref, w_ref = refs[0], refs[1]
    extras = refs[2:len(refs) - n_out]
    outs = refs[len(refs) - n_out:]
    acc = jnp.dot(a_ref[...].astype(BF16), w_ref[...].astype(BF16), preferred_element_type=F32)
    if mode == "none":
        res = acc
    elif mode == "resid":
        res = extras[0][...] + scale * acc
    elif mode == "glu":
        g = extras[0][...]
        res = g * jax.nn.sigmoid(acc + extras[1][...])
    elif mode == "ple":
        x_ref, p_ref, wp_ref = extras
        proj = jnp.dot(p_ref[...].astype(BF16), wp_ref[...].astype(BF16), preferred_element_type=F32)
        res = x_ref[...] + jax.nn.sigmoid(acc) * proj
    else:
        raise ValueError(mode)
    for o in outs:
        o[...] = res.astype(o.dtype)


def linear(a, w, li, *, n_cols=None, col_off=0, row_off=0, tn, mode="none", extras=(),
           out_dtypes=(F32,), scale=1.0):
    m, k = a.shape
    n_cols = w.shape[2] if n_cols is None else n_cols
    tm = _row_tile(m, ROW_TILES)
    n_tiles = len(out_dtypes) + (0 if mode == "none" else 1)
    while (4 * tm * k + 10 * k * tn + 4 * tm * tn * (1 + 2 * n_tiles) > LINEAR_VMEM_BUDGET and tn > LANES
           and col_off % (tn // 2) == 0):
        tn //= 2
    assert m % tm == 0 and n_cols % tn == 0 and col_off % tn == 0 and row_off % k == 0
    jo, ro = col_off // tn, row_off // k
    tile = pl.BlockSpec((tm, tn), lambda i, j: (i, j))
    in_specs = [pl.BlockSpec((tm, k), lambda i, j: (i, 0)),
                pl.BlockSpec((None, k, tn), lambda i, j: (li, ro, j + jo))]
    if mode == "resid":
        in_specs += [tile]
    elif mode == "glu":
        in_specs += [tile, pl.BlockSpec((1, tn), lambda i, j: (0, j))]
    elif mode == "ple":
        kp = extras[1].shape[1]
        in_specs += [tile, pl.BlockSpec((tm, kp), lambda i, j: (i, 0)),
                     pl.BlockSpec((None, kp, tn), lambda i, j: (li, 0, j))]
    res = pl.pallas_call(
        functools.partial(_linear_kernel, mode=mode, n_out=len(out_dtypes), scale=scale),
        grid=(m // tm, n_cols // tn),
        in_specs=in_specs,
        out_specs=[tile] * len(out_dtypes),
        out_shape=[jax.ShapeDtypeStruct((m, n_cols), dt) for dt in out_dtypes],
        compiler_params=_params("parallel", "arbitrary"),
        name="linear_" + mode,
    )(a, w, *extras)
    return res if len(out_dtypes) > 1 else res[0]


FFN_TN = 256
DOWN_TK = 1024


def _swiglu_kernel(a_ref, w1_ref, w3_ref, o_ref, *, n_real):
    j = pl.program_id(1)

    @pl.when(j < n_real)
    def _():
        a = a_ref[...]
        g = jnp.dot(a, w1_ref[...].astype(BF16), preferred_element_type=F32)
        u = jnp.dot(a, w3_ref[...].astype(BF16), preferred_element_type=F32)
        o_ref[...] = (jax.nn.silu(g) * u).astype(o_ref.dtype)

    @pl.when(j >= n_real)
    def _():
        o_ref[...] = jnp.zeros(o_ref.shape, o_ref.dtype)


def swiglu_up(h, w1, w3, li, *, n_pad):
    m, k = h.shape
    n = w1.shape[2]
    tn = FFN_TN
    tm = _row_tile(m, ROW_TILES)
    n_real = n // tn
    w_spec = pl.BlockSpec((None, k, tn), lambda i, j: (li, 0, jnp.minimum(j, n_real - 1)))
    return pl.pallas_call(
        functools.partial(_swiglu_kernel, n_real=n_real),
        grid=(m // tm, n_pad // tn),
        in_specs=[pl.BlockSpec((tm, k), lambda i, j: (i, 0)), w_spec, w_spec],
        out_specs=pl.BlockSpec((tm, tn), lambda i, j: (i, j)),
        out_shape=jax.ShapeDtypeStruct((m, n_pad), BF16),
        compiler_params=_params("parallel", "arbitrary"),
        name="swiglu_up",
    )(h, w1, w3)


def _down_kernel(a_ref, w_ref, x_ref, o_ref, *, scale, tail_rows):
    kk = pl.program_id(2)
    last = pl.num_programs(2) - 1

    @pl.when(kk == 0)
    def _():
        o_ref[...] = x_ref[...]

    @pl.when(kk < last)
    def _():
        o_ref[...] += scale * jnp.dot(a_ref[...], w_ref[...].astype(BF16), preferred_element_type=F32)

    @pl.when(kk == last)
    def _():
        w = w_ref[...]
        row = lax.broadcasted_iota(jnp.int32, w.shape, 0)
        w = jnp.where(row < tail_rows, w, 0.0).astype(BF16)
        o_ref[...] += scale * jnp.dot(a_ref[...], w, preferred_element_type=F32)


def down_resid(act, w, li, x, *, scale):
    m, k_pad = act.shape
    k, n = w.shape[1:]
    tm = _row_tile(m, (2048,) + ROW_TILES)
    tn = min(n, 1024)
    tk = DOWN_TK
    assert k_pad % tk == 0 and k_pad - k < tk
    return pl.pallas_call(
        functools.partial(_down_kernel, scale=scale, tail_rows=tk - (k_pad - k)),
        grid=(m // tm, n // tn, k_pad // tk),
        in_specs=[pl.BlockSpec((tm, tk), lambda i, j, kk: (i, kk)),
                  pl.BlockSpec((None, tk, tn), lambda i, j, kk: (li, kk, j)),
                  pl.BlockSpec((tm, tn), lambda i, j, kk: (i, j))],
        out_specs=pl.BlockSpec((tm, tn), lambda i, j, kk: (i, j)),
        out_shape=jax.ShapeDtypeStruct((m, n), F32),
        compiler_params=_params("parallel", "parallel", "arbitrary"),
        name="down_resid",
    )(act, w, x)


def macaron_half(x, g, w1, w3, w2, li):
    h = rmsnorm(x, g, BF16)
    n_pad = -(-w1.shape[2] // DOWN_TK) * DOWN_TK
    act = swiglu_up(h, w1, w3, li, n_pad=n_pad)
    return down_resid(act, w2, li, x, scale=0.5)


def ple_add(x, p, g, w_gate, w_proj, li):
    h = rmsnorm(x, g, BF16)
    return linear(h, w_gate, li, tn=256, mode="ple", extras=(x, p.astype(BF16), w_proj))


def t5_bucket(dist):
    max_exact = NUM_BUCKETS // 2
    d = jnp.maximum(dist, 0)
    ratio = jnp.log(jnp.maximum(d, 1).astype(F32) / max_exact) / math.log(MAX_DISTANCE / max_exact)
    large = jnp.minimum(max_exact + (ratio * (NUM_BUCKETS - max_exact)).astype(jnp.int32), NUM_BUCKETS - 1)
    return jnp.where(d < max_exact, d, large)


def bias_of_dist(bias_cols, dist, valid):
    onehot = t5_bucket(dist)[..., None] == jnp.arange(NUM_BUCKETS)
    b = jnp.sum(jnp.where(onehot[..., None], bias_cols.astype(F32), 0.0), axis=-2)
    b = jnp.where(valid[..., None], b, -jnp.inf)
    return jnp.moveaxis(b, -1, 0)


def sample_bias(bias_cols, base, t, s_pad, valid_fn):
    n = jnp.arange(s_pad + t - 1)
    dist = base + (t - 1) - n
    table = bias_of_dist(bias_cols, dist, valid_fn(dist))
    return jnp.stack([table[:, t - 1 - i:t - 1 - i + s_pad] for i in range(t)], axis=1)


def _toeplitz_kernel(w_ref, o_ref):
    nd, tq, _ = o_ref.shape
    for d in range(nd):
        x = jnp.broadcast_to(w_ref[d], (tq, 2 * tq))
        r = pltpu.roll(x, 0, 1, stride=1, stride_axis=0)
        o_ref[d] = r[:, tq:]


def toeplitz_bias_tiles(bias_cols, n_delta, tq, step, max_steps, transposed=False):
    h = bias_cols.shape[1]
    lane = jnp.arange(2 * tq)[None, :]
    steps = jnp.arange(n_delta)[:, None] * tq + (lane - tq if transposed else tq - lane)
    rows = bias_of_dist(bias_cols, steps * step, (steps >= 0) & (steps <= max_steps))
    return pl.pallas_call(
        _toeplitz_kernel,
        grid=(h,),
        in_specs=[pl.BlockSpec((None, n_delta, 1, 2 * tq), lambda i: (i, 0, 0, 0))],
        out_specs=pl.BlockSpec((None, n_delta, tq, tq), lambda i: (i, 0, 0, 0)),
        out_shape=jax.ShapeDtypeStruct((h, n_delta, tq, tq), F32),
        compiler_params=_params("parallel"),
        name="toeplitz_bias_tiles",
    )(rows.reshape(h, n_delta, 1, 2 * tq))


def _flash_init(m_ref, l_ref, acc_ref):
    m_ref[...] = jnp.full(m_ref.shape, NEG_INIT, F32)
    l_ref[...] = jnp.zeros(l_ref.shape, F32)
    acc_ref[...] = jnp.zeros(acc_ref.shape, F32)


def _flash_step(q_ref, k_at, v_at, bias_at, sel, m_ref, l_ref, acc_ref, *, hq, hkv):
    rep = hq // hkv
    tq = q_ref.shape[0]
    scale = HEAD_DIM ** -0.5
    stack = lambda parts: parts[0] if rep == 1 else jnp.concatenate(parts, axis=0)
    sel_rows = None if sel is None else stack([sel] * rep)
    for g in range(hkv):
        heads = range(g * rep, (g + 1) * rep)
        cols = [slice(h * HEAD_DIM, (h + 1) * HEAD_DIM) for h in heads]
        qs = stack([q_ref[:, c] for c in cols]).astype(BF16)
        s = lax.dot_general(qs, k_at(g), NT_DIMS, preferred_element_type=F32) * scale
        s = s + stack([bias_at(h) for h in heads])
        if sel_rows is not None:
            s = jnp.where(sel_rows, s, -jnp.inf)
        gsl = slice(g * rep, (g + 1) * rep)
        m_old = m_ref[gsl].reshape(rep * tq, LANES)
        m_new = jnp.maximum(m_old, jnp.max(s, axis=1, keepdims=True))
        alpha = jnp.exp(m_old - m_new)
        p = jnp.exp(s - m_new[:, :1])
        l_new = alpha * l_ref[gsl].reshape(rep * tq, LANES) + jnp.sum(p, axis=1, keepdims=True)
        pv = jnp.dot(p.astype(BF16), v_at(g), preferred_element_type=F32)
        for r, c in enumerate(cols):
            rows = slice(r * tq, (r + 1) * tq)
            acc_ref[:, c] = alpha[rows] * acc_ref[:, c] + pv[rows]
        l_ref[gsl] = l_new.reshape(rep, tq, LANES)
        m_ref[gsl] = m_new.reshape(rep, tq, LANES)


def _flash_finish(o_ref, lse_ref, m_ref, l_ref, acc_ref, *, hq):
    for h in range(hq):
        hs = slice(h * HEAD_DIM, (h + 1) * HEAD_DIM)
        l = l_ref[h]
        o_ref[:, hs] = (acc_ref[:, hs] / l).astype(o_ref.dtype)
        if lse_ref is not None:
            lse_ref[:, hs] = m_ref[h] + jnp.log(l)


def _flash_scratch(tq, hq):
    return [pltpu.VMEM((hq, tq, LANES), F32), pltpu.VMEM((hq, tq, LANES), F32),
            pltpu.VMEM((tq, hq * HEAD_DIM), F32)]


TN_DIMS = (((0,), (0,)), ((), ()))


def _dsa_prompt_kernel(q_ref, k_ref, v_ref, bias_ref, sel_ref, o_ref, m_ref, l_ref, acc_ref):
    qb, kk = pl.program_id(1), pl.program_id(2)
    tq = q_ref.shape[0]
    rep = H_A // H_KV_A
    scale = HEAD_DIM ** -0.5

    @pl.when(kk == 0)
    def _():
        m_ref[...] = jnp.full(m_ref.shape, NEG_INIT, F32)
        l_ref[...] = jnp.zeros(l_ref.shape, F32)
        acc_ref[...] = jnp.zeros(acc_ref.shape, F32)

    @pl.when(kk <= qb)
    def _():
        sel_t = jnp.concatenate([sel_ref[...].T] * rep, axis=1) > 0.0
        for g in range(H_KV_A):
            heads = range(g * rep, (g + 1) * rep)
            gs = slice(g * HEAD_DIM, (g + 1) * HEAD_DIM)
            qs = jnp.concatenate([q_ref[:, h * HEAD_DIM:(h + 1) * HEAD_DIM] for h in heads], axis=0)
            s = lax.dot_general(k_ref[:, gs], qs, NT_DIMS, preferred_element_type=F32)
            s = s * scale + jnp.concatenate([bias_ref[h] for h in heads], axis=1)
            s = jnp.where(sel_t, s, -jnp.inf)
            m_old = m_ref[g]
            m_new = jnp.maximum(m_old, jnp.max(s, axis=0, keepdims=True))
            alpha = jnp.exp(m_old - m_new)
            p = jnp.exp(s - m_new)
            l_ref[g] = alpha * l_ref[g] + jnp.sum(p, axis=0, keepdims=True)
            pv = lax.dot_general(v_ref[:, gs], p.astype(BF16), TN_DIMS, preferred_element_type=F32)
            acc_ref[g] = alpha * acc_ref[g] + pv
            m_ref[g] = m_new

    @pl.when(kk == pl.num_programs(2) - 1)
    def _():
        for g in range(H_KV_A):
            out_t = acc_ref[g] / l_ref[g]
            for r in range(rep):
                h = g * rep + r
                o_ref[:, h * HEAD_DIM:(h + 1) * HEAD_DIM] = out_t[:, r * tq:(r + 1) * tq].T.astype(o_ref.dtype)


def dsa_prompt_attention(q, kvb, bias_lib_t, sel, *, batch, seq, tq):
    nq = seq // tq
    rep = H_A // H_KV_A
    kb_of = lambda qb, kk: jnp.minimum(kk, qb)
    kvw = H_KV_A * HEAD_DIM
    in_specs = [
        pl.BlockSpec((tq, Q_A_W), lambda b, qb, kk: (b * nq + qb, 0)),
        pl.BlockSpec((tq, kvw), lambda b, qb, kk: (b * nq + kb_of(qb, kk), 0)),
        pl.BlockSpec((tq, kvw), lambda b, qb, kk: (b * nq + kb_of(qb, kk), 1)),
        pl.BlockSpec((H_A, None, tq, tq), lambda b, qb, kk: (0, qb - kb_of(qb, kk), 0, 0)),
        pl.BlockSpec((tq, tq), lambda b, qb, kk: (b * nq + qb, kb_of(qb, kk))),
    ]
    return pl.pallas_call(
        _dsa_prompt_kernel,
        grid=(batch, nq, nq),
        in_specs=in_specs,
        out_specs=pl.BlockSpec((tq, Q_A_W), lambda b, qb, kk: (b * nq + qb, 0)),
        out_shape=jax.ShapeDtypeStruct((batch * seq, Q_A_W), BF16),
        scratch_shapes=[pltpu.VMEM((H_KV_A, 1, rep * tq), F32), pltpu.VMEM((H_KV_A, 1, rep * tq), F32),
                        pltpu.VMEM((H_KV_A, HEAD_DIM, rep * tq), F32)],
        compiler_params=_params("parallel", "arbitrary", "arbitrary"),
        name="dsa_prompt_attention",
    )(q, kvb, kvb, bias_lib_t, sel)


def _dilated_kernel(q_ref, kp_ref, kc_ref, vp_ref, vc_ref, bias_ref, o_ref, lse_ref, *, dilation, tq):
    scale = HEAD_DIM ** -0.5
    col = lax.broadcasted_iota(jnp.int32, (tq, 2 * tq), 1)
    usable = (col >= tq) | (pl.program_id(1) > 0)
    for h in range(q_ref.shape[1] // HEAD_DIM):
        hs = slice(h * HEAD_DIM, (h + 1) * HEAD_DIM)
        for r in range(dilation):
            rows = pl.ds(r, tq, stride=dilation) if dilation > 1 else slice(None)
            k = jnp.concatenate([kp_ref[rows, hs], kc_ref[rows, hs]], axis=0).astype(BF16)
            v = jnp.concatenate([vp_ref[rows, hs], vc_ref[rows, hs]], axis=0).astype(BF16)
            s = lax.dot_general(q_ref[rows, hs].astype(BF16), k, NT_DIMS, preferred_element_type=F32)
            s = jnp.where(usable, s * scale + bias_ref[h], -jnp.inf)
            m = jnp.max(s, axis=1, keepdims=True)
            p = jnp.exp(s - m)
            l = jnp.sum(p, axis=1, keepdims=True)
            o_ref[rows, hs] = jnp.dot(p.astype(BF16), v, preferred_element_type=F32) / l
            lse_ref[rows, hs] = jnp.broadcast_to(m + jnp.log(l), (tq, HEAD_DIM))


def dilated_prompt_group(q, kv, bias_lib, *, batch, seq, g, dilation, tq):
    span = tq * dilation
    nq = seq // span
    hps = H_C_G if dilation == 1 else 1
    cw = hps * HEAD_DIM
    hblocks = H_C_G // hps
    bias = jnp.concatenate([bias_lib[:, 1], bias_lib[:, 0]], axis=-1)
    prev = lambda qb: jnp.maximum(qb - 1, 0)
    blk = lambda f: pl.BlockSpec((span, cw), f)
    k0, v0 = g * hblocks, (H_C // hps) + g * hblocks
    in_specs = [
        blk(lambda b, qb, hb: (b * nq + qb, g * hblocks + hb)),
        blk(lambda b, qb, hb: (b * nq + prev(qb), k0 + hb)),
        blk(lambda b, qb, hb: (b * nq + qb, k0 + hb)),
        blk(lambda b, qb, hb: (b * nq + prev(qb), v0 + hb)),
        blk(lambda b, qb, hb: (b * nq + qb, v0 + hb)),
        pl.BlockSpec((hps, tq, 2 * tq), lambda b, qb, hb: (hb, 0, 0)),
    ]
    out_spec = blk(lambda b, qb, hb: (b * nq + qb, hb))
    return pl.pallas_call(
        functools.partial(_dilated_kernel, dilation=dilation, tq=tq),
        grid=(batch, nq, hblocks),
        in_specs=in_specs,
        out_specs=[out_spec, out_spec],
        out_shape=[jax.ShapeDtypeStruct((batch * seq, C_OUT), F32)] * 2,
        compiler_params=_params("parallel", "arbitrary", "arbitrary"),
        name="dilated_prompt_group",
    )(q, kv, kv, kv, kv, bias)


def _sample_attn_kernel(*refs, hq, hkv, n_src, has_sel, with_lse, has_table):
    if has_table:
        refs = refs[1:]
    q_ref = refs[0]
    src_refs = refs[1:1 + n_src]
    new_ref, bias_ref = refs[1 + n_src], refs[2 + n_src]
    pos = 3 + n_src
    sel_ref = None
    if has_sel:
        sel_ref = refs[pos]
        pos += 1
    o_ref = refs[pos]
    pos += 1
    lse_ref = None
    if with_lse:
        lse_ref = refs[pos]
        pos += 1
    m_ref, l_ref, acc_ref = refs[pos:pos + 3]
    kb = pl.program_id(1)
    nk = pl.num_programs(1)
    kw = hkv * HEAD_DIM
    new_rows = new_ref.shape[0]

    @pl.when(kb == 0)
    def _():
        _flash_init(m_ref, l_ref, acc_ref)

    def cached(which):
        def at(g):
            parts = [r[:, which, g, :] for r in src_refs]
            return (parts[0] if n_src == 1 else jnp.concatenate(parts, axis=0)).astype(BF16)
        return at

    def fresh(which):
        return lambda g: new_ref[:, which * kw + g * HEAD_DIM:which * kw + (g + 1) * HEAD_DIM].astype(BF16)

    @pl.when(kb < nk - 1)
    def _():
        sel = None if sel_ref is None else sel_ref[...] > 0.0
        _flash_step(q_ref, cached(0), cached(1), lambda h: bias_ref[h], sel,
                    m_ref, l_ref, acc_ref, hq=hq, hkv=hkv)

    @pl.when(kb == nk - 1)
    def _():
        sel = None if sel_ref is None else sel_ref[:, :new_rows] > 0.0
        _flash_step(q_ref, fresh(0), fresh(1), lambda h: bias_ref[h, :, :new_rows],
                    sel, m_ref, l_ref, acc_ref, hq=hq, hkv=hkv)
        _flash_finish(o_ref, lse_ref, m_ref, l_ref, acc_ref, hq=hq)


def sample_attention(q, src, li, new_kv, bias, sel, *, hq, hkv, page_table=None, tk, with_lse):
    bsz, t, _ = q.shape
    kvw = 2 * hkv * HEAD_DIM
    new_rows = new_kv.shape[1]
    paged = page_table is not None
    if paged:
        n_src = tk // PAGE_SIZE
        nk = page_table.shape[1] // n_src
        src_specs = [pl.BlockSpec((None, None, PAGE_SIZE, 2, hkv, HEAD_DIM),
                                  functools.partial(lambda p, b, kb, pt: (li, pt[b, jnp.minimum(kb, nk - 1) * n_src + p], 0, 0, 0, 0), p))
                     for p in range(n_src)]
        ix = lambda f: (lambda b, kb, pt: f(b, kb))
    else:
        n_src = 1
        nk = src.shape[2] // tk
        src_specs = [pl.BlockSpec((None, None, tk, 2, hkv, HEAD_DIM),
                                  lambda b, kb: (li, b, jnp.minimum(kb, nk - 1), 0, 0, 0))]
        ix = lambda f: f
    in_specs = ([pl.BlockSpec((None, t, hq * HEAD_DIM), ix(lambda b, kb: (b, 0, 0)))] + src_specs +
                [pl.BlockSpec((None, new_rows, kvw), ix(lambda b, kb: (b, 0, 0))),
                 pl.BlockSpec((hq, t, tk), ix(lambda b, kb: (0, 0, kb)))])
    args = [q] + [src] * n_src + [new_kv, bias]
    if sel is not None:
        in_specs.append(pl.BlockSpec((None, t, tk), ix(lambda b, kb: (b, 0, kb))))
        args.append(sel)
    out_spec = pl.BlockSpec((None, t, hq * HEAD_DIM), ix(lambda b, kb: (b, 0, 0)))
    n_out = 2 if with_lse else 1
    kern = functools.partial(_sample_attn_kernel, hq=hq, hkv=hkv, n_src=n_src, has_sel=sel is not None,
                             with_lse=with_lse, has_table=paged)
    out_shape = [jax.ShapeDtypeStruct((bsz, t, hq * HEAD_DIM), F32)] * n_out
    scratch = _flash_scratch(t, hq)
    cp = _params("parallel", "arbitrary")
    if paged:
        gs = pltpu.PrefetchScalarGridSpec(num_scalar_prefetch=1, grid=(bsz, nk + 1), in_specs=in_specs,
                                          out_specs=[out_spec] * n_out, scratch_shapes=scratch)
        res = pl.pallas_call(kern, grid_spec=gs, out_shape=out_shape, compiler_params=cp,
                             name="sample_attention_paged")(page_table, *args)
    else:
        res = pl.pallas_call(kern, grid=(bsz, nk + 1), in_specs=in_specs, out_specs=[out_spec] * n_out,
                             out_shape=out_shape, scratch_shapes=scratch, compiler_params=cp,
                             name="sample_attention")(*args)
    return res if with_lse else res[0]


IDX_SCALE = (H_IDX ** -0.5) * (D_IDX ** -0.5)


def _idx_prompt_kernel(qi_ref, ki_ref, wi_ref, o_ref, *, tq):
    qb, kb = pl.program_id(1), pl.program_id(2)

    @pl.when(kb > qb)
    def _():
        o_ref[...] = jnp.full(o_ref.shape, -jnp.inf, F32)

    @pl.when(kb <= qb)
    def _():
        kt = ki_ref[...].astype(BF16)
        w = wi_ref[...] * IDX_SCALE
        acc = jnp.zeros((tq, tq), F32)
        for h in range(H_IDX):
            s = lax.dot_general(qi_ref[:, h * D_IDX:(h + 1) * D_IDX], kt, NT_DIMS, preferred_element_type=F32)
            acc = acc + jnp.maximum(s, 0.0) * w[:, h:h + 1]
        row = lax.broadcasted_iota(jnp.int32, (tq, tq), 0) + qb * tq
        col = lax.broadcasted_iota(jnp.int32, (tq, tq), 1) + kb * tq
        o_ref[...] = jnp.where(col <= row, acc, -jnp.inf)


def indexer_prompt(qi, rest, *, batch, seq, tq, ki_blk, wi_blk):
    nq = seq // tq
    return pl.pallas_call(
        functools.partial(_idx_prompt_kernel, tq=tq),
        grid=(batch, nq, nq),
        in_specs=[pl.BlockSpec((tq, H_IDX * D_IDX), lambda b, qb, kb: (b * nq + qb, 0)),
                  pl.BlockSpec((tq, D_IDX), lambda b, qb, kb: (b * nq + jnp.minimum(kb, qb), ki_blk)),
                  pl.BlockSpec((tq, LANES), lambda b, qb, kb: (b * nq + qb, wi_blk))],
        out_specs=pl.BlockSpec((tq, tq), lambda b, qb, kb: (b * nq + qb, kb)),
        out_shape=jax.ShapeDtypeStruct((batch * seq, seq), F32),
        compiler_params=_params("parallel", "parallel", "arbitrary"),
        name="indexer_prompt",
    )(qi, rest, rest)


def _split_dot_nt(e, x):
    hi = x.astype(BF16)
    lo = (x - hi.astype(F32)).astype(BF16)
    return (lax.dot_general(e, hi, NT_DIMS, preferred_element_type=F32) +
            lax.dot_general(e, lo, NT_DIMS, preferred_element_type=F32))


def _idx_sample_kernel(pt_ref, qit_ref, w_ref, e_ref, *refs, n_src, t):
    del pt_ref
    src_refs = refs[:n_src]
    new_ref, o_ref = refs[n_src], refs[n_src + 1]
    kb = pl.program_id(1)
    nk = pl.num_programs(1)
    tk = o_ref.shape[1]
    new_rows = new_ref.shape[0]

    def scores(keys):
        st = jnp.dot(keys.astype(BF16), qit_ref[...], preferred_element_type=F32)
        x = jnp.maximum(st, 0.0) * w_ref[...]
        return _split_dot_nt(e_ref[...], x)[:t]

    @pl.when(kb < nk - 1)
    def _():
        o_ref[...] = scores(jnp.concatenate([r[...] for r in src_refs], axis=0))

    @pl.when(kb == nk - 1)
    def _():
        sc = scores(new_ref[...])
        row = lax.broadcasted_iota(jnp.int32, (t, new_rows), 0)
        col = lax.broadcasted_iota(jnp.int32, (t, new_rows), 1)
        o_ref[...] = jnp.full(o_ref.shape, -jnp.inf, F32)
        o_ref[:, :new_rows] = jnp.where(col <= row, sc, -jnp.inf)


def indexer_sample(qi, wi, ki_new, pool_kidx, li, page_table, *, tk):
    bsz, t, _ = qi.shape
    n_src = tk // PAGE_SIZE
    nk = page_table.shape[1] // n_src
    qit = qi.reshape(bsz, t * H_IDX, D_IDX).transpose(0, 2, 1)
    w = (wi * IDX_SCALE).reshape(bsz, 1, t * H_IDX)
    e_rows = 2 * SUBLANES
    e = jnp.repeat(jnp.eye(e_rows, t, dtype=BF16), H_IDX, axis=1)
    src_specs = [pl.BlockSpec((None, None, PAGE_SIZE, D_IDX),
                              functools.partial(lambda p, b, kb, pt: (li, pt[b, jnp.minimum(kb, nk - 1) * n_src + p], 0, 0), p))
                 for p in range(n_src)]
    gs = pltpu.PrefetchScalarGridSpec(
        num_scalar_prefetch=1, grid=(bsz, nk + 1),
        in_specs=[pl.BlockSpec((None, D_IDX, t * H_IDX), lambda b, kb, pt: (b, 0, 0)),
                  pl.BlockSpec((None, 1, t * H_IDX), lambda b, kb, pt: (b, 0, 0)),
                  pl.BlockSpec((e_rows, t * H_IDX), lambda b, kb, pt: (0, 0))] + src_specs +
                 [pl.BlockSpec((None, ki_new.shape[1], D_IDX), lambda b, kb, pt: (b, 0, 0))],
        out_specs=pl.BlockSpec((None, t, tk), lambda b, kb, pt: (b, 0, kb)))
    return pl.pallas_call(
        functools.partial(_idx_sample_kernel, n_src=n_src, t=t),
        grid_spec=gs,
        out_shape=jax.ShapeDtypeStruct((bsz, t, (nk + 1) * tk), F32),
        compiler_params=_params("parallel", "arbitrary"),
        name="indexer_sample",
    )(page_table, qit, w, e, *([pool_kidx] * n_src), ki_new)


TOPK_CHUNK = 512
INT_MIN = -2 ** 31


def _topk_kernel(s_ref, u_ref, o_ref, *, k):
    x = s_ref[...]
    rows, n = x.shape
    x = jnp.where(x == 0.0, 0.0, x)
    bits = lax.bitcast_convert_type(x, jnp.int32)
    key = bits ^ ((bits >> 31) & jnp.int32(0x7FFFFFFF))
    kf = jnp.float32(k)

    def body(it, tau_u):
        cand_u = tau_u | jnp.left_shift(jnp.int32(1), 31 - it)
        cand = cand_u ^ jnp.int32(INT_MIN)
        cnt = jnp.sum(jnp.where(key >= cand, 1.0, 0.0), axis=1, keepdims=True)
        return jnp.where(cnt >= kf, cand_u, tau_u)

    tau_u = lax.fori_loop(0, 32, body, jnp.zeros((rows, 1), jnp.int32))
    tau = tau_u ^ jnp.int32(INT_MIN)
    gt = key > tau
    eq = key == tau
    need = kf - jnp.sum(jnp.where(gt, 1.0, 0.0), axis=1, keepdims=True)
    run = jnp.zeros((rows, 1), F32)
    for c in range(n // TOPK_CHUNK):
        cs = slice(c * TOPK_CHUNK, (c + 1) * TOPK_CHUNK)
        e = jnp.where(eq[:, cs], 1.0, 0.0)
        before = jnp.dot(e.astype(BF16), u_ref[...], preferred_element_type=F32) + run
        o_ref[:, cs] = jnp.where(gt[:, cs] | (eq[:, cs] & (before < need)), 1.0, 0.0)
        run = run + jnp.sum(e, axis=1, keepdims=True)


def topk_mask(scores, k, *, tr):
    r, s = scores.shape
    assert s % TOPK_CHUNK == 0 and r % tr == 0
    ii = jnp.arange(TOPK_CHUNK)
    upper = (ii[:, None] < ii[None, :]).astype(BF16)
    return pl.pallas_call(
        functools.partial(_topk_kernel, k=k),
        grid=(r // tr,),
        in_specs=[pl.BlockSpec((tr, s), lambda i: (i, 0)),
                  pl.BlockSpec((TOPK_CHUNK, TOPK_CHUNK), lambda i: (0, 0))],
        out_specs=pl.BlockSpec((tr, s), lambda i: (i, 0)),
        out_shape=jax.ShapeDtypeStruct((r, s), F32),
        compiler_params=_params("parallel"),
        name="topk_mask",
    )(scores, upper)


def _s5_param_kernel(lr_ref, li_ref, ls_ref, br_ref, bi_ref, abr_ref, abi_ref, bbr_ref, bbi_ref):
    lr, li = lr_ref[...], li_ref[...]
    step = jnp.exp(ls_ref[...])
    mag = jnp.exp(lr * step)
    ab_re, ab_im = mag * jnp.cos(li * step), mag * jnp.sin(li * step)
    den = lr * lr + li * li
    nr, ni = ab_re - 1.0, ab_im
    f_re, f_im = (nr * lr + ni * li) / den, (ni * lr - nr * li) / den
    br, bi = br_ref[...], bi_ref[...]
    abr_ref[...] = ab_re
    abi_ref[...] = ab_im
    bbr_ref[...] = f_re * br - f_im * bi
    bbi_ref[...] = f_re * bi + f_im * br


def s5_params(lam_re, lam_im, log_step, b_re, b_im):
    w = B_STATE * B_GROUP
    rep = lambda a: jnp.repeat(a, B_GROUP, axis=1)
    ls = jnp.broadcast_to(log_step[:, None], (B_GROUPS, w))
    shp = jax.ShapeDtypeStruct((B_GROUPS, w), F32)
    abr, abi, bbr, bbi = pl.pallas_call(_s5_param_kernel, out_shape=[shp] * 4, name="s5_params")(
        rep(lam_re), rep(lam_im), ls, b_re.reshape(B_GROUPS, w), b_im.reshape(B_GROUPS, w))
    return (abr[:, ::B_GROUP], abi[:, ::B_GROUP],
            bbr.reshape(B_GROUPS, B_STATE, B_GROUP), bbi.reshape(B_GROUPS, B_STATE, B_GROUP))


def _cmul(ar, ai, br, bi):
    return ar * br - ai * bi, ar * bi + ai * br


def _s5_kernel(u_ref, bbr_ref, bbi_ref, ccr_ref, cci_ref, ar_ref, ai_ref, d_ref, h0r_ref, h0i_ref,
               g_ref, hfr_ref, hfi_ref, xr_ref, xi_ref, *, seg_len, precise):
    u = u_ref[...]
    if precise:
        mm = lambda a, b: jnp.dot(a, b, preferred_element_type=F32, precision=lax.Precision.HIGHEST)
    else:
        mm = lambda a, b: jnp.dot(a.astype(BF16), b.astype(BF16), preferred_element_type=F32)
    xr_ref[...] = mm(u, bbr_ref[...])
    xi_ref[...] = mm(u, bbi_ref[...])
    sw = xr_ref.shape[1]
    ar = jnp.broadcast_to(ar_ref[...], (S5_SEG, sw))
    ai = jnp.broadcast_to(ai_ref[...], (S5_SEG, sw))

    def rows(j):
        return pl.ds(pl.multiple_of(j * S5_SEG, S5_SEG), S5_SEG)

    def local_scan(j, carry):
        hr, hi = _cmul(ar, ai, *carry)
        hr = hr + xr_ref[rows(j), :]
        hi = hi + xi_ref[rows(j), :]
        xr_ref[rows(j), :] = hr
        xi_ref[rows(j), :] = hi
        return hr, hi

    zero = jnp.zeros((S5_SEG, sw), F32)
    fr, fi = lax.fori_loop(0, seg_len, local_scan, (zero, zero))

    pr, pi = ar[:1], ai[:1]
    for _ in range(int(math.log2(seg_len))):
        pr, pi = _cmul(pr, pi, pr, pi)

    cr, ci = h0r_ref[...], h0i_ref[...]
    crs, cis = [], []
    for s in range(S5_SEG):
        crs.append(cr)
        cis.append(ci)
        tr, ti = _cmul(pr, pi, cr, ci)
        cr, ci = fr[s:s + 1] + tr, fi[s:s + 1] + ti
    hfr_ref[...] = cr
    hfi_ref[...] = ci
    c_re = jnp.concatenate(crs, axis=0)
    c_im = jnp.concatenate(cis, axis=0)

    def add_carry(j, pw):
        tr, ti = _cmul(pw[0], pw[1], c_re, c_im)
        xr_ref[rows(j), :] += tr
        xi_ref[rows(j), :] += ti
        return _cmul(pw[0], pw[1], ar, ai)

    lax.fori_loop(0, seg_len, add_carry, (ar, ai))

    y = mm(xr_ref[...], ccr_ref[...]) - mm(xi_ref[...], cci_ref[...]) + d_ref[...] * u
    g_ref[...] = jax.nn.gelu(y)


def s5_scan(u_perm, h0_re, h0_im, ab_re, ab_im, bb_re, bb_im, c_re, c_im, d_skip, *, batch, seq, precise):
    seg_len = seq // S5_SEG
    assert seg_len & (seg_len - 1) == 0
    nch = B_GROUPS // S5_CHUNK_GROUPS
    uw, sw = S5_CHUNK_GROUPS * B_GROUP, S5_CHUNK_GROUPS * B_STATE
    eye = jnp.eye(S5_CHUNK_GROUPS, dtype=F32)

    def in_blockdiag(bb):
        bb = bb.reshape(nch, S5_CHUNK_GROUPS, B_STATE, B_GROUP)
        return jnp.einsum("kgnc,gh->kgchn", bb, eye).reshape(nch, uw, sw)

    def out_blockdiag(cc):
        cc = cc.reshape(nch, S5_CHUNK_GROUPS, B_GROUP, B_STATE)
        return jnp.einsum("kgcn,gh->kgnhc", cc, eye).reshape(nch, sw, uw)

    row = lambda a: a.reshape(nch, 1, sw)
    st = lambda a: a.astype(F32).reshape(batch, nch, 1, sw)
    w_in = pl.BlockSpec((None, uw, sw), lambda b, c: (c, 0, 0))
    w_out = pl.BlockSpec((None, sw, uw), lambda b, c: (c, 0, 0))
    a_spec = pl.BlockSpec((None, 1, sw), lambda b, c: (c, 0, 0))
    st_spec = pl.BlockSpec((None, None, 1, sw), lambda b, c: (b, c, 0, 0))
    u_spec = pl.BlockSpec((seq, uw), lambda b, c: (b, c))
    g, hfr, hfi = pl.pallas_call(
        functools.partial(_s5_kernel, seg_len=seg_len, precise=precise),
        grid=(batch, nch),
        in_specs=[u_spec, w_in, w_in, w_out, w_out, a_spec, a_spec,
                  pl.BlockSpec((1, uw), lambda b, c: (0, c)), st_spec, st_spec],
        out_specs=[u_spec, st_spec, st_spec],
        out_shape=[jax.ShapeDtypeStruct((batch * seq, B_WIDTH), F32),
                   jax.ShapeDtypeStruct((batch, nch, 1, sw), F32),
                   jax.ShapeDtypeStruct((batch, nch, 1, sw), F32)],
        scratch_shapes=[pltpu.VMEM((seq, sw), F32), pltpu.VMEM((seq, sw), F32)],
        compiler_params=_params("parallel", "parallel"),
        name="s5_scan",
    )(u_perm, in_blockdiag(bb_re), in_blockdiag(bb_im), out_blockdiag(c_re), out_blockdiag(c_im),
      row(ab_re), row(ab_im), d_skip.reshape(1, B_WIDTH), st(h0_re), st(h0_im))
    return g, hfr.reshape(batch, B_GROUPS, B_STATE), hfi.reshape(batch, B_GROUPS, B_STATE)


def to_segment_order(a, batch, seq):
    seg_len = seq // S5_SEG
    return a.reshape(batch, S5_SEG, seg_len, -1).swapaxes(1, 2).reshape(batch * seq, -1)


def from_segment_order(a, batch, seq):
    seg_len = seq // S5_SEG
    return a.reshape(batch, seg_len, S5_SEG, -1).swapaxes(1, 2).reshape(batch * seq, -1)


def _merge_kernel(o0, o1, o2, l0, l1, l2, out_ref):
    la, lb, lc = l0[...], l1[...], l2[...]
    m = jnp.maximum(jnp.maximum(la, lb), lc)
    wa, wb, wc = jnp.exp(la - m), jnp.exp(lb - m), jnp.exp(lc - m)
    out = (wa * o0[...] + wb * o1[...] + wc * o2[...]) / (wa + wb + wc)
    out_ref[...] = out.astype(out_ref.dtype)


def merge_groups(outs, lses):
    m, w = outs[0].shape
    tm = min(m, 512)
    spec = pl.BlockSpec((tm, w), lambda i: (i, 0))
    return pl.pallas_call(
        _merge_kernel, grid=(m // tm,), in_specs=[spec] * 6, out_specs=spec,
        out_shape=jax.ShapeDtypeStruct((m, w), BF16), compiler_params=_params("parallel"),
        name="merge_groups",
    )(*outs, *lses)


DSA_TQ = 256
C_TQ = 128
SAMPLE_TK = PAGES_PER_STEP * PAGE_SIZE
NEW_ROWS = 128


def _pad_rows(a, rows):
    return jnp.pad(a, ((0, 0), (0, rows - a.shape[1]), (0, 0)))


def even_mixer(x, h, *, dims, li, w_in, w_rest, w_idx_qb, w_out, w_glu, b_glu, s5c, bias_table, h0_re, h0_im,
               pool_kv, pool_kidx, page_table):
    bp, tp, bs, ts = dims
    mp, ms = bp * tp, bs * ts
    ab_re, ab_im, bb_re, bb_im, c_re, c_im, d_skip = s5c

    q = linear(h, w_in, li, n_cols=Q_A_W, col_off=0, tn=512, out_dtypes=(BF16,))
    kv, kvb = linear(h, w_in, li, n_cols=KV_A_W, col_off=OFF_KV, tn=512, out_dtypes=(F32, BF16))
    cq = linear(h, w_in, li, n_cols=R_IDX, col_off=OFF_CQ, tn=512, out_dtypes=(BF16,))
    rest = linear(h, w_rest, 0, tn=256)
    u, ki, wi = rest[:, :B_WIDTH], rest[:, B_WIDTH:B_WIDTH + D_IDX], rest[:, B_WIDTH + D_IDX:B_WIDTH + D_IDX + H_IDX]
    qi = linear(cq, w_idx_qb, li, tn=512, out_dtypes=(BF16,))

    scores = indexer_prompt(qi, rest, batch=bp, seq=tp, tq=DSA_TQ,
                            ki_blk=B_WIDTH // LANES, wi_blk=B_WIDTH // LANES + 1)
    sel = topk_mask(scores, min(TOPK_MAX, tp // 4), tr=256)
    lib = toeplitz_bias_tiles(bias_table, tp // DSA_TQ, DSA_TQ, 1, tp, transposed=True)
    att_p = dsa_prompt_attention(q, kvb, lib, sel, batch=bp, seq=tp, tq=DSA_TQ)
    zeros = jnp.zeros((bp, B_GROUPS, B_STATE), F32)
    g_perm, hp_re, hp_im = s5_scan(to_segment_order(u[:mp], bp, tp), zeros, zeros, ab_re, ab_im, bb_re, bb_im,
                                   c_re, c_im, d_skip, batch=bp, seq=tp, precise=False)
    g_p = from_segment_order(g_perm, bp, tp)

    ki_new = _pad_rows(ki[mp:].reshape(bs, ts, D_IDX), NEW_ROWS)
    scores = indexer_sample(qi[mp:].reshape(bs, ts, H_IDX * D_IDX), wi[mp:].reshape(bs, ts, H_IDX),
                            ki_new, pool_kidx, li, page_table, tk=SAMPLE_TK)
    s_pad = scores.shape[-1]
    sel = topk_mask(scores.reshape(ms, s_pad), min(TOPK_MAX, (PAST_LEN + ts) // 4), tr=ms).reshape(bs, ts, s_pad)
    bias = sample_bias(bias_table, PAST_LEN, ts, s_pad, lambda d: d >= 0)
    kv_new = _pad_rows(kv[mp:].reshape(bs, ts, KV_A_W), NEW_ROWS)
    att_s = sample_attention(q[mp:].astype(F32).reshape(bs, ts, Q_A_W), pool_kv, li, kv_new, bias, sel,
                             hq=H_A, hkv=H_KV_A, page_table=page_table, tk=SAMPLE_TK, with_lse=False)
    g_s, hs_re, hs_im = s5_scan(u[mp:], h0_re, h0_im, ab_re, ab_im, bb_re, bb_im,
                                c_re, c_im, d_skip, batch=bs, seq=ts, precise=True)

    att = jnp.concatenate([att_p, att_s.reshape(ms, Q_A_W).astype(BF16)], axis=0)
    g = jnp.concatenate([g_p, g_s], axis=0)
    ssm = linear(g.astype(BF16), w_glu, li, tn=512, mode="glu", extras=(g, b_glu.reshape(1, B_WIDTH)),
                 out_dtypes=(BF16,))
    x = linear(att, w_out, li, row_off=0, tn=512, mode="resid", extras=(x,))
    x = linear(ssm, w_out, li, row_off=Q_A_W, tn=512, mode="resid", extras=(x,))
    return x, kv, ki, (hp_re, hp_im), (hs_re, hs_im)


def odd_mixer(x, h, li, w_in, w_out, bias_table, *, dims, caches):
    bp, tp, bs, ts = dims
    mp, ms = bp * tp, bs * ts
    q = linear(h, w_in, li, n_cols=C_WIDTH, col_off=0, tn=512)
    kv = linear(h, w_in, li, n_cols=2 * C_WIDTH, col_off=C_WIDTH, tn=512)
    kv5_s = kv[mp:].reshape(bs, ts, 2, H_C, HEAD_DIM)
    outs_p, lses_p, outs_s, lses_s, rows_p, rows_s = [], [], [], [], [], []
    for g, (window, dilation) in enumerate(C_PAIRS):
        cols = bias_table[:, g * H_C_G:(g + 1) * H_C_G]
        heads = slice(g * H_C_G, (g + 1) * H_C_G)
        lib = toeplitz_bias_tiles(cols, 2, C_TQ, dilation, window // dilation)
        o, l = dilated_prompt_group(q, kv, lib, batch=bp, seq=tp, g=g, dilation=dilation, tq=C_TQ)
        outs_p.append(o)
        lses_p.append(l)
        wrows = min(window, tp)
        tail = lambda c0: jnp.stack([kv[(b + 1) * tp - wrows:(b + 1) * tp, c0 + g * C_OUT:c0 + (g + 1) * C_OUT]
                                     for b in range(bp)]).reshape(bp, wrows, H_C_G, HEAD_DIM)
        rows_p.append(jnp.stack([tail(0), tail(C_WIDTH)], axis=2))

        buf = caches[g]
        wb = buf.shape[2]
        tk = min(wb, 512)
        s_pad = wb + tk
        bias = sample_bias(cols, wb, ts, s_pad,
                           lambda d: (d >= 0) & (d <= window) & (d % dilation == 0))
        kv_g = kv5_s[:, :, :, heads]
        new_kv = _pad_rows(kv_g.reshape(bs, ts, 2 * C_OUT), NEW_ROWS)
        qg = q[mp:, g * C_OUT:(g + 1) * C_OUT].astype(F32).reshape(bs, ts, C_OUT)
        o, l = sample_attention(qg, buf, li, new_kv, bias, None, hq=H_C_G, hkv=H_C_G, tk=tk, with_lse=True)
        outs_s.append(o.reshape(ms, C_OUT))
        lses_s.append(l.reshape(ms, C_OUT))
        rows_s.append(kv_g)
    att = jnp.concatenate([merge_groups(outs_p, lses_p), merge_groups(outs_s, lses_s)], axis=0)
    x = linear(att, w_out, li, tn=512, mode="resid", extras=(x,))
    return x, rows_p, rows_s


def kernel(x_prompt, x_sample, cache_a_kv, cache_a_kidx, state_b_re, state_b_im, cache_c0_kv, cache_c1_kv, cache_c2_kv, page_table, p_prompt, p_sample, bias_table, norm_g, final_g, ffn1_w1, ffn1_w3, ffn1_w2, ffn2_w1, ffn2_w3, ffn2_w2, ple_gate, ple_proj, w_in_even, w_idx_qb, w_out_even, s5_lam_re, s5_lam_im, s5_log_step, s5_b_re, s5_b_im, s5_c_re, s5_c_im, s5_d, s5_w_glu, s5_b_glu, w_in_odd, w_out_odd):
    bp, tp, _ = x_prompt.shape
    bs, ts, _ = x_sample.shape
    mp, ms = bp * tp, bs * ts
    dims = (bp, tp, bs, ts)
    x = jnp.concatenate([x_prompt.reshape(mp, D_MODEL), x_sample.reshape(ms, D_MODEL)], axis=0)
    p_all = jnp.concatenate([p_prompt.reshape(DEPTH, mp, D_PLE), p_sample.reshape(DEPTH, ms, D_PLE)], axis=1)
    outs_p = {k: [] for k in ("kv", "ki", "re", "im", "c0", "c1", "c2")}
    outs_s = {k: [] for k in ("kv", "ki", "re", "im", "c0", "c1", "c2")}
    for i in range(DEPTH):
        li = i // 2
        x = macaron_half(x, norm_g[i, 0], ffn1_w1, ffn1_w3, ffn1_w2, i)
        h = rmsnorm(x, norm_g[i, 1], BF16)
        if i % 2 == 0:
            w_in = w_in_even[li]
            w_rest = jnp.concatenate([w_in[:, OFF_U:], w_in[:, OFF_KI:OFF_U], w_in[:, OFF_W:OFF_KI],
                                      jnp.zeros((D_MODEL, LANES - H_IDX), w_in.dtype)], axis=1)[None]
            s5c = (*s5_params(s5_lam_re[li], s5_lam_im[li], s5_log_step[li], s5_b_re[li], s5_b_im[li]),
                   s5_c_re[li], s5_c_im[li], s5_d[li])
            x, kv, ki, st_p, st_s = even_mixer(
                x, h, dims=dims, li=li, w_in=w_in_even, w_rest=w_rest, w_idx_qb=w_idx_qb, w_out=w_out_even,
                w_glu=s5_w_glu, b_glu=s5_b_glu[li], s5c=s5c, bias_table=bias_table, h0_re=state_b_re[li],
                h0_im=state_b_im[li], pool_kv=cache_a_kv, pool_kidx=cache_a_kidx, page_table=page_table)
            outs_p["kv"].append(kv[:mp].reshape(bp, tp, 2, H_KV_A, HEAD_DIM))
            outs_p["ki"].append(ki[:mp].reshape(bp, tp, D_IDX))
            outs_p["re"].append(st_p[0])
            outs_p["im"].append(st_p[1])
            outs_s["kv"].append(kv[mp:].reshape(bs, ts, 2, H_KV_A, HEAD_DIM))
            outs_s["ki"].append(ki[mp:].reshape(bs, ts, D_IDX))
            outs_s["re"].append(st_s[0])
            outs_s["im"].append(st_s[1])
        else:
            x, rows_p, rows_s = odd_mixer(x, h, li, w_in_odd, w_out_odd, bias_table, dims=dims,
                                          caches=(cache_c0_kv, cache_c1_kv, cache_c2_kv))
            for g in range(N_C_GROUPS):
                outs_p["c%d" % g].append(rows_p[g])
                outs_s["c%d" % g].append(rows_s[g])
        x = macaron_half(x, norm_g[i, 2], ffn2_w1, ffn2_w3, ffn2_w2, i)
        x = ple_add(x, p_all[i], norm_g[i, 3], ple_gate, ple_proj, i)
    y_prompt = rmsnorm(x, final_g, F32, rows=mp).reshape(bp, tp, D_MODEL)
    y_sample = rmsnorm(x, final_g, F32, row_off=mp, rows=ms).reshape(bs, ts, D_MODEL)
    order = ("kv", "ki", "re", "im", "c0", "c1", "c2")
    return (y_prompt, y_sample, *[jnp.stack(outs_p[k]) for k in order], *[jnp.stack(outs_s[k]) for k in order])
```

```python
import functools
import math

import jax
import jax.numpy as jnp
from jax import lax
from jax.experimental import pallas as pl
from jax.experimental.pallas import tpu as pltpu

D_MODEL = 4096
DEPTH = 2
PAST_LEN = 16384
PAGE_SIZE = 128
HEAD_DIM = 128
EPS = 1e-6
H_A = 24
H_KV_A = 8
H_IDX = 32
D_IDX = 128
R_IDX = 512
TOPK_MAX = 256
B_WIDTH = 1024
B_GROUP = 16
B_GROUPS = 64
B_STATE = 64
C_PAIRS = ((128, 1), (512, 4), (2048, 16))
N_C_GROUPS = 3
H_C_G = 8
H_C = 24
C_WIDTH = H_C * HEAD_DIM
C_OUT = H_C_G * HEAD_DIM
NUM_BUCKETS = 32
MAX_DISTANCE = 2048
D_FF = 11008
D_PLE = 256
Q_A_W = H_A * HEAD_DIM
KV_A_W = 2 * H_KV_A * HEAD_DIM
OFF_KV = Q_A_W
OFF_CQ = OFF_KV + KV_A_W
OFF_W = OFF_CQ + R_IDX
OFF_KI = OFF_W + H_IDX
OFF_U = OFF_KI + D_IDX
IN_EVEN = OFF_U + B_WIDTH

LANES = 128
SUBLANES = 8
VMEM_LIMIT = 56 * 1024 * 1024

LINEAR_VMEM_BUDGET = 44 * 1024 * 1024
ROW_TILES = (1376, 1024, 512, 256, 128, 64, 16)

S5_SEG = SUBLANES
S5_CHUNK_GROUPS = 8
S5_UNROLL = 8
PAGES_PER_STEP = 8
NEG_INIT = -1e30

F32 = jnp.float32
BF16 = jnp.bfloat16
NT_DIMS = (((1,), (1,)), ((), ()))


def _params(*sem):
    return pltpu.CompilerParams(dimension_semantics=sem, vmem_limit_bytes=VMEM_LIMIT)


def _rms_kernel(x_ref, g_ref, o_ref):
    x = x_ref[...]
    ms = jnp.mean(x * x, axis=-1, keepdims=True)
    o_ref[...] = (x * lax.rsqrt(ms + EPS) * g_ref[...]).astype(o_ref.dtype)


def _row_tile(m, candidates):
    for c in candidates:
        if m % c == 0:
            return c
    return m


def rmsnorm(x, g, out_dtype, *, row_off=0, rows=None):
    d = x.shape[1]
    rows = x.shape[0] if rows is None else rows
    tm = _row_tile(math.gcd(rows, row_off) if row_off else rows, (688, 256, 64, 16))
    blk0 = row_off // tm
    return pl.pallas_call(
        _rms_kernel,
        grid=(rows // tm,),
        in_specs=[pl.BlockSpec((tm, d), lambda i: (i + blk0, 0)), pl.BlockSpec((1, d), lambda i: (0, 0))],
        out_specs=pl.BlockSpec((tm, d), lambda i: (i, 0)),
        out_shape=jax.ShapeDtypeStruct((rows, d), out_dtype),
        compiler_params=_params("parallel"),
        name="rmsnorm",
    )(x, g.reshape(1, d))


def _linear_kernel(*refs, mode, n_out, scale):
    a_ref, w_ref = refs[0], refs[1]
    extras = refs[2:len(refs) - n_out]
    outs = refs[len(refs) - n_out:]
    acc = jnp.dot(a_ref[...].astype(BF16), w_ref[...].astype(BF16), preferred_element_type=F32)
    if mode == "none":
        res = acc
    elif mode == "resid":
        res = extras[0][...] + scale * acc
    elif mode == "glu":
        g = extras[0][...]
        res = g * jax.nn.sigmoid(acc + extras[1][...])
    elif mode == "ple":
        x_ref, p_ref, wp_ref = extras
        proj = jnp.dot(p_ref[...].astype(BF16), wp_ref[...].astype(BF16), preferred_element_type=F32)
        res = x_ref[...] + jax.nn.sigmoid(acc) * proj
    else:
        raise ValueError(mode)
    for o in outs:
        o[...] = res.astype(o.dtype)


def linear(a, w, li, *, n_cols=None, col_off=0, row_off=0, tn, mode="none", extras=(),
           out_dtypes=(F32,), scale=1.0):
    m, k = a.shape
    n_cols = w.shape[2] if n_cols is None else n_cols
    tm = _row_tile(m, ROW_TILES)
    n_tiles = len(out_dtypes) + (0 if mode == "none" else 1)
    while (4 * tm * k + 10 * k * tn + 4 * tm * tn * (1 + 2 * n_tiles) > LINEAR_VMEM_BUDGET and tn > LANES
           and col_off % (tn // 2) == 0):
        tn //= 2
    assert m % tm == 0 and n_cols % tn == 0 and col_off % tn == 0 and row_off % k == 0
    jo, ro = col_off // tn, row_off // k
    tile = pl.BlockSpec((tm, tn), lambda i, j: (i, j))
    in_specs = [pl.BlockSpec((tm, k), lambda i, j: (i, 0)),
                pl.BlockSpec((None, k, tn), lambda i, j: (li, ro, j + jo))]
    if mode == "resid":
        in_specs += [tile]
    elif mode == "glu":
        in_specs += [tile, pl.BlockSpec((1, tn), lambda i, j: (0, j))]
    elif mode == "ple":
        kp = extras[1].shape[1]
        in_specs += [tile, pl.BlockSpec((tm, kp), lambda i, j: (i, 0)),
                     pl.BlockSpec((None, kp, tn), lambda i, j: (li, 0, j))]
    res = pl.pallas_call(
        functools.partial(_linear_kernel, mode=mode, n_out=len(out_dtypes), scale=scale),
        grid=(m // tm, n_cols // tn),
        in_specs=in_specs,
        out_specs=[tile] * len(out_dtypes),
        out_shape=[jax.ShapeDtypeStruct((m, n_cols), dt) for dt in out_dtypes],
        compiler_params=_params("parallel", "arbitrary"),
        name="linear_" + mode,
    )(a, w, *extras)
    return res if len(out_dtypes) > 1 else res[0]


FFN_TN = 256
DOWN_TK = 1024


def _swiglu_kernel(a_ref, w1_ref, w3_ref, o_ref, *, n_real):
    j = pl.program_id(1)

    @pl.when(j < n_real)
    def _():
        a = a_ref[...]
        g = jnp.dot(a, w1_ref[...].astype(BF16), preferred_element_type=F32)
        u = jnp.dot(a, w3_ref[...].astype(BF16), preferred_element_type=F32)
        o_ref[...] = (jax.nn.silu(g) * u).astype(o_ref.dtype)

    @pl.when(j >= n_real)
    def _():
        o_ref[...] = jnp.zeros(o_ref.shape, o_ref.dtype)


def swiglu_up(h, w1, w3, li, *, n_pad):
    m, k = h.shape
    n = w1.shape[2]
    tn = FFN_TN
    tm = _row_tile(m, ROW_TILES)
    n_real = n // tn
    w_spec = pl.BlockSpec((None, k, tn), lambda i, j: (li, 0, jnp.minimum(j, n_real - 1)))
    return pl.pallas_call(
        functools.partial(_swiglu_kernel, n_real=n_real),
        grid=(m // tm, n_pad // tn),
        in_specs=[pl.BlockSpec((tm, k), lambda i, j: (i, 0)), w_spec, w_spec],
        out_specs=pl.BlockSpec((tm, tn), lambda i, j: (i, j)),
        out_shape=jax.ShapeDtypeStruct((m, n_pad), BF16),
        compiler_params=_params("parallel", "arbitrary"),
        name="swiglu_up",
    )(h, w1, w3)


def _down_kernel(a_ref, w_ref, x_ref, o_ref, *, scale, tail_rows):
    kk = pl.program_id(2)
    last = pl.num_programs(2) - 1

    @pl.when(kk == 0)
    def _():
        o_ref[...] = x_ref[...]

    @pl.when(kk < last)
    def _():
        o_ref[...] += scale * jnp.dot(a_ref[...], w_ref[...].astype(BF16), preferred_element_type=F32)

    @pl.when(kk == last)
    def _():
        w = w_ref[...]
        row = lax.broadcasted_iota(jnp.int32, w.shape, 0)
        w = jnp.where(row < tail_rows, w, 0.0).astype(BF16)
        o_ref[...] += scale * jnp.dot(a_ref[...], w, preferred_element_type=F32)


def down_resid(act, w, li, x, *, scale):
    m, k_pad = act.shape
    k, n = w.shape[1:]
    tm = _row_tile(m, (2048,) + ROW_TILES)
    tn = min(n, 1024)
    tk = DOWN_TK
    assert k_pad % tk == 0 and k_pad - k < tk
    return pl.pallas_call(
        functools.partial(_down_kernel, scale=scale, tail_rows=tk - (k_pad - k)),
        grid=(m // tm, n // tn, k_pad // tk),
        in_specs=[pl.BlockSpec((tm, tk), lambda i, j, kk: (i, kk)),
                  pl.BlockSpec((None, tk, tn), lambda i, j, kk: (li, kk, j)),
                  pl.BlockSpec((tm, tn), lambda i, j, kk: (i, j))],
        out_specs=pl.BlockSpec((tm, tn), lambda i, j, kk: (i, j)),
        out_shape=jax.ShapeDtypeStruct((m, n), F32),
        compiler_params=_params("parallel", "parallel", "arbitrary"),
        name="down_resid",
    )(act, w, x)


def macaron_half(x, g, w1, w3, w2, li):
    h = rmsnorm(x, g, BF16)
    n_pad = -(-w1.shape[2] // DOWN_TK) * DOWN_TK
    act = swiglu_up(h, w1, w3, li, n_pad=n_pad)
    return down_resid(act, w2, li, x, scale=0.5)


def ple_add(x, p, g, w_gate, w_proj, li):
    h = rmsnorm(x, g, BF16)
    return linear(h, w_gate, li, tn=256, mode="ple", extras=(x, p.astype(BF16), w_proj))


def t5_bucket(dist):
    max_exact = NUM_BUCKETS // 2
    d = jnp.maximum(dist, 0)
    ratio = jnp.log(jnp.maximum(d, 1).astype(F32) / max_exact) / math.log(MAX_DISTANCE / max_exact)
    large = jnp.minimum(max_exact + (ratio * (NUM_BUCKETS - max_exact)).astype(jnp.int32), NUM_BUCKETS - 1)
    return jnp.where(d < max_exact, d, large)


def bias_of_dist(bias_cols, dist, valid):
    onehot = t5_bucket(dist)[..., None] == jnp.arange(NUM_BUCKETS)
    b = jnp.sum(jnp.where(onehot[..., None], bias_cols.astype(F32), 0.0), axis=-2)
    b = jnp.where(valid[..., None], b, -jnp.inf)
    return jnp.moveaxis(b, -1, 0)


def sample_bias(bias_cols, base, t, s_pad, valid_fn):
    n = jnp.arange(s_pad + t - 1)
    dist = base + (t - 1) - n
    table = bias_of_dist(bias_cols, dist, valid_fn(dist))
    return jnp.stack([table[:, t - 1 - i:t - 1 - i + s_pad] for i in range(t)], axis=1)


def _toeplitz_kernel(w_ref, o_ref):
    nd, tq, _ = o_ref.shape
    for d in range(nd):
        x = jnp.broadcast_to(w_ref[d], (tq, 2 * tq))
        r = pltpu.roll(x, 0, 1, stride=1, stride_axis=0)
        o_ref[d] = r[:, tq:]


def toeplitz_bias_tiles(bias_cols, n_delta, tq, step, max_steps, transposed=False):
    h = bias_cols.shape[1]
    lane = jnp.arange(2 * tq)[None, :]
    steps = jnp.arange(n_delta)[:, None] * tq + (lane - tq if transposed else tq - lane)
    rows = bias_of_dist(bias_cols, steps * step, (steps >= 0) & (steps <= max_steps))
    return pl.pallas_call(
        _toeplitz_kernel,
        grid=(h,),
        in_specs=[pl.BlockSpec((None, n_delta, 1, 2 * tq), lambda i: (i, 0, 0, 0))],
        out_specs=pl.BlockSpec((None, n_delta, tq, tq), lambda i: (i, 0, 0, 0)),
        out_shape=jax.ShapeDtypeStruct((h, n_delta, tq, tq), F32),
        compiler_params=_params("parallel"),
        name="toeplitz_bias_tiles",
    )(rows.reshape(h, n_delta, 1, 2 * tq))


def _flash_init(m_ref, l_ref, acc_ref):
    m_ref[...] = jnp.full(m_ref.shape, NEG_INIT, F32)
    l_ref[...] = jnp.zeros(l_ref.shape, F32)
    acc_ref[...] = jnp.zeros(acc_ref.shape, F32)


def _flash_step(q_ref, k_at, v_at, bias_at, sel, m_ref, l_ref, acc_ref, *, hq, hkv):
    rep = hq // hkv
    tq = q_ref.shape[0]
    scale = HEAD_DIM ** -0.5
    stack = lambda parts: parts[0] if rep == 1 else jnp.concatenate(parts, axis=0)
    sel_rows = None if sel is None else stack([sel] * rep)
    for g in range(hkv):
        heads = range(g * rep, (g + 1) * rep)
        cols = [slice(h * HEAD_DIM, (h + 1) * HEAD_DIM) for h in heads]
        qs = stack([q_ref[:, c] for c in cols]).astype(BF16)
        s = lax.dot_general(qs, k_at(g), NT_DIMS, preferred_element_type=F32) * scale
        s = s + stack([bias_at(h) for h in heads])
        if sel_rows is not None:
            s = jnp.where(sel_rows, s, -jnp.inf)
        gsl = slice(g * rep, (g + 1) * rep)
        m_old = m_ref[gsl].reshape(rep * tq, LANES)
        m_new = jnp.maximum(m_old, jnp.max(s, axis=1, keepdims=True))
        alpha = jnp.exp(m_old - m_new)
        p = jnp.exp(s - m_new[:, :1])
        l_new = alpha * l_ref[gsl].reshape(rep * tq, LANES) + jnp.sum(p, axis=1, keepdims=True)
        pv = jnp.dot(p.astype(BF16), v_at(g), preferred_element_type=F32)
        for r, c in enumerate(cols):
            rows = slice(r * tq, (r + 1) * tq)
            acc_ref[:, c] = alpha[rows] * acc_ref[:, c] + pv[rows]
        l_ref[gsl] = l_new.reshape(rep, tq, LANES)
        m_ref[gsl] = m_new.reshape(rep, tq, LANES)


def _flash_finish(o_ref, lse_ref, m_ref, l_ref, acc_ref, *, hq):
    for h in range(hq):
        hs = slice(h * HEAD_DIM, (h + 1) * HEAD_DIM)
        l = l_ref[h]
        o_ref[:, hs] = (acc_ref[:, hs] / l).astype(o_ref.dtype)
        if lse_ref is not None:
            lse_ref[:, hs] = m_ref[h] + jnp.log(l)


def _flash_scratch(tq, hq):
    return [pltpu.VMEM((hq, tq, LANES), F32), pltpu.VMEM((hq, tq, LANES), F32),
            pltpu.VMEM((tq, hq * HEAD_DIM), F32)]


TN_DIMS = (((0,), (0,)), ((), ()))


def _dsa_prompt_kernel(q_ref, k_ref, v_ref, bias_ref, sel_ref, o_ref, m_ref, l_ref, acc_ref):
    qb, kk = pl.program_id(1), pl.program_id(2)
    tq = q_ref.shape[0]
    rep = H_A // H_KV_A
    scale = HEAD_DIM ** -0.5

    @pl.when(kk == 0)
    def _():
        m_ref[...] = jnp.full(m_ref.shape, NEG_INIT, F32)
        l_ref[...] = jnp.zeros(l_ref.shape, F32)
        acc_ref[...] = jnp.zeros(acc_ref.shape, F32)

    @pl.when(kk <= qb)
    def _():
        sel_t = jnp.concatenate([sel_ref[...].T] * rep, axis=1) > 0.0
        for g in range(H_KV_A):
            heads = range(g * rep, (g + 1) * rep)
            gs = slice(g * HEAD_DIM, (g + 1) * HEAD_DIM)
            qs = jnp.concatenate([q_ref[:, h * HEAD_DIM:(h + 1) * HEAD_DIM] for h in heads], axis=0)
            s = lax.dot_general(k_ref[:, gs], qs, NT_DIMS, preferred_element_type=F32)
            s = s * scale + jnp.concatenate([bias_ref[h] for h in heads], axis=1)
            s = jnp.where(sel_t, s, -jnp.inf)
            m_old = m_ref[g]
            m_new = jnp.maximum(m_old, jnp.max(s, axis=0, keepdims=True))
            alpha = jnp.exp(m_old - m_new)
            p = jnp.exp(s - m_new)
            l_ref[g] = alpha * l_ref[g] + jnp.sum(p, axis=0, keepdims=True)
            pv = lax.dot_general(v_ref[:, gs], p.astype(BF16), TN_DIMS, preferred_element_type=F32)
            acc_ref[g] = alpha * acc_ref[g] + pv
            m_ref[g] = m_new

    @pl.when(kk == pl.num_programs(2) - 1)
    def _():
        for g in range(H_KV_A):
            out_t = acc_ref[g] / l_ref[g]
            for r in range(rep):
                h = g * rep + r
                o_ref[:, h * HEAD_DIM:(h + 1) * HEAD_DIM] = out_t[:, r * tq:(r + 1) * tq].T.astype(o_ref.dtype)


def dsa_prompt_attention(q, kvb, bias_lib_t, sel, *, batch, seq, tq):
    nq = seq // tq
    rep = H_A // H_KV_A
    kb_of = lambda qb, kk: jnp.minimum(kk, qb)
    kvw = H_KV_A * HEAD_DIM
    in_specs = [
        pl.BlockSpec((tq, Q_A_W), lambda b, qb, kk: (b * nq + qb, 0)),
        pl.BlockSpec((tq, kvw), lambda b, qb, kk: (b * nq + kb_of(qb, kk), 0)),
        pl.BlockSpec((tq, kvw), lambda b, qb, kk: (b * nq + kb_of(qb, kk), 1)),
        pl.BlockSpec((H_A, None, tq, tq), lambda b, qb, kk: (0, qb - kb_of(qb, kk), 0, 0)),
        pl.BlockSpec((tq, tq), lambda b, qb, kk: (b * nq + qb, kb_of(qb, kk))),
    ]
    return pl.pallas_call(
        _dsa_prompt_kernel,
        grid=(batch, nq, nq),
        in_specs=in_specs,
        out_specs=pl.BlockSpec((tq, Q_A_W), lambda b, qb, kk: (b * nq + qb, 0)),
        out_shape=jax.ShapeDtypeStruct((batch * seq, Q_A_W), BF16),
        scratch_shapes=[pltpu.VMEM((H_KV_A, 1, rep * tq), F32), pltpu.VMEM((H_KV_A, 1, rep * tq), F32),
                        pltpu.VMEM((H_KV_A, HEAD_DIM, rep * tq), F32)],
        compiler_params=_params("parallel", "arbitrary", "arbitrary"),
        name="dsa_prompt_attention",
    )(q, kvb, kvb, bias_lib_t, sel)


def _dilated_kernel(q_ref, kp_ref, kc_ref, vp_ref, vc_ref, bias_ref, o_ref, lse_ref, *, dilation, tq):
    scale = HEAD_DIM ** -0.5
    col = lax.broadcasted_iota(jnp.int32, (tq, 2 * tq), 1)
    usable = (col >= tq) | (pl.program_id(1) > 0)
    for h in range(q_ref.shape[1] // HEAD_DIM):
        hs = slice(h * HEAD_DIM, (h + 1) * HEAD_DIM)
        for r in range(dilation):
            rows = pl.ds(r, tq, stride=dilation) if dilation > 1 else slice(None)
            k = jnp.concatenate([kp_ref[rows, hs], kc_ref[rows, hs]], axis=0).astype(BF16)
            v = jnp.concatenate([vp_ref[rows, hs], vc_ref[rows, hs]], axis=0).astype(BF16)
            s = lax.dot_general(q_ref[rows, hs].astype(BF16), k, NT_DIMS, preferred_element_type=F32)
            s = jnp.where(usable, s * scale + bias_ref[h], -jnp.inf)
            m = jnp.max(s, axis=1, keepdims=True)
            p = jnp.exp(s - m)
            l = jnp.sum(p, axis=1, keepdims=True)
            o_ref[rows, hs] = jnp.dot(p.astype(BF16), v, preferred_element_type=F32) / l
            lse_ref[rows, hs] = jnp.broadcast_to(m + jnp.log(l), (tq, HEAD_DIM))


def dilated_prompt_group(q, kv, bias_lib, *, batch, seq, g, dilation, tq):
    span = tq * dilation
    nq = seq // span
    hps = H_C_G if dilation == 1 else 1
    cw = hps * HEAD_DIM
    hblocks = H_C_G // hps
    bias = jnp.concatenate([bias_lib[:, 1], bias_lib[:, 0]], axis=-1)
    prev = lambda qb: jnp.maximum(qb - 1, 0)
    blk = lambda f: pl.BlockSpec((span, cw), f)
    k0, v0 = g * hblocks, (H_C // hps) + g * hblocks
    in_specs = [
        blk(lambda b, qb, hb: (b * nq + qb, g * hblocks + hb)),
        blk(lambda b, qb, hb: (b * nq + prev(qb), k0 + hb)),
        blk(lambda b, qb, hb: (b * nq + qb, k0 + hb)),
        blk(lambda b, qb, hb: (b * nq + prev(qb), v0 + hb)),
        blk(lambda b, qb, hb: (b * nq + qb, v0 + hb)),
        pl.BlockSpec((hps, tq, 2 * tq), lambda b, qb, hb: (hb, 0, 0)),
    ]
    out_spec = blk(lambda b, qb, hb: (b * nq + qb, hb))
    return pl.pallas_call(
        functools.partial(_dilated_kernel, dilation=dilation, tq=tq),
        grid=(batch, nq, hblocks),
        in_specs=in_specs,
        out_specs=[out_spec, out_spec],
        out_shape=[jax.ShapeDtypeStruct((batch * seq, C_OUT), F32)] * 2,
        compiler_params=_params("parallel", "arbitrary", "arbitrary"),
        name="dilated_prompt_group",
    )(q, kv, kv, kv, kv, bias)


def _sample_attn_kernel(*refs, hq, hkv, n_src, has_sel, with_lse, has_table):
    if has_table:
        refs = refs[1:]
    q_ref = refs[0]
    src_refs = refs[1:1 + n_src]
    new_ref, bias_ref = refs[1 + n_src], refs[2 + n_src]
    pos = 3 + n_src
    sel_ref = None
    if has_sel:
        sel_ref = refs[pos]
        pos += 1
    o_ref = refs[pos]
    pos += 1
    lse_ref = None
    if with_lse:
        lse_ref = refs[pos]
        pos += 1
    m_ref, l_ref, acc_ref = refs[pos:pos + 3]
    kb = pl.program_id(1)
    nk = pl.num_programs(1)
    kw = hkv * HEAD_DIM
    new_rows = new_ref.shape[0]

    @pl.when(kb == 0)
    def _():
        _flash_init(m_ref, l_ref, acc_ref)

    def cached(which):
        def at(g):
            parts = [r[:, which, g, :] for r in src_refs]
            return (parts[0] if n_src == 1 else jnp.concatenate(parts, axis=0)).astype(BF16)
        return at

    def fresh(which):
        return lambda g: new_ref[:, which * kw + g * HEAD_DIM:which * kw + (g + 1) * HEAD_DIM].astype(BF16)

    @pl.when(kb < nk - 1)
    def _():
        sel = None if sel_ref is None else sel_ref[...] > 0.0
        _flash_step(q_ref, cached(0), cached(1), lambda h: bias_ref[h], sel,
                    m_ref, l_ref, acc_ref, hq=hq, hkv=hkv)

    @pl.when(kb == nk - 1)
    def _():
        sel = None if sel_ref is None else sel_ref[:, :new_rows] > 0.0
        _flash_step(q_ref, fresh(0), fresh(1), lambda h: bias_ref[h, :, :new_rows],
                    sel, m_ref, l_ref, acc_ref, hq=hq, hkv=hkv)
        _flash_finish(o_ref, lse_ref, m_ref, l_ref, acc_ref, hq=hq)


def sample_attention(q, src, li, new_kv, bias, sel, *, hq, hkv, page_table=None, tk, with_lse):
    bsz, t, _ = q.shape
    kvw = 2 * hkv * HEAD_DIM
    new_rows = new_kv.shape[1]
    paged = page_table is not None
    if paged:
        n_src = tk // PAGE_SIZE
        nk = page_table.shape[1] // n_src
        src_specs = [pl.BlockSpec((None, None, PAGE_SIZE, 2, hkv, HEAD_DIM),
                                  functools.partial(lambda p, b, kb, pt: (li, pt[b, jnp.minimum(kb, nk - 1) * n_src + p], 0, 0, 0, 0), p))
                     for p in range(n_src)]
        ix = lambda f: (lambda b, kb, pt: f(b, kb))
    else:
        n_src = 1
        nk = src.shape[2] // tk
        src_specs = [pl.BlockSpec((None, None, tk, 2, hkv, HEAD_DIM),
                                  lambda b, kb: (li, b, jnp.minimum(kb, nk - 1), 0, 0, 0))]
        ix = lambda f: f
    in_specs = ([pl.BlockSpec((None, t, hq * HEAD_DIM), ix(lambda b, kb: (b, 0, 0)))] + src_specs +
                [pl.BlockSpec((None, new_rows, kvw), ix(lambda b, kb: (b, 0, 0))),
                 pl.BlockSpec((hq, t, tk), ix(lambda b, kb: (0, 0, kb)))])
    args = [q] + [src] * n_src + [new_kv, bias]
    if sel is not None:
        in_specs.append(pl.BlockSpec((None, t, tk), ix(lambda b, kb: (b, 0, kb))))
        args.append(sel)
    out_spec = pl.BlockSpec((None, t, hq * HEAD_DIM), ix(lambda b, kb: (b, 0, 0)))
    n_out = 2 if with_lse else 1
    kern = functools.partial(_sample_attn_kernel, hq=hq, hkv=hkv, n_src=n_src, has_sel=sel is not None,
                             with_lse=with_lse, has_table=paged)
    out_shape = [jax.ShapeDtypeStruct((bsz, t, hq * HEAD_DIM), F32)] * n_out
    scratch = _flash_scratch(t, hq)
    cp = _params("parallel", "arbitrary")
    if paged:
        gs = pltpu.PrefetchScalarGridSpec(num_scalar_prefetch=1, grid=(bsz, nk + 1), in_specs=in_specs,
                                          out_specs=[out_spec] * n_out, scratch_shapes=scratch)
        res = pl.pallas_call(kern, grid_spec=gs, out_shape=out_shape, compiler_params=cp,
                             name="sample_attention_paged")(page_table, *args)
    else:
        res = pl.pallas_call(kern, grid=(bsz, nk + 1), in_specs=in_specs, out_specs=[out_spec] * n_out,
                             out_shape=out_shape, scratch_shapes=scratch, compiler_params=cp,
                             name="sample_attention")(*args)
    return res if with_lse else res[0]


IDX_SCALE = (H_IDX ** -0.5) * (D_IDX ** -0.5)


def _idx_prompt_kernel(qi_ref, ki_ref, wi_ref, o_ref, *, tq):
    qb, kb = pl.program_id(1), pl.program_id(2)

    @pl.when(kb > qb)
    def _():
        o_ref[...] = jnp.full(o_ref.shape, -jnp.inf, F32)

    @pl.when(kb <= qb)
    def _():
        kt = ki_ref[...].astype(BF16)
        w = wi_ref[...] * IDX_SCALE
        acc = jnp.zeros((tq, tq), F32)
        for h in range(H_IDX):
            s = lax.dot_general(qi_ref[:, h * D_IDX:(h + 1) * D_IDX], kt, NT_DIMS, preferred_element_type=F32)
            acc = acc + jnp.maximum(s, 0.0) * w[:, h:h + 1]
        row = lax.broadcasted_iota(jnp.int32, (tq, tq), 0) + qb * tq
        col = lax.broadcasted_iota(jnp.int32, (tq, tq), 1) + kb * tq
        o_ref[...] = jnp.where(col <= row, acc, -jnp.inf)


def indexer_prompt(qi, rest, *, batch, seq, tq, ki_blk, wi_blk):
    nq = seq // tq
    return pl.pallas_call(
        functools.partial(_idx_prompt_kernel, tq=tq),
        grid=(batch, nq, nq),
        in_specs=[pl.BlockSpec((tq, H_IDX * D_IDX), lambda b, qb, kb: (b * nq + qb, 0)),
                  pl.BlockSpec((tq, D_IDX), lambda b, qb, kb: (b * nq + jnp.minimum(kb, qb), ki_blk)),
                  pl.BlockSpec((tq, LANES), lambda b, qb, kb: (b * nq + qb, wi_blk))],
        out_specs=pl.BlockSpec((tq, tq), lambda b, qb, kb: (b * nq + qb, kb)),
        out_shape=jax.ShapeDtypeStruct((batch * seq, seq), F32),
        compiler_params=_params("parallel", "parallel", "arbitrary"),
        name="indexer_prompt",
    )(qi, rest, rest)


def _split_dot_nt(e, x):
    hi = x.astype(BF16)
    lo = (x - hi.astype(F32)).astype(BF16)
    return (lax.dot_general(e, hi, NT_DIMS, preferred_element_type=F32) +
            lax.dot_general(e, lo, NT_DIMS, preferred_element_type=F32))


def _idx_sample_kernel(pt_ref, qit_ref, w_ref, e_ref, *refs, n_src, t):
    del pt_ref
    src_refs = refs[:n_src]
    new_ref, o_ref = refs[n_src], refs[n_src + 1]
    kb = pl.program_id(1)
    nk = pl.num_programs(1)
    tk = o_ref.shape[1]
    new_rows = new_ref.shape[0]

    def scores(keys):
        st = jnp.dot(keys.astype(BF16), qit_ref[...], preferred_element_type=F32)
        x = jnp.maximum(st, 0.0) * w_ref[...]
        return _split_dot_nt(e_ref[...], x)[:t]

    @pl.when(kb < nk - 1)
    def _():
        o_ref[...] = scores(jnp.concatenate([r[...] for r in src_refs], axis=0))

    @pl.when(kb == nk - 1)
    def _():
        sc = scores(new_ref[...])
        row = lax.broadcasted_iota(jnp.int32, (t, new_rows), 0)
        col = lax.broadcasted_iota(jnp.int32, (t, new_rows), 1)
        o_ref[...] = jnp.full(o_ref.shape, -jnp.inf, F32)
        o_ref[:, :new_rows] = jnp.where(col <= row, sc, -jnp.inf)


def indexer_sample(qi, wi, ki_new, pool_kidx, li, page_table, *, tk):
    bsz, t, _ = qi.shape
    n_src = tk // PAGE_SIZE
    nk = page_table.shape[1] // n_src
    qit = qi.reshape(bsz, t * H_IDX, D_IDX).transpose(0, 2, 1)
    w = (wi * IDX_SCALE).reshape(bsz, 1, t * H_IDX)
    e_rows = 2 * SUBLANES
    e = jnp.repeat(jnp.eye(e_rows, t, dtype=BF16), H_IDX, axis=1)
    src_specs = [pl.BlockSpec((None, None, PAGE_SIZE, D_IDX),
                              functools.partial(lambda p, b, kb, pt: (li, pt[b, jnp.minimum(kb, nk - 1) * n_src + p], 0, 0), p))
                 for p in range(n_src)]
    gs = pltpu.PrefetchScalarGridSpec(
        num_scalar_prefetch=1, grid=(bsz, nk + 1),
        in_specs=[pl.BlockSpec((None, D_IDX, t * H_IDX), lambda b, kb, pt: (b, 0, 0)),
                  pl.BlockSpec((None, 1, t * H_IDX), lambda b, kb, pt: (b, 0, 0)),
                  pl.BlockSpec((e_rows, t * H_IDX), lambda b, kb, pt: (0, 0))] + src_specs +
                 [pl.BlockSpec((None, ki_new.shape[1], D_IDX), lambda b, kb, pt: (b, 0, 0))],
        out_specs=pl.BlockSpec((None, t, tk), lambda b, kb, pt: (b, 0, kb)))
    return pl.pallas_call(
        functools.partial(_idx_sample_kernel, n_src=n_src, t=t),
        grid_spec=gs,
        out_shape=jax.ShapeDtypeStruct((bsz, t, (nk + 1) * tk), F32),
        compiler_params=_params("parallel", "arbitrary"),
        name="indexer_sample",
    )(page_table, qit, w, e, *([pool_kidx] * n_src), ki_new)


TOPK_CHUNK = 512
INT_MIN = -2 ** 31


def _topk_kernel(s_ref, u_ref, o_ref, *, k, causal_tile):
    n = s_ref.shape[1]
    if causal_tile is None:
        _topk_rows(s_ref, u_ref, o_ref, k, n)
        return
    nb = n // causal_tile
    qb = pl.program_id(0) % nb
    widths = sorted({min(n, -(-(w * causal_tile) // TOPK_CHUNK) * TOPK_CHUNK) for w in range(1, nb + 1)})
    for width in widths:
        lo = 0 if width == widths[0] else widths[widths.index(width) - 1]

        @pl.when(((qb + 1) * causal_tile > lo) & ((qb + 1) * causal_tile <= width))
        def _(width=width):
            _topk_rows(s_ref, u_ref, o_ref, k, width)
            if width < n:
                o_ref[:, width:] = jnp.zeros((o_ref.shape[0], n - width), F32)


def _topk_rows(s_ref, u_ref, o_ref, k, n):
    assert n >= k
    x = s_ref[:, :n]
    rows = x.shape[0]
    bits = lax.bitcast_convert_type(x, jnp.int32)
    key = bits ^ ((bits >> 31) & jnp.int32(0x7FFFFFFF))
    kf = jnp.float32(k)

    def body(it, tau_u):
        cand_u = tau_u | jnp.left_shift(jnp.int32(1), 31 - it)
        cand = cand_u ^ jnp.int32(INT_MIN)
        cnt = jnp.sum(jnp.where(key >= cand, 1.0, 0.0), axis=1, keepdims=True)
        return jnp.where(cnt >= kf, cand_u, tau_u)

    tau_u = lax.fori_loop(0, 32, body, jnp.zeros((rows, 1), jnp.int32))
    tau = tau_u ^ jnp.int32(INT_MIN)
    gt = key > tau
    eq = key == tau
    need = kf - jnp.sum(jnp.where(gt, 1.0, 0.0), axis=1, keepdims=True)
    run = jnp.zeros((rows, 1), F32)
    for c in range(n // TOPK_CHUNK):
        cs = slice(c * TOPK_CHUNK, (c + 1) * TOPK_CHUNK)
        e = jnp.where(eq[:, cs], 1.0, 0.0)
        before = jnp.dot(e.astype(BF16), u_ref[...], preferred_element_type=F32) + run
        o_ref[:, cs] = jnp.where(gt[:, cs] | (eq[:, cs] & (before < need)), 1.0, 0.0)
        run = run + jnp.sum(e, axis=1, keepdims=True)


def topk_mask(scores, k, *, tr, causal=False):
    r, s = scores.shape
    assert s % TOPK_CHUNK == 0 and r % tr == 0 and (not causal or s % tr == 0)
    ii = jnp.arange(TOPK_CHUNK)
    upper = (ii[:, None] < ii[None, :]).astype(BF16)
    return pl.pallas_call(
        functools.partial(_topk_kernel, k=k, causal_tile=tr if causal else None),
        grid=(r // tr,),
        in_specs=[pl.BlockSpec((tr, s), lambda i: (i, 0)),
                  pl.BlockSpec((TOPK_CHUNK, TOPK_CHUNK), lambda i: (0, 0))],
        out_specs=pl.BlockSpec((tr, s), lambda i: (i, 0)),
        out_shape=jax.ShapeDtypeStruct((r, s), F32),
        compiler_params=_params("parallel"),
        name="topk_mask",
    )(scores, upper)


def _s5_param_kernel(lr_ref, li_ref, ls_ref, br_ref, bi_ref, abr_ref, abi_ref, bbr_ref, bbi_ref):
    lr, li = lr_ref[...], li_ref[...]
    step = jnp.exp(ls_ref[...])
    mag = jnp.exp(lr * step)
    ab_re, ab_im = mag * jnp.cos(li * step), mag * jnp.sin(li * step)
    den = lr * lr + li * li
    nr, ni = ab_re - 1.0, ab_im
    f_re, f_im = (nr * lr + ni * li) / den, (ni * lr - nr * li) / den
    br, bi = br_ref[...], bi_ref[...]
    abr_ref[...] = ab_re
    abi_ref[...] = ab_im
    bbr_ref[...] = f_re * br - f_im * bi
    bbi_ref[...] = f_re * bi + f_im * br


def s5_params(lam_re, lam_im, log_step, b_re, b_im):
    w = B_STATE * B_GROUP
    rep = lambda a: jnp.repeat(a, B_GROUP, axis=1)
    ls = jnp.broadcast_to(log_step[:, None], (B_GROUPS, w))
    shp = jax.ShapeDtypeStruct((B_GROUPS, w), F32)
    abr, abi, bbr, bbi = pl.pallas_call(_s5_param_kernel, out_shape=[shp] * 4, name="s5_params")(
        rep(lam_re), rep(lam_im), ls, b_re.reshape(B_GROUPS, w), b_im.reshape(B_GROUPS, w))
    return (abr[:, ::B_GROUP], abi[:, ::B_GROUP],
            bbr.reshape(B_GROUPS, B_STATE, B_GROUP), bbi.reshape(B_GROUPS, B_STATE, B_GROUP))


def _cmul(ar, ai, br, bi):
    return ar * br - ai * bi, ar * bi + ai * br


def _s5_kernel(u_ref, bbr_ref, bbi_ref, ccr_ref, cci_ref, ar_ref, ai_ref, d_ref, h0r_ref, h0i_ref,
               g_ref, hfr_ref, hfi_ref, xr_ref, xi_ref, *, seg_len, precise):
    u = u_ref[...]
    if precise:
        mm = lambda a, b: jnp.dot(a, b, preferred_element_type=F32, precision=lax.Precision.HIGHEST)
    else:
        mm = lambda a, b: jnp.dot(a.astype(BF16), b.astype(BF16), preferred_element_type=F32)
    xr_ref[...] = mm(u, bbr_ref[...])
    xi_ref[...] = mm(u, bbi_ref[...])
    sw = xr_ref.shape[1]
    ar = jnp.broadcast_to(ar_ref[...], (S5_SEG, sw))
    ai = jnp.broadcast_to(ai_ref[...], (S5_SEG, sw))

    def rows(j):
        return pl.ds(pl.multiple_of(j * S5_SEG, S5_SEG), S5_SEG)

    def local_scan(j, carry):
        hr, hi = _cmul(ar, ai, *carry)
        hr = hr + xr_ref[rows(j), :]
        hi = hi + xi_ref[rows(j), :]
        xr_ref[rows(j), :] = hr
        xi_ref[rows(j), :] = hi
        return hr, hi

    zero = jnp.zeros((S5_SEG, sw), F32)
    unroll = min(seg_len, S5_UNROLL)
    fr, fi = lax.fori_loop(0, seg_len, local_scan, (zero, zero), unroll=unroll)

    pr, pi = ar[:1], ai[:1]
    for _ in range(int(math.log2(seg_len))):
        pr, pi = _cmul(pr, pi, pr, pi)

    cr, ci = h0r_ref[...], h0i_ref[...]
    crs, cis = [], []
    for s in range(S5_SEG):
        crs.append(cr)
        cis.append(ci)
        tr, ti = _cmul(pr, pi, cr, ci)
        cr, ci = fr[s:s + 1] + tr, fi[s:s + 1] + ti
    hfr_ref[...] = cr
    hfi_ref[...] = ci
    c_re = jnp.concatenate(crs, axis=0)
    c_im = jnp.concatenate(cis, axis=0)

    def add_carry(j, pw):
        tr, ti = _cmul(pw[0], pw[1], c_re, c_im)
        xr_ref[rows(j), :] += tr
        xi_ref[rows(j), :] += ti
        return _cmul(pw[0], pw[1], ar, ai)

    lax.fori_loop(0, seg_len, add_carry, (ar, ai), unroll=unroll)

    y = mm(xr_ref[...], ccr_ref[...]) - mm(xi_ref[...], cci_ref[...]) + d_ref[...] * u
    g_ref[...] = jax.nn.gelu(y)


def s5_scan(u_perm, h0_re, h0_im, ab_re, ab_im, bb_re, bb_im, c_re, c_im, d_skip, *, batch, seq, precise):
    seg_len = seq // S5_SEG
    assert seg_len & (seg_len - 1) == 0
    nch = B_GROUPS // S5_CHUNK_GROUPS
    uw, sw = S5_CHUNK_GROUPS * B_GROUP, S5_CHUNK_GROUPS * B_STATE
    eye = jnp.eye(S5_CHUNK_GROUPS, dtype=F32)

    def in_blockdiag(bb):
        bb = bb.reshape(nch, S5_CHUNK_GROUPS, B_STATE, B_GROUP)
        return jnp.einsum("kgnc,gh->kgchn", bb, eye).reshape(nch, uw, sw)

    def out_blockdiag(cc):
        cc = cc.reshape(nch, S5_CHUNK_GROUPS, B_GROUP, B_STATE)
        return jnp.einsum("kgcn,gh->kgnhc", cc, eye).reshape(nch, sw, uw)

    row = lambda a: a.reshape(nch, 1, sw)
    st = lambda a: a.astype(F32).reshape(batch, nch, 1, sw)
    w_in = pl.BlockSpec((None, uw, sw), lambda b, c: (c, 0, 0))
    w_out = pl.BlockSpec((None, sw, uw), lambda b, c: (c, 0, 0))
    a_spec = pl.BlockSpec((None, 1, sw), lambda b, c: (c, 0, 0))
    st_spec = pl.BlockSpec((None, None, 1, sw), lambda b, c: (b, c, 0, 0))
    u_spec = pl.BlockSpec((seq, uw), lambda b, c: (b, c))
    g, hfr, hfi = pl.pallas_call(
        functools.partial(_s5_kernel, seg_len=seg_len, precise=precise),
        grid=(batch, nch),
        in_specs=[u_spec, w_in, w_in, w_out, w_out, a_spec, a_spec,
                  pl.BlockSpec((1, uw), lambda b, c: (0, c)), st_spec, st_spec],
        out_specs=[u_spec, st_spec, st_spec],
        out_shape=[jax.ShapeDtypeStruct((batch * seq, B_WIDTH), F32),
                   jax.ShapeDtypeStruct((batch, nch, 1, sw), F32),
                   jax.ShapeDtypeStruct((batch, nch, 1, sw), F32)],
        scratch_shapes=[pltpu.VMEM((seq, sw), F32), pltpu.VMEM((seq, sw), F32)],
        compiler_params=_params("parallel", "parallel"),
        name="s5_scan",
    )(u_perm, in_blockdiag(bb_re), in_blockdiag(bb_im), out_blockdiag(c_re), out_blockdiag(c_im),
      row(ab_re), row(ab_im), d_skip.reshape(1, B_WIDTH), st(h0_re), st(h0_im))
    return g, hfr.reshape(batch, B_GROUPS, B_STATE), hfi.reshape(batch, B_GROUPS, B_STATE)


def to_segment_order(a, batch, seq):
    seg_len = seq // S5_SEG
    return a.reshape(batch, S5_SEG, seg_len, -1).swapaxes(1, 2).reshape(batch * seq, -1)


def from_segment_order(a, batch, seq):
    seg_len = seq // S5_SEG
    return a.reshape(batch, seg_len, S5_SEG, -1).swapaxes(1, 2).reshape(batch * seq, -1)


def _merge_kernel(o0, o1, o2, l0, l1, l2, out_ref):
    la, lb, lc = l0[...], l1[...], l2[...]
    m = jnp.maximum(jnp.maximum(la, lb), lc)
    wa, wb, wc = jnp.exp(la - m), jnp.exp(lb - m), jnp.exp(lc - m)
    out = (wa * o0[...] + wb * o1[...] + wc * o2[...]) / (wa + wb + wc)
    out_ref[...] = out.astype(out_ref.dtype)


def merge_groups(outs, lses):
    m, w = outs[0].shape
    tm = min(m, 512)
    spec = pl.BlockSpec((tm, w), lambda i: (i, 0))
    return pl.pallas_call(
        _merge_kernel, grid=(m // tm,), in_specs=[spec] * 6, out_specs=spec,
        out_shape=jax.ShapeDtypeStruct((m, w), BF16), compiler_params=_params("parallel"),
        name="merge_groups",
    )(*outs, *lses)


DSA_TQ = 256
C_TQ = 128
SAMPLE_TK = PAGES_PER_STEP * PAGE_SIZE
NEW_ROWS = 128


def _pad_rows(a, rows):
    return jnp.pad(a, ((0, 0), (0, rows - a.shape[1]), (0, 0)))


def even_mixer(x, h, *, dims, li, w_in, w_rest, w_idx_qb, w_out, w_glu, b_glu, s5c, bias_table, h0_re, h0_im,
               pool_kv, pool_kidx, page_table):
    bp, tp, bs, ts = dims
    mp, ms = bp * tp, bs * ts
    ab_re, ab_im, bb_re, bb_im, c_re, c_im, d_skip = s5c

    q = linear(h, w_in, li, n_cols=Q_A_W, col_off=0, tn=512, out_dtypes=(BF16,))
    kv, kvb = linear(h, w_in, li, n_cols=KV_A_W, col_off=OFF_KV, tn=512, out_dtypes=(F32, BF16))
    cq = linear(h, w_in, li, n_cols=R_IDX, col_off=OFF_CQ, tn=512, out_dtypes=(BF16,))
    rest = linear(h, w_rest, 0, tn=256)
    u, ki, wi = rest[:, :B_WIDTH], rest[:, B_WIDTH:B_WIDTH + D_IDX], rest[:, B_WIDTH + D_IDX:B_WIDTH + D_IDX + H_IDX]
    qi = linear(cq, w_idx_qb, li, tn=512, out_dtypes=(BF16,))

    scores = indexer_prompt(qi, rest, batch=bp, seq=tp, tq=DSA_TQ,
                            ki_blk=B_WIDTH // LANES, wi_blk=B_WIDTH // LANES + 1)
    sel = topk_mask(scores, min(TOPK_MAX, tp // 4), tr=DSA_TQ, causal=True)
    lib = toeplitz_bias_tiles(bias_table, tp // DSA_TQ, DSA_TQ, 1, tp, transposed=True)
    att_p = dsa_prompt_attention(q, kvb, lib, sel, batch=bp, seq=tp, tq=DSA_TQ)
    zeros = jnp.zeros((bp, B_GROUPS, B_STATE), F32)
    g_perm, hp_re, hp_im = s5_scan(to_segment_order(u[:mp], bp, tp), zeros, zeros, ab_re, ab_im, bb_re, bb_im,
                                   c_re, c_im, d_skip, batch=bp, seq=tp, precise=False)
    g_p = from_segment_order(g_perm, bp, tp)

    ki_new = _pad_rows(ki[mp:].reshape(bs, ts, D_IDX), NEW_ROWS)
    scores = indexer_sample(qi[mp:].reshape(bs, ts, H_IDX * D_IDX), wi[mp:].reshape(bs, ts, H_IDX),
                            ki_new, pool_kidx, li, page_table, tk=SAMPLE_TK)
    s_pad = scores.shape[-1]
    sel = topk_mask(scores.reshape(ms, s_pad), min(TOPK_MAX, (PAST_LEN + ts) // 4), tr=ms).reshape(bs, ts, s_pad)
    bias = sample_bias(bias_table, PAST_LEN, ts, s_pad, lambda d: d >= 0)
    kv_new = _pad_rows(kv[mp:].reshape(bs, ts, KV_A_W), NEW_ROWS)
    att_s = sample_attention(q[mp:].astype(F32).reshape(bs, ts, Q_A_W), pool_kv, li, kv_new, bias, sel,
                             hq=H_A, hkv=H_KV_A, page_table=page_table, tk=SAMPLE_TK, with_lse=False)
    g_s, hs_re, hs_im = s5_scan(u[mp:], h0_re, h0_im, ab_re, ab_im, bb_re, bb_im,
                                c_re, c_im, d_skip, batch=bs, seq=ts, precise=True)

    att = jnp.concatenate([att_p, att_s.reshape(ms, Q_A_W).astype(BF16)], axis=0)
    g = jnp.concatenate([g_p, g_s], axis=0)
    ssm = linear(g.astype(BF16), w_glu, li, tn=512, mode="glu", extras=(g, b_glu.reshape(1, B_WIDTH)),
                 out_dtypes=(BF16,))
    x = linear(att, w_out, li, row_off=0, tn=512, mode="resid", extras=(x,))
    x = linear(ssm, w_out, li, row_off=Q_A_W, tn=512, mode="resid", extras=(x,))
    return x, kv, ki, (hp_re, hp_im), (hs_re, hs_im)


def odd_mixer(x, h, li, w_in, w_out, bias_table, *, dims, caches):
    bp, tp, bs, ts = dims
    mp, ms = bp * tp, bs * ts
    q = linear(h, w_in, li, n_cols=C_WIDTH, col_off=0, tn=512)
    kv = linear(h, w_in, li, n_cols=2 * C_WIDTH, col_off=C_WIDTH, tn=512)
    kv5_s = kv[mp:].reshape(bs, ts, 2, H_C, HEAD_DIM)
    outs_p, lses_p, outs_s, lses_s, rows_p, rows_s = [], [], [], [], [], []
    for g, (window, dilation) in enumerate(C_PAIRS):
        cols = bias_table[:, g * H_C_G:(g + 1) * H_C_G]
        heads = slice(g * H_C_G, (g + 1) * H_C_G)
        lib = toeplitz_bias_tiles(cols, 2, C_TQ, dilation, window // dilation)
        o, l = dilated_prompt_group(q, kv, lib, batch=bp, seq=tp, g=g, dilation=dilation, tq=C_TQ)
        outs_p.append(o)
        lses_p.append(l)
        wrows = min(window, tp)
        tail = lambda c0: jnp.stack([kv[(b + 1) * tp - wrows:(b + 1) * tp, c0 + g * C_OUT:c0 + (g + 1) * C_OUT]
                                     for b in range(bp)]).reshape(bp, wrows, H_C_G, HEAD_DIM)
        rows_p.append(jnp.stack([tail(0), tail(C_WIDTH)], axis=2))

        buf = caches[g]
        wb = buf.shape[2]
        tk = min(wb, 512)
        s_pad = wb + tk
        bias = sample_bias(cols, wb, ts, s_pad,
                           lambda d: (d >= 0) & (d <= window) & (d % dilation == 0))
        kv_g = kv5_s[:, :, :, heads]
        new_kv = _pad_rows(kv_g.reshape(bs, ts, 2 * C_OUT), NEW_ROWS)
        qg = q[mp:, g * C_OUT:(g + 1) * C_OUT].astype(F32).reshape(bs, ts, C_OUT)
        o, l = sample_attention(qg, buf, li, new_kv, bias, None, hq=H_C_G, hkv=H_C_G, tk=tk, with_lse=True)
        outs_s.append(o.reshape(ms, C_OUT))
        lses_s.append(l.reshape(ms, C_OUT))
        rows_s.append(kv_g)
    att = jnp.concatenate([merge_groups(outs_p, lses_p), merge_groups(outs_s, lses_s)], axis=0)
    x = linear(att, w_out, li, tn=512, mode="resid", extras=(x,))
    return x, rows_p, rows_s


def kernel(x_prompt, x_sample, cache_a_kv, cache_a_kidx, state_b_re, state_b_im, cache_c0_kv, cache_c1_kv, cache_c2_kv, page_table, p_prompt, p_sample, bias_table, norm_g, final_g, ffn1_w1, ffn1_w3, ffn1_w2, ffn2_w1, ffn2_w3, ffn2_w2, ple_gate, ple_proj, w_in_even, w_idx_qb, w_out_even, s5_lam_re, s5_lam_im, s5_log_step, s5_b_re, s5_b_im, s5_c_re, s5_c_im, s5_d, s5_w_glu, s5_b_glu, w_in_odd, w_out_odd):
    bp, tp, _ = x_prompt.shape
    bs, ts, _ = x_sample.shape
    mp, ms = bp * tp, bs * ts
    dims = (bp, tp, bs, ts)
    x = jnp.concatenate([x_prompt.reshape(mp, D_MODEL), x_sample.reshape(ms, D_MODEL)], axis=0)
    p_all = jnp.concatenate([p_prompt.reshape(DEPTH, mp, D_PLE), p_sample.reshape(DEPTH, ms, D_PLE)], axis=1)
    outs_p = {k: [] for k in ("kv", "ki", "re", "im", "c0", "c1", "c2")}
    outs_s = {k: [] for k in ("kv", "ki", "re", "im", "c0", "c1", "c2")}
    for i in range(DEPTH):
        li = i // 2
        x = macaron_half(x, norm_g[i, 0], ffn1_w1, ffn1_w3, ffn1_w2, i)
        h = rmsnorm(x, norm_g[i, 1], BF16)
        if i % 2 == 0:
            w_in = w_in_even[li]
            w_rest = jnp.concatenate([w_in[:, OFF_U:], w_in[:, OFF_KI:OFF_U], w_in[:, OFF_W:OFF_KI],
                                      jnp.zeros((D_MODEL, LANES - H_IDX), w_in.dtype)], axis=1)[None]
            s5c = (*s5_params(s5_lam_re[li], s5_lam_im[li], s5_log_step[li], s5_b_re[li], s5_b_im[li]),
                   s5_c_re[li], s5_c_im[li], s5_d[li])
            x, kv, ki, st_p, st_s = even_mixer(
                x, h, dims=dims, li=li, w_in=w_in_even, w_rest=w_rest, w_idx_qb=w_idx_qb, w_out=w_out_even,
                w_glu=s5_w_glu, b_glu=s5_b_glu[li], s5c=s5c, bias_table=bias_table, h0_re=state_b_re[li],
                h0_im=state_b_im[li], pool_kv=cache_a_kv, pool_kidx=cache_a_kidx, page_table=page_table)
            outs_p["kv"].append(kv[:mp].reshape(bp, tp, 2, H_KV_A, HEAD_DIM))
            outs_p["ki"].append(ki[:mp].reshape(bp, tp, D_IDX))
            outs_p["re"].append(st_p[0])
            outs_p["im"].append(st_p[1])
            outs_s["kv"].append(kv[mp:].reshape(bs, ts, 2, H_KV_A, HEAD_DIM))
            outs_s["ki"].append(ki[mp:].reshape(bs, ts, D_IDX))
            outs_s["re"].append(st_s[0])
            outs_s["im"].append(st_s[1])
        else:
            x, rows_p, rows_s = odd_mixer(x, h, li, w_in_odd, w_out_odd, bias_table, dims=dims,
                                          caches=(cache_c0_kv, cache_c1_kv, cache_c2_kv))
            for g in range(N_C_GROUPS):
                outs_p["c%d" % g].append(rows_p[g])
                outs_s["c%d" % g].append(rows_s[g])
        x = macaron_half(x, norm_g[i, 2], ffn2_w1, ffn2_w3, ffn2_w2, i)
        x = ple_add(x, p_all[i], norm_g[i, 3], ple_gate, ple_proj, i)
    y_prompt = rmsnorm(x, final_g, F32, rows=mp).reshape(bp, tp, D_MODEL)
    y_sample = rmsnorm(x, final_g, F32, row_off=mp, rows=ms).reshape(bs, ts, D_MODEL)
    order = ("kv", "ki", "re", "im", "c0", "c1", "c2")
    return (y_prompt, y_sample, *[jnp.stack(outs_p[k]) for k in order], *[jnp.stack(outs_s[k]) for k in order])
```

```python
import functools
import math

import jax
import jax.numpy as jnp
from jax import lax
from jax.experimental import pallas as pl
from jax.experimental.pallas import tpu as pltpu

D_MODEL = 4096
DEPTH = 2
PAST_LEN = 16384
PAGE_SIZE = 128
HEAD_DIM = 128
EPS = 1e-6
H_A = 24
H_KV_A = 8
H_IDX = 32
D_IDX = 128
R_IDX = 512
TOPK_MAX = 256
B_WIDTH = 1024
B_GROUP = 16
B_GROUPS = 64
B_STATE = 64
C_PAIRS = ((128, 1), (512, 4), (2048, 16))
N_C_GROUPS = 3
H_C_G = 8
H_C = 24
C_WIDTH = H_C * HEAD_DIM
C_OUT = H_C_G * HEAD_DIM
NUM_BUCKETS = 32
MAX_DISTANCE = 2048
D_FF = 11008
D_PLE = 256
Q_A_W = H_A * HEAD_DIM
KV_A_W = 2 * H_KV_A * HEAD_DIM
OFF_KV = Q_A_W
OFF_CQ = OFF_KV + KV_A_W
OFF_W = OFF_CQ + R_IDX
OFF_KI = OFF_W + H_IDX
OFF_U = OFF_KI + D_IDX
IN_EVEN = OFF_U + B_WIDTH

LANES = 128
SUBLANES = 8
VMEM_LIMIT = 56 * 1024 * 1024

LINEAR_VMEM_BUDGET = 44 * 1024 * 1024
ROW_TILES = (1376, 1024, 512, 256, 128, 64, 16)

S5_SEG = SUBLANES
S5_CHUNK_GROUPS = 8
S5_UNROLL = 8
PAGES_PER_STEP = 8
NEG_INIT = -1e30

F32 = jnp.float32
BF16 = jnp.bfloat16
NT_DIMS = (((1,), (1,)), ((), ()))


def _params(*sem):
    return pltpu.CompilerParams(dimension_semantics=sem, vmem_limit_bytes=VMEM_LIMIT)


def _rms_kernel(x_ref, g_ref, o_ref):
    x = x_ref[...]
    ms = jnp.mean(x * x, axis=-1, keepdims=True)
    o_ref[...] = (x * lax.rsqrt(ms + EPS) * g_ref[...]).astype(o_ref.dtype)


def _row_tile(m, candidates):
    for c in candidates:
        if m % c == 0:
            return c
    return m


def rmsnorm(x, g, out_dtype, *, row_off=0, rows=None):
    d = x.shape[1]
    rows = x.shape[0] if rows is None else rows
    tm = _row_tile(math.gcd(rows, row_off) if row_off else rows, (688, 256, 64, 16))
    blk0 = row_off // tm
    return pl.pallas_call(
        _rms_kernel,
        grid=(rows // tm,),
        in_specs=[pl.BlockSpec((tm, d), lambda i: (i + blk0, 0)), pl.BlockSpec((1, d), lambda i: (0, 0))],
        out_specs=pl.BlockSpec((tm, d), lambda i: (i, 0)),
        out_shape=jax.ShapeDtypeStruct((rows, d), out_dtype),
        compiler_params=_params("parallel"),
        name="rmsnorm",
    )(x, g.reshape(1, d))


def _row_scale(parts_ref, width):
    ssq = jnp.sum(parts_ref[...], axis=1, keepdims=True) * (1.0 / LANES)
    return lax.rsqrt(ssq * (1.0 / width) + EPS)


def _emit_norm_inputs(res, g_ref, xg_ref, part_ref):
    xg_ref[...] = (res * g_ref[...]).astype(xg_ref.dtype)
    ssq = jnp.broadcast_to(jnp.sum(res * res, axis=1, keepdims=True), part_ref.shape)
    first = pl.program_id(1) == 0

    @pl.when(first)
    def _():
        part_ref[...] = ssq

    @pl.when(jnp.logical_not(first))
    def _():
        part_ref[...] += ssq


def _linear_kernel(*refs, mode, n_out, scale, norm_width, emit_norm):
    a_ref, w_ref = refs[0], refs[1]
    n_tail = n_out + (2 if emit_norm else 0)
    first = 3 if norm_width else 2
    extras = refs[first:len(refs) - n_tail]
    outs = refs[len(refs) - n_tail:len(refs) - n_tail + n_out]
    acc = jnp.dot(a_ref[...].astype(BF16), w_ref[...].astype(BF16), preferred_element_type=F32)
    if norm_width:
        acc = acc * _row_scale(refs[2], norm_width)
    if emit_norm:
        g_next = extras[-1]
        extras = extras[:-1]
    if mode == "none":
        res = acc
    elif mode == "resid":
        res = extras[0][...] + scale * acc
    elif mode == "glu":
        g = extras[0][...]
        res = g * jax.nn.sigmoid(acc + extras[1][...])
    elif mode == "ple":
        x_ref, p_ref, wp_ref = extras
        proj = jnp.dot(p_ref[...].astype(BF16), wp_ref[...].astype(BF16), preferred_element_type=F32)
        res = x_ref[...] + jax.nn.sigmoid(acc) * proj
    else:
        raise ValueError(mode)
    for o in outs:
        o[...] = res.astype(o.dtype)
    if emit_norm:
        _emit_norm_inputs(res, g_next, refs[-2], refs[-1])


def linear(a, w, li, *, n_cols=None, col_off=0, row_off=0, tn, mode="none", extras=(),
           out_dtypes=(F32,), scale=1.0, row_ssq=None, next_g=None):
    m, k = a.shape
    n_cols = w.shape[2] if n_cols is None else n_cols
    tm = _row_tile(m, ROW_TILES)
    n_tiles = len(out_dtypes) + (0 if mode == "none" else 1) + (0 if next_g is None else 1)
    while (4 * tm * k + 10 * k * tn + 4 * tm * tn * (1 + 2 * n_tiles) > LINEAR_VMEM_BUDGET and tn > LANES
           and col_off % (tn // 2) == 0):
        tn //= 2
    assert m % tm == 0 and n_cols % tn == 0 and col_off % tn == 0 and row_off % k == 0
    jo, ro = col_off // tn, row_off // k
    tile = pl.BlockSpec((tm, tn), lambda i, j: (i, j))
    in_specs = [pl.BlockSpec((tm, k), lambda i, j: (i, 0)),
                pl.BlockSpec((None, k, tn), lambda i, j: (li, ro, j + jo))]
    args = [a, w]
    if row_ssq is not None:
        in_specs.append(pl.BlockSpec((tm, row_ssq.shape[1]), lambda i, j: (i, 0)))
        args.append(row_ssq)
    if mode == "resid":
        in_specs += [tile]
    elif mode == "glu":
        in_specs += [tile, pl.BlockSpec((1, tn), lambda i, j: (0, j))]
    elif mode == "ple":
        kp = extras[1].shape[1]
        in_specs += [tile, pl.BlockSpec((tm, kp), lambda i, j: (i, 0)),
                     pl.BlockSpec((None, kp, tn), lambda i, j: (li, 0, j))]
    args += list(extras)
    out_specs = [tile] * len(out_dtypes)
    out_shape = [jax.ShapeDtypeStruct((m, n_cols), dt) for dt in out_dtypes]
    if next_g is not None:
        in_specs.append(pl.BlockSpec((1, tn), lambda i, j: (0, j)))
        args.append(next_g.reshape(1, n_cols))
        out_specs += [tile, pl.BlockSpec((tm, LANES), lambda i, j: (i, 0))]
        out_shape += [jax.ShapeDtypeStruct((m, n_cols), BF16), jax.ShapeDtypeStruct((m, LANES), F32)]
    res = pl.pallas_call(
        functools.partial(_linear_kernel, mode=mode, n_out=len(out_dtypes), scale=scale,
                          norm_width=k if row_ssq is not None else 0, emit_norm=next_g is not None),
        grid=(m // tm, n_cols // tn),
        in_specs=in_specs,
        out_specs=out_specs,
        out_shape=out_shape,
        compiler_params=_params("parallel", "arbitrary"),
        name="linear_" + mode,
    )(*args)
    return res if len(res) > 1 else res[0]


FFN_TN = 256
DOWN_TK = 1024


def _swiglu_kernel(a_ref, w1_ref, w3_ref, *refs, n_real, norm_width):
    o_ref = refs[-1]
    j = pl.program_id(1)

    @pl.when(j < n_real)
    def _():
        a = a_ref[...]
        g = jnp.dot(a, w1_ref[...].astype(BF16), preferred_element_type=F32)
        u = jnp.dot(a, w3_ref[...].astype(BF16), preferred_element_type=F32)
        if norm_width:
            r = _row_scale(refs[0], norm_width)
            g, u = g * r, u * r
        o_ref[...] = (jax.nn.silu(g) * u).astype(o_ref.dtype)

    @pl.when(j >= n_real)
    def _():
        o_ref[...] = jnp.zeros(o_ref.shape, o_ref.dtype)


def swiglu_up(h, w1, w3, li, *, n_pad, row_ssq=None):
    m, k = h.shape
    n = w1.shape[2]
    tn = FFN_TN
    tm = _row_tile(m, ROW_TILES)
    n_real = n // tn
    w_spec = pl.BlockSpec((None, k, tn), lambda i, j: (li, 0, jnp.minimum(j, n_real - 1)))
    in_specs = [pl.BlockSpec((tm, k), lambda i, j: (i, 0)), w_spec, w_spec]
    args = [h, w1, w3]
    if row_ssq is not None:
        in_specs.append(pl.BlockSpec((tm, row_ssq.shape[1]), lambda i, j: (i, 0)))
        args.append(row_ssq)
    return pl.pallas_call(
        functools.partial(_swiglu_kernel, n_real=n_real, norm_width=k if row_ssq is not None else 0),
        grid=(m // tm, n_pad // tn),
        in_specs=in_specs,
        out_specs=pl.BlockSpec((tm, tn), lambda i, j: (i, j)),
        out_shape=jax.ShapeDtypeStruct((m, n_pad), BF16),
        compiler_params=_params("parallel", "arbitrary"),
        name="swiglu_up",
    )(*args)


def _down_kernel(a_ref, w_ref, x_ref, g_ref, o_ref, xg_ref, part_ref, *, scale, tail_rows):
    kk = pl.program_id(2)
    last = pl.num_programs(2) - 1

    @pl.when(kk == 0)
    def _():
        o_ref[...] = x_ref[...]

    @pl.when(kk < last)
    def _():
        o_ref[...] += scale * jnp.dot(a_ref[...], w_ref[...].astype(BF16), preferred_element_type=F32)

    @pl.when(kk == last)
    def _():
        w = w_ref[...]
        row = lax.broadcasted_iota(jnp.int32, w.shape, 0)
        w = jnp.where(row < tail_rows, w, 0.0).astype(BF16)
        res = o_ref[...] + scale * jnp.dot(a_ref[...], w, preferred_element_type=F32)
        o_ref[...] = res
        _emit_norm_inputs(res, g_ref, xg_ref, part_ref)


def down_resid(act, w, li, x, next_g, *, scale):
    m, k_pad = act.shape
    k, n = w.shape[1:]
    tm = _row_tile(m, (2048,) + ROW_TILES)
    tn = min(n, 1024)
    tk = DOWN_TK
    assert k_pad % tk == 0 and k_pad - k < tk
    tile = pl.BlockSpec((tm, tn), lambda i, j, kk: (i, j))
    return pl.pallas_call(
        functools.partial(_down_kernel, scale=scale, tail_rows=tk - (k_pad - k)),
        grid=(m // tm, n // tn, k_pad // tk),
        in_specs=[pl.BlockSpec((tm, tk), lambda i, j, kk: (i, kk)),
                  pl.BlockSpec((None, tk, tn), lambda i, j, kk: (li, kk, j)),
                  tile, pl.BlockSpec((1, tn), lambda i, j, kk: (0, j))],
        out_specs=[tile, tile, pl.BlockSpec((tm, LANES), lambda i, j, kk: (i, 0))],
        out_shape=[jax.ShapeDtypeStruct((m, n), F32), jax.ShapeDtypeStruct((m, n), BF16),
                   jax.ShapeDtypeStruct((m, LANES), F32)],
        compiler_params=_params("parallel", "arbitrary", "arbitrary"),
        name="down_resid",
    )(act, w, x, next_g.reshape(1, n))


def macaron_half(x, hn, w1, w3, w2, li, next_g):
    n_pad = -(-w1.shape[2] // DOWN_TK) * DOWN_TK
    act = swiglu_up(hn[0], w1, w3, li, n_pad=n_pad, row_ssq=hn[1])
    x, xg, parts = down_resid(act, w2, li, x, next_g, scale=0.5)
    return x, (xg, parts)


def ple_add(x, hn, p, w_gate, w_proj, li, next_g):
    res = linear(hn[0], w_gate, li, tn=256, mode="ple", extras=(x, p.astype(BF16), w_proj),
                 row_ssq=hn[1], next_g=next_g)
    return (res, None) if next_g is None else (res[0], (res[1], res[2]))


def t5_bucket(dist):
    max_exact = NUM_BUCKETS // 2
    d = jnp.maximum(dist, 0)
    ratio = jnp.log(jnp.maximum(d, 1).astype(F32) / max_exact) / math.log(MAX_DISTANCE / max_exact)
    large = jnp.minimum(max_exact + (ratio * (NUM_BUCKETS - max_exact)).astype(jnp.int32), NUM_BUCKETS - 1)
    return jnp.where(d < max_exact, d, large)


def bias_of_dist(bias_cols, dist, valid):
    onehot = t5_bucket(dist)[..., None] == jnp.arange(NUM_BUCKETS)
    b = jnp.sum(jnp.where(onehot[..., None], bias_cols.astype(F32), 0.0), axis=-2)
    b = jnp.where(valid[..., None], b, -jnp.inf)
    return jnp.moveaxis(b, -1, 0)


def sample_bias(bias_cols, base, t, s_pad, valid_fn):
    n = jnp.arange(s_pad + t - 1)
    dist = base + (t - 1) - n
    table = bias_of_dist(bias_cols, dist, valid_fn(dist))
    return jnp.stack([table[:, t - 1 - i:t - 1 - i + s_pad] for i in range(t)], axis=1)


def _toeplitz_kernel(w_ref, o_ref):
    nd, tq, _ = o_ref.shape
    for d in range(nd):
        x = jnp.broadcast_to(w_ref[d], (tq, 2 * tq))
        r = pltpu.roll(x, 0, 1, stride=1, stride_axis=0)
        o_ref[d] = r[:, tq:]


def toeplitz_bias_tiles(bias_cols, n_delta, tq, step, max_steps, transposed=False):
    h = bias_cols.shape[1]
    lane = jnp.arange(2 * tq)[None, :]
    steps = jnp.arange(n_delta)[:, None] * tq + (lane - tq if transposed else tq - lane)
    rows = bias_of_dist(bias_cols, steps * step, (steps >= 0) & (steps <= max_steps))
    return pl.pallas_call(
        _toeplitz_kernel,
        grid=(h,),
        in_specs=[pl.BlockSpec((None, n_delta, 1, 2 * tq), lambda i: (i, 0, 0, 0))],
        out_specs=pl.BlockSpec((None, n_delta, tq, tq), lambda i: (i, 0, 0, 0)),
        out_shape=jax.ShapeDtypeStruct((h, n_delta, tq, tq), F32),
        compiler_params=_params("parallel"),
        name="toeplitz_bias_tiles",
    )(rows.reshape(h, n_delta, 1, 2 * tq))


def _flash_init(m_ref, l_ref, acc_ref):
    m_ref[...] = jnp.full(m_ref.shape, NEG_INIT, F32)
    l_ref[...] = jnp.zeros(l_ref.shape, F32)
    acc_ref[...] = jnp.zeros(acc_ref.shape, F32)


def _flash_step(q_ref, k_at, v_at, bias_at, sel, m_ref, l_ref, acc_ref, *, hq, hkv):
    rep = hq // hkv
    tq = q_ref.shape[0]
    scale = HEAD_DIM ** -0.5
    stack = lambda parts: parts[0] if rep == 1 else jnp.concatenate(parts, axis=0)
    sel_rows = None if sel is None else stack([sel] * rep)
    for g in range(hkv):
        heads = range(g * rep, (g + 1) * rep)
        cols = [slice(h * HEAD_DIM, (h + 1) * HEAD_DIM) for h in heads]
        qs = stack([q_ref[:, c] for c in cols]).astype(BF16)
        s = lax.dot_general(qs, k_at(g), NT_DIMS, preferred_element_type=F32) * scale
        s = s + stack([bias_at(h) for h in heads])
        if sel_rows is not None:
            s = jnp.where(sel_rows, s, -jnp.inf)
        gsl = slice(g * rep, (g + 1) * rep)
        m_old = m_ref[gsl].reshape(rep * tq, LANES)
        m_new = jnp.maximum(m_old, jnp.max(s, axis=1, keepdims=True))
        alpha = jnp.exp(m_old - m_new)
        p = jnp.exp(s - m_new[:, :1])
        l_new = alpha * l_ref[gsl].reshape(rep * tq, LANES) + jnp.sum(p, axis=1, keepdims=True)
        pv = jnp.dot(p.astype(BF16), v_at(g), preferred_element_type=F32)
        for r, c in enumerate(cols):
            rows = slice(r * tq, (r + 1) * tq)
            acc_ref[:, c] = alpha[rows] * acc_ref[:, c] + pv[rows]
        l_ref[gsl] = l_new.reshape(rep, tq, LANES)
        m_ref[gsl] = m_new.reshape(rep, tq, LANES)


def _flash_finish(o_ref, lse_ref, m_ref, l_ref, acc_ref, *, hq):
    for h in range(hq):
        hs = slice(h * HEAD_DIM, (h + 1) * HEAD_DIM)
        l = l_ref[h]
        o_ref[:, hs] = (acc_ref[:, hs] / l).astype(o_ref.dtype)
        if lse_ref is not None:
            lse_ref[:, hs] = m_ref[h] + jnp.log(l)


def _flash_scratch(tq, hq):
    return [pltpu.VMEM((hq, tq, LANES), F32), pltpu.VMEM((hq, tq, LANES), F32),
            pltpu.VMEM((tq, hq * HEAD_DIM), F32)]


TN_DIMS = (((0,), (0,)), ((), ()))


def _dsa_prompt_kernel(q_ref, k_ref, v_ref, bias_ref, sel_ref, o_ref, m_ref, l_ref, acc_ref):
    qb, kk = pl.program_id(1), pl.program_id(2)
    tq = q_ref.shape[0]
    rep = H_A // H_KV_A
    scale = HEAD_DIM ** -0.5

    @pl.when(kk == 0)
    def _():
        m_ref[...] = jnp.full(m_ref.shape, NEG_INIT, F32)
        l_ref[...] = jnp.zeros(l_ref.shape, F32)
        acc_ref[...] = jnp.zeros(acc_ref.shape, F32)

    @pl.when(kk <= qb)
    def _():
        sel_t = jnp.concatenate([sel_ref[...].T] * rep, axis=1) > 0.0
        for g in range(H_KV_A):
            heads = range(g * rep, (g + 1) * rep)
            gs = slice(g * HEAD_DIM, (g + 1) * HEAD_DIM)
            qs = jnp.concatenate([q_ref[:, h * HEAD_DIM:(h + 1) * HEAD_DIM] for h in heads], axis=0)
            s = lax.dot_general(k_ref[:, gs], qs, NT_DIMS, preferred_element_type=F32)
            s = s * scale + jnp.concatenate([bias_ref[h] for h in heads], axis=1)
            s = jnp.where(sel_t, s, -jnp.inf)
            m_old = m_ref[g]
            m_new = jnp.maximum(m_old, jnp.max(s, axis=0, keepdims=True))
            alpha = jnp.exp(m_old - m_new)
            p = jnp.exp(s - m_new)
            l_ref[g] = alpha * l_ref[g] + jnp.sum(p, axis=0, keepdims=True)
            pv = lax.dot_general(v_ref[:, gs], p.astype(BF16), TN_DIMS, preferred_element_type=F32)
            acc_ref[g] = alpha * acc_ref[g] + pv
            m_ref[g] = m_new

    @pl.when(kk == pl.num_programs(2) - 1)
    def _():
        for g in range(H_KV_A):
            out_t = acc_ref[g] / l_ref[g]
            for r in range(rep):
                h = g * rep + r
                o_ref[:, h * HEAD_DIM:(h + 1) * HEAD_DIM] = out_t[:, r * tq:(r + 1) * tq].T.astype(o_ref.dtype)


def dsa_prompt_attention(q, kvb, bias_lib_t, sel, *, batch, seq, tq):
    nq = seq // tq
    rep = H_A // H_KV_A
    kb_of = lambda qb, kk: jnp.minimum(kk, qb)
    kvw = H_KV_A * HEAD_DIM
    in_specs = [
        pl.BlockSpec((tq, Q_A_W), lambda b, qb, kk: (b * nq + qb, 0)),
        pl.BlockSpec((tq, kvw), lambda b, qb, kk: (b * nq + kb_of(qb, kk), 0)),
        pl.BlockSpec((tq, kvw), lambda b, qb, kk: (b * nq + kb_of(qb, kk), 1)),
        pl.BlockSpec((H_A, None, tq, tq), lambda b, qb, kk: (0, qb - kb_of(qb, kk), 0, 0)),
        pl.BlockSpec((tq, tq), lambda b, qb, kk: (b * nq + qb, kb_of(qb, kk))),
    ]
    return pl.pallas_call(
        _dsa_prompt_kernel,
        grid=(batch, nq, nq),
        in_specs=in_specs,
        out_specs=pl.BlockSpec((tq, Q_A_W), lambda b, qb, kk: (b * nq + qb, 0)),
        out_shape=jax.ShapeDtypeStruct((batch * seq, Q_A_W), BF16),
        scratch_shapes=[pltpu.VMEM((H_KV_A, 1, rep * tq), F32), pltpu.VMEM((H_KV_A, 1, rep * tq), F32),
                        pltpu.VMEM((H_KV_A, HEAD_DIM, rep * tq), F32)],
        compiler_params=_params("parallel", "arbitrary", "arbitrary"),
        name="dsa_prompt_attention",
    )(q, kvb, kvb, bias_lib_t, sel)


def _dilated_kernel(q_ref, kp_ref, kc_ref, vp_ref, vc_ref, bias_ref, o_ref, lse_ref, *, dilation, tq):
    scale = HEAD_DIM ** -0.5
    col = lax.broadcasted_iota(jnp.int32, (tq, 2 * tq), 1)
    usable = (col >= tq) | (pl.program_id(1) > 0)
    for h in range(q_ref.shape[1] // HEAD_DIM):
        hs = slice(h * HEAD_DIM, (h + 1) * HEAD_DIM)
        for r in range(dilation):
            rows = pl.ds(r, tq, stride=dilation) if dilation > 1 else slice(None)
            k = jnp.concatenate([kp_ref[rows, hs], kc_ref[rows, hs]], axis=0).astype(BF16)
            v = jnp.concatenate([vp_ref[rows, hs], vc_ref[rows, hs]], axis=0).astype(BF16)
            s = lax.dot_general(q_ref[rows, hs].astype(BF16), k, NT_DIMS, preferred_element_type=F32)
            s = jnp.where(usable, s * scale + bias_ref[h], -jnp.inf)
            m = jnp.max(s, axis=1, keepdims=True)
            p = jnp.exp(s - m)
            l = jnp.sum(p, axis=1, keepdims=True)
            o_ref[rows, hs] = jnp.dot(p.astype(BF16), v, preferred_element_type=F32) / l
            lse_ref[rows, hs] = jnp.broadcast_to(m + jnp.log(l), (tq, HEAD_DIM))


def dilated_prompt_group(q, kv, bias_lib, *, batch, seq, g, dilation, tq):
    span = tq * dilation
    nq = seq // span
    hps = H_C_G if dilation == 1 else 1
    cw = hps * HEAD_DIM
    hblocks = H_C_G // hps
    bias = jnp.concatenate([bias_lib[:, 1], bias_lib[:, 0]], axis=-1)
    prev = lambda qb: jnp.maximum(qb - 1, 0)
    blk = lambda f: pl.BlockSpec((span, cw), f)
    k0, v0 = g * hblocks, (H_C // hps) + g * hblocks
    in_specs = [
        blk(lambda b, qb, hb: (b * nq + qb, g * hblocks + hb)),
        blk(lambda b, qb, hb: (b * nq + prev(qb), k0 + hb)),
        blk(lambda b, qb, hb: (b * nq + qb, k0 + hb)),
        blk(lambda b, qb, hb: (b * nq + prev(qb), v0 + hb)),
        blk(lambda b, qb, hb: (b * nq + qb, v0 + hb)),
        pl.BlockSpec((hps, tq, 2 * tq), lambda b, qb, hb: (hb, 0, 0)),
    ]
    out_spec = blk(lambda b, qb, hb: (b * nq + qb, hb))
    return pl.pallas_call(
        functools.partial(_dilated_kernel, dilation=dilation, tq=tq),
        grid=(batch, nq, hblocks),
        in_specs=in_specs,
        out_specs=[out_spec, out_spec],
        out_shape=[jax.ShapeDtypeStruct((batch * seq, C_OUT), F32)] * 2,
        compiler_params=_params("parallel", "arbitrary", "arbitrary"),
        name="dilated_prompt_group",
    )(q, kv, kv, kv, kv, bias)


def _sample_attn_kernel(*refs, hq, hkv, n_src, has_sel, with_lse, has_table):
    if has_table:
        refs = refs[1:]
    q_ref = refs[0]
    src_refs = refs[1:1 + n_src]
    new_ref, bias_ref = refs[1 + n_src], refs[2 + n_src]
    pos = 3 + n_src
    sel_ref = None
    if has_sel:
        sel_ref = refs[pos]
        pos += 1
    o_ref = refs[pos]
    pos += 1
    lse_ref = None
    if with_lse:
        lse_ref = refs[pos]
        pos += 1
    m_ref, l_ref, acc_ref = refs[pos:pos + 3]
    kb = pl.program_id(1)
    nk = pl.num_programs(1)
    kw = hkv * HEAD_DIM
    new_rows = new_ref.shape[0]

    @pl.when(kb == 0)
    def _():
        _flash_init(m_ref, l_ref, acc_ref)

    def cached(which):
        def at(g):
            parts = [r[:, which, g, :] for r in src_refs]
            return (parts[0] if n_src == 1 else jnp.concatenate(parts, axis=0)).astype(BF16)
        return at

    def fresh(which):
        return lambda g: new_ref[:, which * kw + g * HEAD_DIM:which * kw + (g + 1) * HEAD_DIM].astype(BF16)

    @pl.when(kb < nk - 1)
    def _():
        sel = None if sel_ref is None else sel_ref[...] > 0.0
        _flash_step(q_ref, cached(0), cached(1), lambda h: bias_ref[h], sel,
                    m_ref, l_ref, acc_ref, hq=hq, hkv=hkv)

    @pl.when(kb == nk - 1)
    def _():
        sel = None if sel_ref is None else sel_ref[:, :new_rows] > 0.0
        _flash_step(q_ref, fresh(0), fresh(1), lambda h: bias_ref[h, :, :new_rows],
                    sel, m_ref, l_ref, acc_ref, hq=hq, hkv=hkv)
        _flash_finish(o_ref, lse_ref, m_ref, l_ref, acc_ref, hq=hq)


def sample_attention(q, src, li, new_kv, bias, sel, *, hq, hkv, page_table=None, tk, with_lse):
    bsz, t, _ = q.shape
    kvw = 2 * hkv * HEAD_DIM
    new_rows = new_kv.shape[1]
    paged = page_table is not None
    if paged:
        n_src = tk // PAGE_SIZE
        nk = page_table.shape[1] // n_src
        src_specs = [pl.BlockSpec((None, None, PAGE_SIZE, 2, hkv, HEAD_DIM),
                                  functools.partial(lambda p, b, kb, pt: (li, pt[b, jnp.minimum(kb, nk - 1) * n_src + p], 0, 0, 0, 0), p))
                     for p in range(n_src)]
        ix = lambda f: (lambda b, kb, pt: f(b, kb))
    else:
        n_src = 1
        nk = src.shape[2] // tk
        src_specs = [pl.BlockSpec((None, None, tk, 2, hkv, HEAD_DIM),
                                  lambda b, kb: (li, b, jnp.minimum(kb, nk - 1), 0, 0, 0))]
        ix = lambda f: f
    in_specs = ([pl.BlockSpec((None, t, hq * HEAD_DIM), ix(lambda b, kb: (b, 0, 0)))] + src_specs +
                [pl.BlockSpec((None, new_rows, kvw), ix(lambda b, kb: (b, 0, 0))),
                 pl.BlockSpec((hq, t, tk), ix(lambda b, kb: (0, 0, kb)))])
    args = [q] + [src] * n_src + [new_kv, bias]
    if sel is not None:
        in_specs.append(pl.BlockSpec((None, t, tk), ix(lambda b, kb: (b, 0, kb))))
        args.append(sel)
    out_spec = pl.BlockSpec((None, t, hq * HEAD_DIM), ix(lambda b, kb: (b, 0, 0)))
    n_out = 2 if with_lse else 1
    kern = functools.partial(_sample_attn_kernel, hq=hq, hkv=hkv, n_src=n_src, has_sel=sel is not None,
                             with_lse=with_lse, has_table=paged)
    out_shape = [jax.ShapeDtypeStruct((bsz, t, hq * HEAD_DIM), F32)] * n_out
    scratch = _flash_scratch(t, hq)
    cp = _params("parallel", "arbitrary")
    if paged:
        gs = pltpu.PrefetchScalarGridSpec(num_scalar_prefetch=1, grid=(bsz, nk + 1), in_specs=in_specs,
                                          out_specs=[out_spec] * n_out, scratch_shapes=scratch)
        res = pl.pallas_call(kern, grid_spec=gs, out_shape=out_shape, compiler_params=cp,
                             name="sample_attention_paged")(page_table, *args)
    else:
        res = pl.pallas_call(kern, grid=(bsz, nk + 1), in_specs=in_specs, out_specs=[out_spec] * n_out,
                             out_shape=out_shape, scratch_shapes=scratch, compiler_params=cp,
                             name="sample_attention")(*args)
    return res if with_lse else res[0]


IDX_SCALE = (H_IDX ** -0.5) * (D_IDX ** -0.5)


def _idx_prompt_kernel(qi_ref, ki_ref, wi_ref, o_ref, *, tq):
    qb, kb = pl.program_id(1), pl.program_id(2)

    @pl.when(kb > qb)
    def _():
        o_ref[...] = jnp.full(o_ref.shape, -jnp.inf, F32)

    @pl.when(kb <= qb)
    def _():
        kt = ki_ref[...].astype(BF16)
        w = wi_ref[...] * IDX_SCALE
        acc = jnp.zeros((tq, tq), F32)
        for h in range(H_IDX):
            s = lax.dot_general(qi_ref[:, h * D_IDX:(h + 1) * D_IDX], kt, NT_DIMS, preferred_element_type=F32)
            acc = acc + jnp.maximum(s, 0.0) * w[:, h:h + 1]
        row = lax.broadcasted_iota(jnp.int32, (tq, tq), 0) + qb * tq
        col = lax.broadcasted_iota(jnp.int32, (tq, tq), 1) + kb * tq
        o_ref[...] = jnp.where(col <= row, acc, -jnp.inf)


def indexer_prompt(qi, rest, *, batch, seq, tq, ki_blk, wi_blk):
    nq = seq // tq
    return pl.pallas_call(
        functools.partial(_idx_prompt_kernel, tq=tq),
        grid=(batch, nq, nq),
        in_specs=[pl.BlockSpec((tq, H_IDX * D_IDX), lambda b, qb, kb: (b * nq + qb, 0)),
                  pl.BlockSpec((tq, D_IDX), lambda b, qb, kb: (b * nq + jnp.minimum(kb, qb), ki_blk)),
                  pl.BlockSpec((tq, LANES), lambda b, qb, kb: (b * nq + qb, wi_blk))],
        out_specs=pl.BlockSpec((tq, tq), lambda b, qb, kb: (b * nq + qb, kb)),
        out_shape=jax.ShapeDtypeStruct((batch * seq, seq), F32),
        compiler_params=_params("parallel", "parallel", "arbitrary"),
        name="indexer_prompt",
    )(qi, rest, rest)


def _split_dot_nt(e, x):
    hi = x.astype(BF16)
    lo = (x - hi.astype(F32)).astype(BF16)
    return (lax.dot_general(e, hi, NT_DIMS, preferred_element_type=F32) +
            lax.dot_general(e, lo, NT_DIMS, preferred_element_type=F32))


def _idx_sample_kernel(pt_ref, qit_ref, w_ref, e_ref, *refs, n_src, t):
    del pt_ref
    src_refs = refs[:n_src]
    new_ref, o_ref = refs[n_src], refs[n_src + 1]
    kb = pl.program_id(1)
    nk = pl.num_programs(1)
    tk = o_ref.shape[1]
    new_rows = new_ref.shape[0]

    def scores(keys):
        st = jnp.dot(keys.astype(BF16), qit_ref[...], preferred_element_type=F32)
        x = jnp.maximum(st, 0.0) * w_ref[...]
        return _split_dot_nt(e_ref[...], x)[:t]

    @pl.when(kb < nk - 1)
    def _():
        o_ref[...] = scores(jnp.concatenate([r[...] for r in src_refs], axis=0))

    @pl.when(kb == nk - 1)
    def _():
        sc = scores(new_ref[...])
        row = lax.broadcasted_iota(jnp.int32, (t, new_rows), 0)
        col = lax.broadcasted_iota(jnp.int32, (t, new_rows), 1)
        o_ref[...] = jnp.full(o_ref.shape, -jnp.inf, F32)
        o_ref[:, :new_rows] = jnp.where(col <= row, sc, -jnp.inf)


def indexer_sample(qi, wi, ki_new, pool_kidx, li, page_table, *, tk):
    bsz, t, _ = qi.shape
    n_src = tk // PAGE_SIZE
    nk = page_table.shape[1] // n_src
    qit = qi.reshape(bsz, t * H_IDX, D_IDX).transpose(0, 2, 1)
    w = (wi * IDX_SCALE).reshape(bsz, 1, t * H_IDX)
    e_rows = 2 * SUBLANES
    e = jnp.repeat(jnp.eye(e_rows, t, dtype=BF16), H_IDX, axis=1)
    src_specs = [pl.BlockSpec((None, None, PAGE_SIZE, D_IDX),
                              functools.partial(lambda p, b, kb, pt: (li, pt[b, jnp.minimum(kb, nk - 1) * n_src + p], 0, 0), p))
                 for p in range(n_src)]
    gs = pltpu.PrefetchScalarGridSpec(
        num_scalar_prefetch=1, grid=(bsz, nk + 1),
        in_specs=[pl.BlockSpec((None, D_IDX, t * H_IDX), lambda b, kb, pt: (b, 0, 0)),
                  pl.BlockSpec((None, 1, t * H_IDX), lambda b, kb, pt: (b, 0, 0)),
                  pl.BlockSpec((e_rows, t * H_IDX), lambda b, kb, pt: (0, 0))] + src_specs +
                 [pl.BlockSpec((None, ki_new.shape[1], D_IDX), lambda b, kb, pt: (b, 0, 0))],
        out_specs=pl.BlockSpec((None, t, tk), lambda b, kb, pt: (b, 0, kb)))
    return pl.pallas_call(
        functools.partial(_idx_sample_kernel, n_src=n_src, t=t),
        grid_spec=gs,
        out_shape=jax.ShapeDtypeStruct((bsz, t, (nk + 1) * tk), F32),
        compiler_params=_params("parallel", "arbitrary"),
        name="indexer_sample",
    )(page_table, qit, w, e, *([pool_kidx] * n_src), ki_new)


TOPK_CHUNK = 512
INT_MIN = -2 ** 31


def _topk_kernel(s_ref, u_ref, o_ref, *, k, causal_tile):
    n = s_ref.shape[1]
    if causal_tile is None:
        _topk_rows(s_ref, u_ref, o_ref, k, n)
        return
    nb = n // causal_tile
    qb = pl.program_id(0) % nb
    widths = sorted({min(n, -(-(w * causal_tile) // TOPK_CHUNK) * TOPK_CHUNK) for w in range(1, nb + 1)})
    for width in widths:
        lo = 0 if width == widths[0] else widths[widths.index(width) - 1]

        @pl.when(((qb + 1) * causal_tile > lo) & ((qb + 1) * causal_tile <= width))
        def _(width=width):
            _topk_rows(s_ref, u_ref, o_ref, k, width)
            if width < n:
                o_ref[:, width:] = jnp.zeros((o_ref.shape[0], n - width), F32)


def _topk_rows(s_ref, u_ref, o_ref, k, n):
    assert n >= k
    x = s_ref[:, :n]
    rows = x.shape[0]
    bits = lax.bitcast_convert_type(x, jnp.int32)
    key = bits ^ ((bits >> 31) & jnp.int32(0x7FFFFFFF))
    kf = jnp.float32(k)

    def body(it, tau_u):
        cand_u = tau_u | jnp.left_shift(jnp.int32(1), 31 - it)
        cand = cand_u ^ jnp.int32(INT_MIN)
        cnt = jnp.sum(jnp.where(key >= cand, 1.0, 0.0), axis=1, keepdims=True)
        return jnp.where(cnt >= kf, cand_u, tau_u)

    tau_u = lax.fori_loop(0, 32, body, jnp.zeros((rows, 1), jnp.int32))
    tau = tau_u ^ jnp.int32(INT_MIN)
    gt = key > tau
    eq = key == tau
    need = kf - jnp.sum(jnp.where(gt, 1.0, 0.0), axis=1, keepdims=True)
    run = jnp.zeros((rows, 1), F32)
    for c in range(n // TOPK_CHUNK):
        cs = slice(c * TOPK_CHUNK, (c + 1) * TOPK_CHUNK)
        e = jnp.where(eq[:, cs], 1.0, 0.0)
        before = jnp.dot(e.astype(BF16), u_ref[...], preferred_element_type=F32) + run
        o_ref[:, cs] = jnp.where(gt[:, cs] | (eq[:, cs] & (before < need)), 1.0, 0.0)
        run = run + jnp.sum(e, axis=1, keepdims=True)


def topk_mask(scores, k, *, tr, causal=False):
    r, s = scores.shape
    assert s % TOPK_CHUNK == 0 and r % tr == 0 and (not causal or s % tr == 0)
    ii = jnp.arange(TOPK_CHUNK)
    upper = (ii[:, None] < ii[None, :]).astype(BF16)
    return pl.pallas_call(
        functools.partial(_topk_kernel, k=k, causal_tile=tr if causal else None),
        grid=(r // tr,),
        in_specs=[pl.BlockSpec((tr, s), lambda i: (i, 0)),
                  pl.BlockSpec((TOPK_CHUNK, TOPK_CHUNK), lambda i: (0, 0))],
        out_specs=pl.BlockSpec((tr, s), lambda i: (i, 0)),
        out_shape=jax.ShapeDtypeStruct((r, s), F32),
        compiler_params=_params("parallel"),
        name="topk_mask",
    )(scores, upper)


def _s5_param_kernel(lr_ref, li_ref, ls_ref, br_ref, bi_ref, abr_ref, abi_ref, bbr_ref, bbi_ref):
    lr, li = lr_ref[...], li_ref[...]
    step = jnp.exp(ls_ref[...])
    mag = jnp.exp(lr * step)
    ab_re, ab_im = mag * jnp.cos(li * step), mag * jnp.sin(li * step)
    den = lr * lr + li * li
    nr, ni = ab_re - 1.0, ab_im
    f_re, f_im = (nr * lr + ni * li) / den, (ni * lr - nr * li) / den
    br, bi = br_ref[...], bi_ref[...]
    abr_ref[...] = ab_re
    abi_ref[...] = ab_im
    bbr_ref[...] = f_re * br - f_im * bi
    bbi_ref[...] = f_re * bi + f_im * br


def s5_params(lam_re, lam_im, log_step, b_re, b_im):
    w = B_STATE * B_GROUP
    rep = lambda a: jnp.repeat(a, B_GROUP, axis=1)
    ls = jnp.broadcast_to(log_step[:, None], (B_GROUPS, w))
    shp = jax.ShapeDtypeStruct((B_GROUPS, w), F32)
    abr, abi, bbr, bbi = pl.pallas_call(_s5_param_kernel, out_shape=[shp] * 4, name="s5_params")(
        rep(lam_re), rep(lam_im), ls, b_re.reshape(B_GROUPS, w), b_im.reshape(B_GROUPS, w))
    return (abr[:, ::B_GROUP], abi[:, ::B_GROUP],
            bbr.reshape(B_GROUPS, B_STATE, B_GROUP), bbi.reshape(B_GROUPS, B_STATE, B_GROUP))


def _cmul(ar, ai, br, bi):
    return ar * br - ai * bi, ar * bi + ai * br


def _s5_kernel(u_ref, bbr_ref, bbi_ref, ccr_ref, cci_ref, ar_ref, ai_ref, d_ref, h0r_ref, h0i_ref,
               g_ref, hfr_ref, hfi_ref, xr_ref, xi_ref, *, seg_len, precise):
    u = u_ref[...]
    if precise:
        mm = lambda a, b: jnp.dot(a, b, preferred_element_type=F32, precision=lax.Precision.HIGHEST)
    else:
        mm = lambda a, b: jnp.dot(a.astype(BF16), b.astype(BF16), preferred_element_type=F32)
    xr_ref[...] = mm(u, bbr_ref[...])
    xi_ref[...] = mm(u, bbi_ref[...])
    sw = xr_ref.shape[1]
    ar = jnp.broadcast_to(ar_ref[...], (S5_SEG, sw))
    ai = jnp.broadcast_to(ai_ref[...], (S5_SEG, sw))

    def rows(j):
        return pl.ds(pl.multiple_of(j * S5_SEG, S5_SEG), S5_SEG)

    def local_scan(j, carry):
        hr, hi = _cmul(ar, ai, *carry)
        hr = hr + xr_ref[rows(j), :]
        hi = hi + xi_ref[rows(j), :]
        xr_ref[rows(j), :] = hr
        xi_ref[rows(j), :] = hi
        return hr, hi

    zero = jnp.zeros((S5_SEG, sw), F32)
    unroll = min(seg_len, S5_UNROLL)
    fr, fi = lax.fori_loop(0, seg_len, local_scan, (zero, zero), unroll=unroll)

    pr, pi = ar[:1], ai[:1]
    for _ in range(int(math.log2(seg_len))):
        pr, pi = _cmul(pr, pi, pr, pi)

    cr, ci = h0r_ref[...], h0i_ref[...]
    crs, cis = [], []
    for s in range(S5_SEG):
        crs.append(cr)
        cis.append(ci)
        tr, ti = _cmul(pr, pi, cr, ci)
        cr, ci = fr[s:s + 1] + tr, fi[s:s + 1] + ti
    hfr_ref[...] = cr
    hfi_ref[...] = ci
    c_re = jnp.concatenate(crs, axis=0)
    c_im = jnp.concatenate(cis, axis=0)

    def add_carry(j, pw):
        tr, ti = _cmul(pw[0], pw[1], c_re, c_im)
        xr_ref[rows(j), :] += tr
        xi_ref[rows(j), :] += ti
        return _cmul(pw[0], pw[1], ar, ai)

    lax.fori_loop(0, seg_len, add_carry, (ar, ai), unroll=unroll)

    y = mm(xr_ref[...], ccr_ref[...]) - mm(xi_ref[...], cci_ref[...]) + d_ref[...] * u
    g_ref[...] = jax.nn.gelu(y)


def s5_scan(u_perm, h0_re, h0_im, ab_re, ab_im, bb_re, bb_im, c_re, c_im, d_skip, *, batch, seq, precise):
    seg_len = seq // S5_SEG
    assert seg_len & (seg_len - 1) == 0
    nch = B_GROUPS // S5_CHUNK_GROUPS
    uw, sw = S5_CHUNK_GROUPS * B_GROUP, S5_CHUNK_GROUPS * B_STATE
    eye = jnp.eye(S5_CHUNK_GROUPS, dtype=F32)

    def in_blockdiag(bb):
        bb = bb.reshape(nch, S5_CHUNK_GROUPS, B_STATE, B_GROUP)
        return jnp.einsum("kgnc,gh->kgchn", bb, eye).reshape(nch, uw, sw)

    def out_blockdiag(cc):
        cc = cc.reshape(nch, S5_CHUNK_GROUPS, B_GROUP, B_STATE)
        return jnp.einsum("kgcn,gh->kgnhc", cc, eye).reshape(nch, sw, uw)

    row = lambda a: a.reshape(nch, 1, sw)
    st = lambda a: a.astype(F32).reshape(batch, nch, 1, sw)
    w_in = pl.BlockSpec((None, uw, sw), lambda b, c: (c, 0, 0))
    w_out = pl.BlockSpec((None, sw, uw), lambda b, c: (c, 0, 0))
    a_spec = pl.BlockSpec((None, 1, sw), lambda b, c: (c, 0, 0))
    st_spec = pl.BlockSpec((None, None, 1, sw), lambda b, c: (b, c, 0, 0))
    u_spec = pl.BlockSpec((seq, uw), lambda b, c: (b, c))
    g, hfr, hfi = pl.pallas_call(
        functools.partial(_s5_kernel, seg_len=seg_len, precise=precise),
        grid=(batch, nch),
        in_specs=[u_spec, w_in, w_in, w_out, w_out, a_spec, a_spec,
                  pl.BlockSpec((1, uw), lambda b, c: (0, c)), st_spec, st_spec],
        out_specs=[u_spec, st_spec, st_spec],
        out_shape=[jax.ShapeDtypeStruct((batch * seq, B_WIDTH), F32),
                   jax.ShapeDtypeStruct((batch, nch, 1, sw), F32),
                   jax.ShapeDtypeStruct((batch, nch, 1, sw), F32)],
        scratch_shapes=[pltpu.VMEM((seq, sw), F32), pltpu.VMEM((seq, sw), F32)],
        compiler_params=_params("parallel", "parallel"),
        name="s5_scan",
    )(u_perm, in_blockdiag(bb_re), in_blockdiag(bb_im), out_blockdiag(c_re), out_blockdiag(c_im),
      row(ab_re), row(ab_im), d_skip.reshape(1, B_WIDTH), st(h0_re), st(h0_im))
    return g, hfr.reshape(batch, B_GROUPS, B_STATE), hfi.reshape(batch, B_GROUPS, B_STATE)


def to_segment_order(a, batch, seq):
    seg_len = seq // S5_SEG
    return a.reshape(batch, S5_SEG, seg_len, -1).swapaxes(1, 2).reshape(batch * seq, -1)


def from_segment_order(a, batch, seq):
    seg_len = seq // S5_SEG
    return a.reshape(batch, seg_len, S5_SEG, -1).swapaxes(1, 2).reshape(batch * seq, -1)


def _merge_kernel(o0, o1, o2, l0, l1, l2, out_ref):
    la, lb, lc = l0[...], l1[...], l2[...]
    m = jnp.maximum(jnp.maximum(la, lb), lc)
    wa, wb, wc = jnp.exp(la - m), jnp.exp(lb - m), jnp.exp(lc - m)
    out = (wa * o0[...] + wb * o1[...] + wc * o2[...]) / (wa + wb + wc)
    out_ref[...] = out.astype(out_ref.dtype)


def merge_groups(outs, lses):
    m, w = outs[0].shape
    tm = min(m, 512)
    spec = pl.BlockSpec((tm, w), lambda i: (i, 0))
    return pl.pallas_call(
        _merge_kernel, grid=(m // tm,), in_specs=[spec] * 6, out_specs=spec,
        out_shape=jax.ShapeDtypeStruct((m, w), BF16), compiler_params=_params("parallel"),
        name="merge_groups",
    )(*outs, *lses)


DSA_TQ = 256
C_TQ = 128
SAMPLE_TK = PAGES_PER_STEP * PAGE_SIZE
NEW_ROWS = 128


def _pad_rows(a, rows):
    return jnp.pad(a, ((0, 0), (0, rows - a.shape[1]), (0, 0)))


def even_mixer(x, hn, *, next_g, dims, li, w_in, w_rest, w_idx_qb, w_out, w_glu, b_glu, s5c, bias_table, h0_re,
               h0_im, pool_kv, pool_kidx, page_table):
    bp, tp, bs, ts = dims
    mp, ms = bp * tp, bs * ts
    ab_re, ab_im, bb_re, bb_im, c_re, c_im, d_skip = s5c

    h, ssq = hn
    q = linear(h, w_in, li, n_cols=Q_A_W, col_off=0, tn=512, out_dtypes=(BF16,), row_ssq=ssq)
    kv, kvb = linear(h, w_in, li, n_cols=KV_A_W, col_off=OFF_KV, tn=512, out_dtypes=(F32, BF16), row_ssq=ssq)
    cq = linear(h, w_in, li, n_cols=R_IDX, col_off=OFF_CQ, tn=512, out_dtypes=(BF16,), row_ssq=ssq)
    rest = linear(h, w_rest, 0, tn=256, row_ssq=ssq)
    u, ki, wi = rest[:, :B_WIDTH], rest[:, B_WIDTH:B_WIDTH + D_IDX], rest[:, B_WIDTH + D_IDX:B_WIDTH + D_IDX + H_IDX]
    qi = linear(cq, w_idx_qb, li, tn=512, out_dtypes=(BF16,))

    scores = indexer_prompt(qi, rest, batch=bp, seq=tp, tq=DSA_TQ,
                            ki_blk=B_WIDTH // LANES, wi_blk=B_WIDTH // LANES + 1)
    sel = topk_mask(scores, min(TOPK_MAX, tp // 4), tr=DSA_TQ, causal=True)
    lib = toeplitz_bias_tiles(bias_table, tp // DSA_TQ, DSA_TQ, 1, tp, transposed=True)
    att_p = dsa_prompt_attention(q, kvb, lib, sel, batch=bp, seq=tp, tq=DSA_TQ)
    zeros = jnp.zeros((bp, B_GROUPS, B_STATE), F32)
    g_perm, hp_re, hp_im = s5_scan(to_segment_order(u[:mp], bp, tp), zeros, zeros, ab_re, ab_im, bb_re, bb_im,
                                   c_re, c_im, d_skip, batch=bp, seq=tp, precise=False)
    g_p = from_segment_order(g_perm, bp, tp)

    ki_new = _pad_rows(ki[mp:].reshape(bs, ts, D_IDX), NEW_ROWS)
    scores = indexer_sample(qi[mp:].reshape(bs, ts, H_IDX * D_IDX), wi[mp:].reshape(bs, ts, H_IDX),
                            ki_new, pool_kidx, li, page_table, tk=SAMPLE_TK)
    s_pad = scores.shape[-1]
    sel = topk_mask(scores.reshape(ms, s_pad), min(TOPK_MAX, (PAST_LEN + ts) // 4), tr=ms).reshape(bs, ts, s_pad)
    bias = sample_bias(bias_table, PAST_LEN, ts, s_pad, lambda d: d >= 0)
    kv_new = _pad_rows(kv[mp:].reshape(bs, ts, KV_A_W), NEW_ROWS)
    att_s = sample_attention(q[mp:].astype(F32).reshape(bs, ts, Q_A_W), pool_kv, li, kv_new, bias, sel,
                             hq=H_A, hkv=H_KV_A, page_table=page_table, tk=SAMPLE_TK, with_lse=False)
    g_s, hs_re, hs_im = s5_scan(u[mp:], h0_re, h0_im, ab_re, ab_im, bb_re, bb_im,
                                c_re, c_im, d_skip, batch=bs, seq=ts, precise=True)

    att = jnp.concatenate([att_p, att_s.reshape(ms, Q_A_W).astype(BF16)], axis=0)
    g = jnp.concatenate([g_p, g_s], axis=0)
    ssm = linear(g.astype(BF16), w_glu, li, tn=512, mode="glu", extras=(g, b_glu.reshape(1, B_WIDTH)),
                 out_dtypes=(BF16,))
    x = linear(att, w_out, li, row_off=0, tn=512, mode="resid", extras=(x,))
    x, xg, parts = linear(ssm, w_out, li, row_off=Q_A_W, tn=512, mode="resid", extras=(x,), next_g=next_g)
    return x, (xg, parts), kv, ki, (hp_re, hp_im), (hs_re, hs_im)


def odd_mixer(x, hn, li, w_in, w_out, bias_table, *, next_g, dims, caches):
    bp, tp, bs, ts = dims
    mp, ms = bp * tp, bs * ts
    h, ssq = hn
    q = linear(h, w_in, li, n_cols=C_WIDTH, col_off=0, tn=512, row_ssq=ssq)
    kv = linear(h, w_in, li, n_cols=2 * C_WIDTH, col_off=C_WIDTH, tn=512, row_ssq=ssq)
    kv5_s = kv[mp:].reshape(bs, ts, 2, H_C, HEAD_DIM)
    outs_p, lses_p, outs_s, lses_s, rows_p, rows_s = [], [], [], [], [], []
    for g, (window, dilation) in enumerate(C_PAIRS):
        cols = bias_table[:, g * H_C_G:(g + 1) * H_C_G]
        heads = slice(g * H_C_G, (g + 1) * H_C_G)
        lib = toeplitz_bias_tiles(cols, 2, C_TQ, dilation, window // dilation)
        o, l = dilated_prompt_group(q, kv, lib, batch=bp, seq=tp, g=g, dilation=dilation, tq=C_TQ)
        outs_p.append(o)
        lses_p.append(l)
        wrows = min(window, tp)
        tail = lambda c0: jnp.stack([kv[(b + 1) * tp - wrows:(b + 1) * tp, c0 + g * C_OUT:c0 + (g + 1) * C_OUT]
                                     for b in range(bp)]).reshape(bp, wrows, H_C_G, HEAD_DIM)
        rows_p.append(jnp.stack([tail(0), tail(C_WIDTH)], axis=2))

        buf = caches[g]
        wb = buf.shape[2]
        tk = min(wb, 512)
        s_pad = wb + tk
        bias = sample_bias(cols, wb, ts, s_pad,
                           lambda d: (d >= 0) & (d <= window) & (d % dilation == 0))
        kv_g = kv5_s[:, :, :, heads]
        new_kv = _pad_rows(kv_g.reshape(bs, ts, 2 * C_OUT), NEW_ROWS)
        qg = q[mp:, g * C_OUT:(g + 1) * C_OUT].astype(F32).reshape(bs, ts, C_OUT)
        o, l = sample_attention(qg, buf, li, new_kv, bias, None, hq=H_C_G, hkv=H_C_G, tk=tk, with_lse=True)
        outs_s.append(o.reshape(ms, C_OUT))
        lses_s.append(l.reshape(ms, C_OUT))
        rows_s.append(kv_g)
    att = jnp.concatenate([merge_groups(outs_p, lses_p), merge_groups(outs_s, lses_s)], axis=0)
    x, xg, parts = linear(att, w_out, li, tn=512, mode="resid", extras=(x,), next_g=next_g)
    return x, (xg, parts), rows_p, rows_s


def kernel(x_prompt, x_sample, cache_a_kv, cache_a_kidx, state_b_re, state_b_im, cache_c0_kv, cache_c1_kv, cache_c2_kv, page_table, p_prompt, p_sample, bias_table, norm_g, final_g, ffn1_w1, ffn1_w3, ffn1_w2, ffn2_w1, ffn2_w3, ffn2_w2, ple_gate, ple_proj, w_in_even, w_idx_qb, w_out_even, s5_lam_re, s5_lam_im, s5_log_step, s5_b_re, s5_b_im, s5_c_re, s5_c_im, s5_d, s5_w_glu, s5_b_glu, w_in_odd, w_out_odd):
    bp, tp, _ = x_prompt.shape
    bs, ts, _ = x_sample.shape
    mp, ms = bp * tp, bs * ts
    dims = (bp, tp, bs, ts)
    x = jnp.concatenate([x_prompt.reshape(mp, D_MODEL), x_sample.reshape(ms, D_MODEL)], axis=0)
    p_all = jnp.concatenate([p_prompt.reshape(DEPTH, mp, D_PLE), p_sample.reshape(DEPTH, ms, D_PLE)], axis=1)
    outs_p = {k: [] for k in ("kv", "ki", "re", "im", "c0", "c1", "c2")}
    outs_s = {k: [] for k in ("kv", "ki", "re", "im", "c0", "c1", "c2")}
    hn = (rmsnorm(x, norm_g[0, 0], BF16), None)
    for i in range(DEPTH):
        li = i // 2
        x, hn = macaron_half(x, hn, ffn1_w1, ffn1_w3, ffn1_w2, i, norm_g[i, 1])
        if i % 2 == 0:
            w_in = w_in_even[li]
            w_rest = jnp.concatenate([w_in[:, OFF_U:], w_in[:, OFF_KI:OFF_U], w_in[:, OFF_W:OFF_KI],
                                      jnp.zeros((D_MODEL, LANES - H_IDX), w_in.dtype)], axis=1)[None]
            s5c = (*s5_params(s5_lam_re[li], s5_lam_im[li], s5_log_step[li], s5_b_re[li], s5_b_im[li]),
                   s5_c_re[li], s5_c_im[li], s5_d[li])
            x, hn, kv, ki, st_p, st_s = even_mixer(
                x, hn, next_g=norm_g[i, 2], dims=dims, li=li, w_in=w_in_even, w_rest=w_rest, w_idx_qb=w_idx_qb, w_out=w_out_even,
                w_glu=s5_w_glu, b_glu=s5_b_glu[li], s5c=s5c, bias_table=bias_table, h0_re=state_b_re[li],
                h0_im=state_b_im[li], pool_kv=cache_a_kv, pool_kidx=cache_a_kidx, page_table=page_table)
            outs_p["kv"].append(kv[:mp].reshape(bp, tp, 2, H_KV_A, HEAD_DIM))
            outs_p["ki"].append(ki[:mp].reshape(bp, tp, D_IDX))
            outs_p["re"].append(st_p[0])
            outs_p["im"].append(st_p[1])
            outs_s["kv"].append(kv[mp:].reshape(bs, ts, 2, H_KV_A, HEAD_DIM))
            outs_s["ki"].append(ki[mp:].reshape(bs, ts, D_IDX))
            outs_s["re"].append(st_s[0])
            outs_s["im"].append(st_s[1])
        else:
            x, hn, rows_p, rows_s = odd_mixer(x, hn, li, w_in_odd, w_out_odd, bias_table, next_g=norm_g[i, 2],
                                              dims=dims, caches=(cache_c0_kv, cache_c1_kv, cache_c2_kv))
            for g in range(N_C_GROUPS):
                outs_p["c%d" % g].append(rows_p[g])
                outs_s["c%d" % g].append(rows_s[g])
        x, hn = macaron_half(x, hn, ffn2_w1, ffn2_w3, ffn2_w2, i, norm_g[i, 3])
        x, hn = ple_add(x, hn, p_all[i], ple_gate, ple_proj, i, norm_g[i + 1, 0] if i + 1 < DEPTH else None)
    y_prompt = rmsnorm(x, final_g, F32, rows=mp).reshape(bp, tp, D_MODEL)
    y_sample = rmsnorm(x, final_g, F32, row_off=mp, rows=ms).reshape(bs, ts, D_MODEL)
    order = ("kv", "ki", "re", "im", "c0", "c1", "c2")
    return (y_prompt, y_sample, *[jnp.stack(outs_p[k]) for k in order], *[jnp.stack(outs_s[k]) for k in order])
```
